```python
import math
import jax
import jax.numpy as jnp
from jax import lax
import numpy as np

D_MODEL = 2048
BATCH = 2
SEQ = 8192
DEPTH = 2

GRID_W = 64
CTX_LEN = 256
EPS = 1e-6
CONV_W = 4
CONV_LEFT = 2

D_LRU = D_MODEL
LRU_BLOCKS = 16
LRU_BS = D_LRU // LRU_BLOCKS
LRU_C = 8.0

D_RWKV = D_MODEL
RWKV_HEAD = 64
RWKV_HEADS = D_RWKV // RWKV_HEAD
W_LORA = 96
A_LORA = 96
G_LORA = 256
RWKV_GN_EPS = 64e-5

D_SSM = D_MODEL
SSM_HEADDIM = 64
SSM_HEADS = D_SSM // SSM_HEADDIM
SSM_STATE = 128
SSM_GROUPS = 4
SSM_CHUNK = 128
D_XBC = D_SSM + 2 * SSM_GROUPS * SSM_STATE

D_FF = 3 * D_MODEL
N_EXPERTS = 8
TOP_K = 2
D_FF_EXPERT = D_FF // 2
N_DENSE = (DEPTH + 1) // 2
N_MOE = DEPTH // 2

N_GATE = 3 * D_MODEL
N_LRU_IN = 2 * D_LRU
N_RWKV_IN = 3 * D_RWKV + 2 * W_LORA + 2 * A_LORA + G_LORA
N_SSM_IN = D_SSM + D_XBC + 2 * SSM_HEADS
OFF_LRU = N_GATE
OFF_RWKV = OFF_LRU + N_LRU_IN
OFF_SSM = OFF_RWKV + N_RWKV_IN
N_IN = OFF_SSM + N_SSM_IN

kernel_name = "hybrid_lru_rwkv7_mamba2_moe_dit"


def rms_norm(x, gain):
    xf = x.astype(jnp.float32)
    xf = xf * lax.rsqrt(jnp.mean(xf * xf, axis=-1, keepdims=True) + EPS)
    return (xf * gain).astype(x.dtype)


def centred_dwconv(u, w, b):
    l = u.shape[1]
    up = jnp.pad(u, ((0, 0), (CONV_LEFT, CONV_W - 1 - CONV_LEFT), (0, 0)))
    out = b
    for tap in range(CONV_W):
        out = out + up[:, tap:tap + l] * w[tap]
    return out


def bidir_token_shift(u, mu):
    prev = jnp.pad(u[:, :-1], ((0, 0), (1, 0), (0, 0)))
    nxt = jnp.pad(u[:, 1:], ((0, 0), (0, 1), (0, 0)))
    return u + mu * (0.5 * (prev + nxt) - u)


def grid_transpose(u, rows, cols):
    b, l, ch = u.shape
    return u.reshape(b, rows, cols, ch).transpose(0, 2, 1, 3).reshape(b, l, ch)


def linear_scan(a, bx, h0):
    def combine(lhs, rhs):
        return lhs[0] * rhs[0], rhs[0] * lhs[1] + rhs[1]
    a_cum, h = lax.associative_scan(combine, (a, bx), axis=1)
    return h + a_cum * h0[:, None]


def rglru_direction(xc, gate_w, gate_b, lam, h0, reverse):
    if reverse:
        xc = jnp.flip(xc, axis=1)
    b, l, _ = xc.shape
    xb = xc.reshape(b, l, LRU_BLOCKS, LRU_BS)
    gates = jnp.einsum('blnk,gnkj->gblnj', xb, gate_w).reshape(2, b, l, D_LRU) + gate_b[:, None, None]
    gates = jax.nn.sigmoid(gates.astype(jnp.float32))
    log_a = -LRU_C * gates[0] * jax.nn.softplus(-lam.astype(jnp.float32))
    a = jnp.exp(log_a)
    bx = jnp.sqrt(-jnp.expm1(2.0 * log_a)) * gates[1] * xc
    h = linear_scan(a, bx, h0)
    h_last = h[:, -1]
    if reverse:
        h = jnp.flip(h, axis=1)
    return h, h_last


def rglru_mixer(p_ctx, p_lat, conv_w, conv_b, gate_w, gate_b, lam):
    xc_ctx = centred_dwconv(p_ctx[..., :D_LRU], conv_w, conv_b)
    xc_lat = centred_dwconv(p_lat[..., :D_LRU], conv_w, conv_b)
    h0 = jnp.zeros((p_ctx.shape[0], D_LRU), jnp.float32)
    y_ctx, y_lat = 0.0, 0.0
    for d in range(2):
        h_c, h_c_last = rglru_direction(xc_ctx, gate_w[d], gate_b[d], lam[d], h0, d == 1)
        h_l, _ = rglru_direction(xc_lat, gate_w[d], gate_b[d], lam[d], h_c_last, d == 1)
        y_ctx = y_ctx + h_c
        y_lat = y_lat + h_l
    return (y_ctx * jax.nn.gelu(p_ctx[..., D_LRU:]), y_lat * jax.nn.gelu(p_lat[..., D_LRU:]))


def rwkv7_features(p, mu, w0, w_up, a0, a_up, g_up, k_k, k_a):
    b, l, _ = p.shape
    p = bidir_token_shift(p, mu)
    heads = lambda t: t.reshape(b, l, RWKV_HEADS, RWKV_HEAD)
    r = p[..., 0:D_RWKV]
    k = p[..., D_RWKV:2 * D_RWKV]
    v = p[..., 2 * D_RWKV:3 * D_RWKV]
    o = 3 * D_RWKV
    w_lo = p[..., o:o + 2 * W_LORA].reshape(b, l, 2, W_LORA)
    o = o + 2 * W_LORA
    a_lo = p[..., o:o + 2 * A_LORA].reshape(b, l, 2, A_LORA)
    o = o + 2 * A_LORA
    g = jax.nn.sigmoid(p[..., o:o + G_LORA]) @ g_up
    kk = heads(k * k_k).astype(jnp.float32)
    kk = kk * lax.rsqrt(jnp.sum(kk * kk, axis=-1, keepdims=True) + 1e-12)
    decays, keys, b_vecs = [], [], []
    for d in range(2):
        w = -jax.nn.softplus(-(w0[d] + jnp.tanh(w_lo[:, :, d]) @ w_up[d]).astype(jnp.float32)) - 0.5
        decays.append(heads(jnp.exp(-jnp.exp(w))))
        a = jax.nn.sigmoid(a0[d] + a_lo[:, :, d] @ a_up[d])
        keys.append(heads(k * (1.0 + (a - 1.0) * k_a)))
        b_vecs.append(kk * heads(a))
    return heads(r), heads(v), kk, g, decays, keys, b_vecs


def rwkv7_scan(r, decay, k, v, a_vec, b_vec, s0, reverse):
    xs = tuple(jnp.moveaxis(t, 1, 0) for t in (r, decay, k, v, a_vec, b_vec))

    def step(s, inp):
        r_t, w_t, k_t, v_t, a_t, b_t = inp
        sa = jnp.einsum('bhvk,bhk->bhv', s, a_t)
        s = s * w_t[:, :, None, :] + sa[..., None] * b_t[:, :, None, :] + v_t[..., None] * k_t[:, :, None, :]
        return s, jnp.einsum('bhvk,bhk->bhv', s, r_t)

    s_fin, y = lax.scan(step, s0, xs, reverse=reverse)
    return jnp.moveaxis(y, 0, 1), s_fin


def rwkv7_output(r, v, g, keys, y, r_k, ln_w, ln_b):
    b, l = r.shape[:2]
    yf = y.astype(jnp.float32)
    mean = jnp.mean(yf, axis=-1, keepdims=True)
    var = jnp.mean(jnp.square(yf - mean), axis=-1, keepdims=True)
    yn = ((yf - mean) * lax.rsqrt(var + RWKV_GN_EPS)).reshape(b, l, D_RWKV) * ln_w + ln_b
    bonus = jnp.sum(r * (keys[0] + keys[1]) * r_k, axis=-1, keepdims=True) * v
    return (yn + bonus.reshape(b, l, D_RWKV)) * g


def rwkv7_mixer(p_ctx, p_lat, mu, w0, w_up, a0, a_up, g_up, k_k, k_a, r_k, ln_w, ln_b):
    rc, vc, kkc, gc, dec_c, key_c, bv_c = rwkv7_features(p_ctx, mu, w0, w_up, a0, a_up, g_up, k_k, k_a)
    rl, vl, kkl, gl, dec_l, key_l, bv_l = rwkv7_features(p_lat, mu, w0, w_up, a0, a_up, g_up, k_k, k_a)
    s0 = jnp.zeros((p_ctx.shape[0], RWKV_HEADS, RWKV_HEAD, RWKV_HEAD), jnp.float32)
    y_ctx, y_lat = 0.0, 0.0
    for d in range(2):
        yc, s_c = rwkv7_scan(rc, dec_c[d], key_c[d], vc, -kkc, bv_c[d], s0, d == 1)
        yl, _ = rwkv7_scan(rl, dec_l[d], key_l[d], vl, -kkl, bv_l[d], s_c, d == 1)
        y_ctx = y_ctx + yc
        y_lat = y_lat + yl
    return (rwkv7_output(rc, vc, gc, key_c, y_ctx, r_k, ln_w, ln_b),
            rwkv7_output(rl, vl, gl, key_l, y_lat, r_k, ln_w, ln_b))


def ssd_chunked(x, dt, a, bm, cm, h0):
    b, l, nh, pd = x.shape
    nc, q, g, hg, n = l // SSM_CHUNK, SSM_CHUNK, SSM_GROUPS, SSM_HEADS // SSM_GROUPS, SSM_STATE
    xq = x.reshape(b, nc, q, g, hg, pd)
    dtq = dt.reshape(b, nc, q, g, hg)
    bq = bm.reshape(b, nc, q, g, n)
    cq = cm.reshape(b, nc, q, g, n)
    cum = jnp.cumsum(dtq * a.reshape(g, hg), axis=2)
    seg = cum[:, :, :, None] - cum[:, :, None, :]
    lower = jnp.tril(jnp.ones((q, q), bool))[None, None, :, :, None, None]
    decay_ij = jnp.exp(jnp.where(lower, seg, -jnp.inf))
    cb = jnp.einsum('bcign,bcjgn->bcijg', cq, bq)
    scores = cb[..., None] * decay_ij * dtq[:, :, None]
    y_diag = jnp.einsum('bcijgh,bcjghp->bcighp', scores, xq)
    to_end = jnp.exp(cum[:, :, -1:] - cum) * dtq
    states = jnp.einsum('bcjgn,bcjgh,bcjghp->bcghpn', bq, to_end, xq)
    chunk_decay = jnp.exp(cum[:, :, -1])

    def step(h, inp):
        st, dc = inp
        return h * dc[..., None, None] + st, h

    h_fin, h_enter = lax.scan(step, h0.reshape(b, g, hg, pd, n),
                              (jnp.moveaxis(states, 1, 0), jnp.moveaxis(chunk_decay, 1, 0)))
    h_enter = jnp.moveaxis(h_enter, 0, 1)
    y_off = jnp.einsum('bcign,bcghpn->bcighp', cq, h_enter) * jnp.exp(cum)[..., None]
    return (y_diag + y_off).reshape(b, l, nh, pd), h_fin.reshape(b, nh, pd, n)


def ssd_direction(x, dt, a, bm, cm, h0, reverse):
    if reverse:
        x, dt, bm, cm = (jnp.flip(t, axis=1) for t in (x, dt, bm, cm))
    y, h_fin = ssd_chunked(x, dt, a, bm, cm, h0)
    if reverse:
        y = jnp.flip(y, axis=1)
    return y, h_fin


def mamba2_prepare(p, conv_w, conv_b):
    b, l, _ = p.shape
    z = p[..., :D_SSM]
    xbc = jax.nn.silu(centred_dwconv(p[..., D_SSM:D_SSM + D_XBC], conv_w, conv_b))
    xs = xbc[..., :D_SSM].reshape(b, l, SSM_HEADS, SSM_HEADDIM)
    bm = xbc[..., D_SSM:D_SSM + SSM_GROUPS * SSM_STATE].reshape(b, l, SSM_GROUPS, SSM_STATE)
    cm = xbc[..., D_SSM + SSM_GROUPS * SSM_STATE:].reshape(b, l, SSM_GROUPS, SSM_STATE)
    dt_raw = p[..., D_SSM + D_XBC:].reshape(b, l, 2, SSM_HEADS).astype(jnp.float32)
    return z, xs, bm, cm, dt_raw


def gated_rms_norm(y, z, gain):
    b, l, _ = z.shape
    yg = (y.reshape(b, l, D_SSM) * jax.nn.silu(z.astype(jnp.float32))).reshape(b, l, SSM_GROUPS, D_SSM // SSM_GROUPS)
    yg = yg * lax.rsqrt(jnp.mean(yg * yg, axis=-1, keepdims=True) + EPS)
    return yg.reshape(b, l, D_SSM) * gain


def mamba2_mixer(p_ctx, p_lat, conv_w, conv_b, a_log, dt_bias, d_skip, norm_w):
    zc, xc, bc, cc, dtc = mamba2_prepare(p_ctx, conv_w, conv_b)
    zl, xl, bl, cl, dtl = mamba2_prepare(p_lat, conv_w, conv_b)
    a = -jnp.exp(a_log.astype(jnp.float32))
    h0 = jnp.zeros((p_ctx.shape[0], SSM_HEADS, SSM_HEADDIM, SSM_STATE), jnp.float32)
    y_ctx = d_skip[:, None] * xc
    y_lat = d_skip[:, None] * xl
    for d in range(2):
        dt_c = jax.nn.softplus(dtc[:, :, d] + dt_bias[d])
        dt_l = jax.nn.softplus(dtl[:, :, d] + dt_bias[d])
        yc, h_c = ssd_direction(xc, dt_c, a[d], bc, cc, h0, d == 1)
        yl, _ = ssd_direction(xl, dt_l, a[d], bl, cl, h_c, d == 1)
        y_ctx = y_ctx + yc
        y_lat = y_lat + yl
    return gated_rms_norm(y_ctx, zc, norm_w), gated_rms_norm(y_lat, zl, norm_w)


def merge_branches(p_gate, y_lru, y_rwkv, y_ssm, w_out_lru, w_out_rwkv, w_out_ssm, w_o):
    g = jax.nn.sigmoid(p_gate)
    m = (g[..., :D_MODEL] * (y_lru @ w_out_lru)
         + g[..., D_MODEL:2 * D_MODEL] * (y_rwkv @ w_out_rwkv)
         + g[..., 2 * D_MODEL:] * (y_ssm @ w_out_ssm))
    return m @ w_o


def swiglu(u, w1, w3, w2):
    return (jax.nn.silu(u @ w1) * (u @ w3)) @ w2


def moe_swiglu(u, router, w1, w3, w2):
    shp = u.shape
    t = u.reshape(-1, shp[-1])
    logits = (t @ router).astype(jnp.float32)
    top_v, top_i = lax.top_k(logits, TOP_K)
    probs = jax.nn.softmax(top_v, axis=-1)
    gates = jnp.sum(jax.nn.one_hot(top_i, N_EXPERTS, dtype=probs.dtype) * probs[..., None], axis=-2)
    y = 0.0
    for e in range(N_EXPERTS):
        y = y + gates[:, e:e + 1] * swiglu(t, w1[e], w3[e], w2[e])
    return y.reshape(shp)


def channel_mixer(u, li, ffn_w1, ffn_w3, ffn_w2, moe_router, moe_w1, moe_w3, moe_w2):
    j = li // 2
    if li % 2 == 0:
        return swiglu(u, ffn_w1[j], ffn_w3[j], ffn_w2[j])
    return moe_swiglu(u, moe_router[j], moe_w1[j], moe_w3[j], moe_w2[j])


def setup_inputs(seed: int = 0) -> dict:
    key = jax.random.key(seed)
    ks = iter(jax.random.split(key, 64))
    f32 = jnp.float32

    def nrm(shape, scale):
        return jax.random.normal(next(ks), shape, f32) * scale

    def unif(shape, lo, hi):
        return jax.random.uniform(next(ks), shape, f32, lo, hi)

    lam_a = unif((DEPTH, 2, D_LRU), 0.9, 0.999)
    dt0 = jnp.exp(unif((DEPTH, 2, SSM_HEADS), math.log(1e-3), math.log(1e-1)))
    return {
        "x": nrm((BATCH, SEQ, D_MODEL), 1.0),
        "c": nrm((BATCH, D_MODEL), 1.0),
        "ctx": nrm((BATCH, CTX_LEN, D_MODEL), 1.0),
        "c_ctx": nrm((D_MODEL,), 1.0),
        "ada_w": nrm((DEPTH, D_MODEL, 6 * D_MODEL), 0.5 * D_MODEL ** -0.5),
        "ada_b": nrm((DEPTH, 6 * D_MODEL), 0.02),
        "norm_mix": 1.0 + nrm((DEPTH, D_MODEL), 0.02),
        "norm_ffn": 1.0 + nrm((DEPTH, D_MODEL), 0.02),
        "norm_final": 1.0 + nrm((D_MODEL,), 0.02),
        "w_in": nrm((DEPTH, D_MODEL, N_IN), D_MODEL ** -0.5),
        "lru_conv_w": nrm((DEPTH, CONV_W, D_LRU), 0.5),
        "lru_conv_b": nrm((DEPTH, D_LRU), 0.02),
        "lru_gate_w": nrm((DEPTH, 2, 2, LRU_BLOCKS, LRU_BS, LRU_BS), LRU_BS ** -0.5),
        "lru_gate_b": nrm((DEPTH, 2, 2, D_LRU), 0.1),
        "lru_lambda": jnp.log(lam_a) - jnp.log1p(-lam_a),
        "rwkv_mu": unif((DEPTH, N_RWKV_IN), 0.0, 1.0),
        "rwkv_w0": unif((DEPTH, 2, D_RWKV), -6.0, -1.0),
        "rwkv_w_up": nrm((DEPTH, 2, W_LORA, D_RWKV), 0.5 * W_LORA ** -0.5),
        "rwkv_a0": nrm((DEPTH, 2, D_RWKV), 0.3),
        "rwkv_a_up": nrm((DEPTH, 2, A_LORA, D_RWKV), 0.5 * A_LORA ** -0.5),
        "rwkv_g_up": nrm((DEPTH, G_LORA, D_RWKV), G_LORA ** -0.5),
        "rwkv_k_k": 0.85 + nrm((DEPTH, D_RWKV), 0.05),
        "rwkv_k_a": 1.0 + nrm((DEPTH, D_RWKV), 0.05),
        "rwkv_r_k": nrm((DEPTH, RWKV_HEADS, RWKV_HEAD), 0.1),
        "rwkv_ln_w": 1.0 + nrm((DEPTH, D_RWKV), 0.02),
        "rwkv_ln_b": nrm((DEPTH, D_RWKV), 0.02),
        "ssm_conv_w": nrm((DEPTH, CONV_W, D_XBC), 0.5),
        "ssm_conv_b": nrm((DEPTH, D_XBC), 0.02),
        "ssm_a_log": jnp.log(unif((DEPTH, 2, SSM_HEADS), 1.0, 16.0)),
        "ssm_dt_bias": dt0 + jnp.log(-jnp.expm1(-dt0)),
        "ssm_d": 1.0 + nrm((DEPTH, SSM_HEADS), 0.1),
        "ssm_norm_w": 1.0 + nrm((DEPTH, D_SSM), 0.02),
        "w_out_lru": nrm((DEPTH, D_LRU, D_MODEL), D_LRU ** -0.5),
        "w_out_rwkv": nrm((DEPTH, D_RWKV, D_MODEL), D_RWKV ** -0.5),
        "w_out_ssm": nrm((DEPTH, D_SSM, D_MODEL), D_SSM ** -0.5),
        "w_o": nrm((DEPTH, D_MODEL, D_MODEL), D_MODEL ** -0.5),
        "ffn_w1": nrm((N_DENSE, D_MODEL, D_FF), D_MODEL ** -0.5),
        "ffn_w3": nrm((N_DENSE, D_MODEL, D_FF), D_MODEL ** -0.5),
        "ffn_w2": nrm((N_DENSE, D_FF, D_MODEL), D_FF ** -0.5),
        "moe_router": nrm((N_MOE, D_MODEL, N_EXPERTS), D_MODEL ** -0.5),
        "moe_w1": nrm((N_MOE, N_EXPERTS, D_MODEL, D_FF_EXPERT), D_MODEL ** -0.5),
        "moe_w3": nrm((N_MOE, N_EXPERTS, D_MODEL, D_FF_EXPERT), D_MODEL ** -0.5),
        "moe_w2": nrm((N_MOE, N_EXPERTS, D_FF_EXPERT, D_MODEL), D_FF_EXPERT ** -0.5),
    }


def reference(x, c, ctx, c_ctx, ada_w, ada_b, norm_mix, norm_ffn, norm_final, w_in,
              lru_conv_w, lru_conv_b, lru_gate_w, lru_gate_b, lru_lambda,
              rwkv_mu, rwkv_w0, rwkv_w_up, rwkv_a0, rwkv_a_up, rwkv_g_up, rwkv_k_k, rwkv_k_a,
              rwkv_r_k, rwkv_ln_w, rwkv_ln_b,
              ssm_conv_w, ssm_conv_b, ssm_a_log, ssm_dt_bias, ssm_d, ssm_norm_w,
              w_out_lru, w_out_rwkv, w_out_ssm, w_o,
              ffn_w1, ffn_w3, ffn_w2, moe_router, moe_w1, moe_w3, moe_w2):
    rows = x.shape[1] // GRID_W
    h_lat, h_ctx = x, ctx
    for li in range(DEPTH):
        last = li == DEPTH - 1
        mod_lat = (jax.nn.silu(c) @ ada_w[li] + ada_b[li])[:, None, :]
        mod_ctx = (jax.nn.silu(c_ctx) @ ada_w[li] + ada_b[li])[None, None, :]
        shm_l, scm_l, gm_l, shf_l, scf_l, gf_l = jnp.split(mod_lat, 6, axis=-1)
        shm_c, scm_c, gm_c, shf_c, scf_c, gf_c = jnp.split(mod_ctx, 6, axis=-1)

        u_lat = rms_norm(h_lat, norm_mix[li]) * (1.0 + scm_l) + shm_l
        u_ctx = rms_norm(h_ctx, norm_mix[li]) * (1.0 + scm_c) + shm_c
        if li % 2 == 1:
            u_lat = grid_transpose(u_lat, rows, GRID_W)
        p_lat = u_lat @ w_in[li]
        p_ctx = u_ctx @ w_in[li]
        ya_c, ya_l = rglru_mixer(p_ctx[..., OFF_LRU:OFF_RWKV], p_lat[..., OFF_LRU:OFF_RWKV],
                                 lru_conv_w[li], lru_conv_b[li], lru_gate_w[li], lru_gate_b[li],
                                 lru_lambda[li])
        yb_c, yb_l = rwkv7_mixer(p_ctx[..., OFF_RWKV:OFF_SSM], p_lat[..., OFF_RWKV:OFF_SSM],
                                 rwkv_mu[li], rwkv_w0[li], rwkv_w_up[li], rwkv_a0[li], rwkv_a_up[li],
                                 rwkv_g_up[li], rwkv_k_k[li], rwkv_k_a[li], rwkv_r_k[li],
                                 rwkv_ln_w[li], rwkv_ln_b[li])
        yc_c, yc_l = mamba2_mixer(p_ctx[..., OFF_SSM:], p_lat[..., OFF_SSM:],
                                  ssm_conv_w[li], ssm_conv_b[li], ssm_a_log[li], ssm_dt_bias[li],
                                  ssm_d[li], ssm_norm_w[li])
        o_lat = merge_branches(p_lat[..., :OFF_LRU], ya_l, yb_l, yc_l,
                               w_out_lru[li], w_out_rwkv[li], w_out_ssm[li], w_o[li])
        if li % 2 == 1:
            o_lat = grid_transpose(o_lat, GRID_W, rows)
        h_lat = h_lat + gm_l * o_lat
        if not last:
            o_ctx = merge_branches(p_ctx[..., :OFF_LRU], ya_c, yb_c, yc_c,
                                   w_out_lru[li], w_out_rwkv[li], w_out_ssm[li], w_o[li])
            h_ctx = h_ctx + gm_c * o_ctx

        v_lat = rms_norm(h_lat, norm_ffn[li]) * (1.0 + scf_l) + shf_l
        h_lat = h_lat + gf_l * channel_mixer(v_lat, li, ffn_w1, ffn_w3, ffn_w2,
                                             moe_router, moe_w1, moe_w3, moe_w2)
        if not last:
            v_ctx = rms_norm(h_ctx, norm_ffn[li]) * (1.0 + scf_c) + shf_c
            h_ctx = h_ctx + gf_c * channel_mixer(v_ctx, li, ffn_w1, ffn_w3, ffn_w2,
                                                 moe_router, moe_w1, moe_w3, moe_w2)
    return rms_norm(h_lat, norm_final)
```

```python
import functools
import math

import jax
import jax.numpy as jnp
from jax import lax
from jax.experimental import pallas as pl
from jax.experimental.pallas import tpu as pltpu

f32 = jnp.float32
bf16 = jnp.bfloat16

D = 2048
GRID_W = 64
EPS = 1e-6
CONV_W = 4
CONV_LEFT = 2
HALO = 8

LRU_BLOCKS = 16
LRU_BS = D // LRU_BLOCKS
LRU_C = 8.0

HEADS = 32
HEAD = 64
W_LORA = 96
A_LORA = 96
G_LORA = 256
GN_EPS = 64e-5
N_RWKV = 3 * D + 2 * W_LORA + 2 * A_LORA + G_LORA
N_RWKV_PAD = 6912
RW_OFF_W = 3 * D
RW_OFF_A = 3 * D + 128
RW_OFF_G = 3 * D + 2 * W_LORA + 2 * A_LORA
RW_CHUNK = 64
RW_GW = 256
RW_NG = D // RW_GW

SSM_HEADS = 32
SSM_P = 64
SSM_N = 128
SSM_G = 4
SSM_Q = 128
D_XBC = D + 2 * SSM_G * SSM_N
N_SSM = D + D_XBC + 2 * SSM_HEADS
N_SSM_PAD = 5376
SS_OFF_DT = D + D_XBC

D_FF = 3 * D
N_EXPERTS = 8
D_FF_EXPERT = D_FF // 2

VMEM_LIMIT = 56 * 1024 * 1024


def _cparams(n_axes, vmem=VMEM_LIMIT):
    return pltpu.CompilerParams(dimension_semantics=("arbitrary",) * n_axes, vmem_limit_bytes=vmem)


def _mm(a, b):
    return jnp.dot(a.astype(bf16), b.astype(bf16), preferred_element_type=f32)


def _mm_nt(a, b):
    return lax.dot_general(a.astype(bf16), b.astype(bf16), (((1,), (1,)), ((), ())),
                           preferred_element_type=f32)


def _split3(x):
    x0 = x.astype(bf16)
    r = x - x0.astype(f32)
    x1 = r.astype(bf16)
    r = r - x1.astype(f32)
    return x0, x1, r.astype(bf16)


def _mm_x3(x, e):
    x0, x1, x2 = _split3(x)
    return (jnp.dot(x0, e, preferred_element_type=f32) + jnp.dot(x1, e, preferred_element_type=f32)
            + jnp.dot(x2, e, preferred_element_type=f32))


def _mm_e3(e, x):
    x0, x1, x2 = _split3(x)
    return (jnp.dot(e, x0, preferred_element_type=f32) + jnp.dot(e, x1, preferred_element_type=f32)
            + jnp.dot(e, x2, preferred_element_type=f32))


def _softplus(x):
    return jnp.maximum(x, 0.0) + jnp.log1p(jnp.exp(-jnp.abs(x)))


def _sigmoid(x):
    return jax.nn.sigmoid(x)


def _silu(x):
    return x * jax.nn.sigmoid(x)


def _iota(shape, dim):
    return lax.broadcasted_iota(jnp.int32, shape, dim)


def _ada_kernel(c_ref, w_ref, b_ref, o_ref):
    cv = c_ref[...]
    o_ref[...] = _mm(_silu(cv), w_ref[...]) + b_ref[...]


def _ada(cvec, w, b):
    n = w.shape[1]
    tn = 1536
    return pl.pallas_call(
        _ada_kernel,
        grid=(n // tn,),
        in_specs=[pl.BlockSpec((8, D), lambda j: (0, 0)),
                  pl.BlockSpec((D, tn), lambda j: (0, j)),
                  pl.BlockSpec((1, tn), lambda j: (0, j))],
        out_specs=pl.BlockSpec((8, tn), lambda j: (0, j)),
        out_shape=jax.ShapeDtypeStruct((8, n), f32),
        compiler_params=_cparams(1),
        name="ada",
    )(cvec, w, b.reshape(1, n))


def _norm_kernel(x_ref, g_ref, sc_ref, sh_ref, o_ref):
    x = x_ref[...]
    ms = jnp.mean(x * x, axis=-1, keepdims=True)
    xn = x * lax.rsqrt(ms + EPS) * g_ref[...]
    o_ref[...] = (xn * (1.0 + sc_ref[...]) + sh_ref[...]).astype(o_ref.dtype)


def _norm(h, gain, mod4, row_of_batch, k_sc, k_sh, *, transposed=False, out_dtype=bf16):
    bsz, l, _ = h.shape
    if transposed:
        rows = l // GRID_W
        hin = h.reshape(bsz, rows, GRID_W * D)
        tl = rows
        nt = GRID_W
        in_spec = pl.BlockSpec((None, tl, D), lambda b, i: (b, 0, i))
    else:
        tl = min(512, l)
        nt = l // tl
        hin = h
        in_spec = pl.BlockSpec((None, tl, D), lambda b, i: (b, i, 0))
    return pl.pallas_call(
        _norm_kernel,
        grid=(bsz, nt),
        in_specs=[in_spec,
                  pl.BlockSpec((1, D), lambda b, i: (0, 0)),
                  pl.BlockSpec((None, None, 1, D), lambda b, i: (row_of_batch(b), k_sc, 0, 0)),
                  pl.BlockSpec((None, None, 1, D), lambda b, i: (row_of_batch(b), k_sh, 0, 0))],
        out_specs=pl.BlockSpec((None, tl, D), lambda b, i: (b, i, 0)),
        out_shape=jax.ShapeDtypeStruct((bsz, l, D), out_dtype),
        compiler_params=_cparams(2),
        name="norm",
    )(hin, gain.reshape(1, D), mod4, mod4)


def _plain_norm_kernel(x_ref, g_ref, o_ref):
    x = x_ref[...]
    ms = jnp.mean(x * x, axis=-1, keepdims=True)
    o_ref[...] = (x * lax.rsqrt(ms + EPS) * g_ref[...]).astype(o_ref.dtype)


def _final_norm(h, gain):
    bsz, l, _ = h.shape
    tl = min(512, l)
    return pl.pallas_call(
        _plain_norm_kernel,
        grid=(bsz, l // tl),
        in_specs=[pl.BlockSpec((None, tl, D), lambda b, i: (b, i, 0)),
                  pl.BlockSpec((1, D), lambda b, i: (0, 0))],
        out_specs=pl.BlockSpec((None, tl, D), lambda b, i: (b, i, 0)),
        out_shape=jax.ShapeDtypeStruct((bsz, l, D), f32),
        compiler_params=_cparams(2),
        name="final_norm",
    )(h, gain.reshape(1, D))


def _proj_kernel(u_ref, w_ref, o_ref, *, act):
    acc = jnp.dot(u_ref[...], w_ref[...], preferred_element_type=f32)
    if act == "sigmoid":
        acc = _sigmoid(acc)
    o_ref[...] = acc.astype(o_ref.dtype)


def _proj(u, w, *, tn, act=None, out_dtype=f32):
    m, k = u.shape
    n = w.shape[1]
    tm = min(1024, m)
    return pl.pallas_call(
        functools.partial(_proj_kernel, act=act),
        grid=(m // tm, n // tn),
        in_specs=[pl.BlockSpec((tm, k), lambda i, j: (i, 0)),
                  pl.BlockSpec((k, tn), lambda i, j: (0, j))],
        out_specs=pl.BlockSpec((tm, tn), lambda i, j: (i, j)),
        out_shape=jax.ShapeDtypeStruct((m, n), out_dtype),
        compiler_params=_cparams(2),
        name="proj",
    )(u, w)


def _lru_kernel(*refs, reverse, final, tl, nt):
    if final:
        (x_ref, prev_ref, next_ref, gate_ref, hb_ref, cw_ref, cb_ref, gw_ref, gb_ref, lam_ref, h0_ref,
         out_ref, hlast_ref, xe_ref, a_ref, bx_ref, hs_ref, h_ref) = refs
    else:
        (x_ref, prev_ref, next_ref, cw_ref, cb_ref, gw_ref, gb_ref, lam_ref, h0_ref,
         out_ref, hlast_ref, xe_ref, a_ref, bx_ref, hs_ref, h_ref) = refs
    i = pl.program_id(1)
    t = (nt - 1 - i) if reverse else i

    @pl.when(i == 0)
    def _():
        h_ref[...] = h0_ref[...]

    zero = jnp.zeros((HALO, D), f32)
    xe_ref[0:HALO, :] = jnp.where(t > 0, prev_ref[...], zero)
    xe_ref[HALO:HALO + tl, :] = x_ref[...]
    xe_ref[HALO + tl:2 * HALO + tl, :] = jnp.where(t < nt - 1, next_ref[...], zero)

    for n in range(LRU_BLOCKS):
        cs = slice(n * LRU_BS, (n + 1) * LRU_BS)
        xc = cb_ref[:, cs]
        for tap in range(CONV_W):
            r0 = HALO - CONV_LEFT + tap
            xc = xc + xe_ref[r0:r0 + tl, cs] * cw_ref[tap:tap + 1, cs]
        g = _mm(xc, gw_ref[n])
        rec = _sigmoid(g[:, :LRU_BS] + gb_ref[0:1, cs])
        inp = _sigmoid(g[:, LRU_BS:] + gb_ref[1:2, cs])
        log_a = -LRU_C * rec * _softplus(-lam_ref[:, cs])
        a_ref[:, cs] = jnp.exp(log_a)
        th = jnp.tanh(log_a)
        bx_ref[:, cs] = jnp.sqrt(-2.0 * th / (1.0 - th)) * inp * xc

    def body(s, h):
        tt = (tl - 1 - s) if reverse else s
        h = a_ref[pl.ds(tt, 1), :] * h + bx_ref[pl.ds(tt, 1), :]
        hs_ref[pl.ds(tt, 1), :] = h
        return h

    h = lax.fori_loop(0, tl, body, h_ref[...], unroll=8)
    h_ref[...] = h

    @pl.when(i == nt - 1)
    def _():
        hlast_ref[...] = h

    if final:
        for n in range(LRU_BLOCKS):
            cs = slice(n * LRU_BS, (n + 1) * LRU_BS)
            y = (hs_ref[:, cs] + hb_ref[:, cs]) * jax.nn.gelu(gate_ref[:, cs])
            out_ref[:, cs] = y.astype(out_ref.dtype)
    else:
        out_ref[...] = hs_ref[...]


def _lru_pass(p, cw, cb, gw, gb, lam, h0, hb, *, reverse):
    bsz, l, _ = p.shape
    tl = min(256, l)
    nt = l // tl
    final = hb is not None
    nh = l // HALO
    tpb = tl // HALO

    def tmap(i):
        return (nt - 1 - i) if reverse else i

    main = pl.BlockSpec((None, tl, D), lambda b, i: (b, tmap(i), 0))
    in_specs = [main,
                pl.BlockSpec((None, HALO, D), lambda b, i: (b, jnp.maximum(tmap(i) * tpb - 1, 0), 0)),
                pl.BlockSpec((None, HALO, D), lambda b, i: (b, jnp.minimum((tmap(i) + 1) * tpb, nh - 1), 0))]
    args = [p, p, p]
    if final:
        in_specs += [pl.BlockSpec((None, tl, D), lambda b, i: (b, tmap(i), 1)), main]
        args += [p, hb]
    in_specs += [pl.BlockSpec((CONV_W, D), lambda b, i: (0, 0)),
                 pl.BlockSpec((1, D), lambda b, i: (0, 0)),
                 pl.BlockSpec((LRU_BLOCKS, LRU_BS, 2 * LRU_BS), lambda b, i: (0, 0, 0)),
                 pl.BlockSpec((2, D), lambda b, i: (0, 0)),
                 pl.BlockSpec((1, D), lambda b, i: (0, 0)),
                 pl.BlockSpec((None, 1, D), lambda b, i: (b, 0, 0))]
    args += [cw, cb, gw, gb, lam, h0]
    out, hlast = pl.pallas_call(
        functools.partial(_lru_kernel, reverse=reverse, final=final, tl=tl, nt=nt),
        grid=(bsz, nt),
        in_specs=in_specs,
        out_specs=[main, pl.BlockSpec((None, 1, D), lambda b, i: (b, 0, 0))],
        out_shape=[jax.ShapeDtypeStruct((bsz, l, D), bf16 if final else f32),
                   jax.ShapeDtypeStruct((bsz, 1, D), f32)],
        scratch_shapes=[pltpu.VMEM((tl + 2 * HALO, D), f32), pltpu.VMEM((tl, D), f32),
                        pltpu.VMEM((tl, D), f32), pltpu.VMEM((tl, D), f32), pltpu.VMEM((1, D), f32)],
        compiler_params=_cparams(2),
        name="lru_fwd" if final else "lru_bwd",
    )(*args)
    return out, hlast


def _rwkv_feat_kernel(p_ref, prev_ref, next_ref, mu_ref, w0_ref, wup_ref, a0_ref, aup_ref, gup_ref,
                      kk_ref, ka_ref, rk_ref, ones_ref,
                      r_out, v_out, kkv_out, g_out, bonus_out, lw_out, key_out, b_out,
                      pe_ref, *, tl, nt):
    i = pl.program_id(1)
    zero = jnp.zeros((HALO, N_RWKV_PAD), f32)
    pe_ref[0:HALO, :] = jnp.where(i > 0, prev_ref[...], zero)
    pe_ref[HALO:HALO + tl, :] = p_ref[...]
    pe_ref[HALO + tl:2 * HALO + tl, :] = jnp.where(i < nt - 1, next_ref[...], zero)

    def shifted(c0, width):
        cs = slice(c0, c0 + width)
        p = pe_ref[HALO:HALO + tl, cs]
        nb = 0.5 * (pe_ref[HALO - 1:HALO - 1 + tl, cs] + pe_ref[HALO + 1:HALO + 1 + tl, cs])
        return p + mu_ref[:, cs] * (nb - p)

    win_w = jnp.tanh(shifted(RW_OFF_W, 256)).astype(bf16)
    win_a = shifted(RW_OFF_A, 256).astype(bf16)
    win_g = _sigmoid(shifted(RW_OFF_G, 256)).astype(bf16)
    ones_bd = ones_ref[...]

    for g in range(RW_NG):
        cs = slice(g * RW_GW, (g + 1) * RW_GW)
        r = shifted(g * RW_GW, RW_GW)
        k = shifted(D + g * RW_GW, RW_GW)
        v = shifted(2 * D + g * RW_GW, RW_GW)
        kf = k * kk_ref[:, cs]
        kk = kf * lax.rsqrt(_mm_x3(kf * kf, ones_bd) + 1e-12)
        ksum = None
        for d in range(2):
            wv = -_softplus(-(w0_ref[d:d + 1, cs] + jnp.dot(win_w, wup_ref[d, :, cs],
                                                              preferred_element_type=f32))) - 0.5
            lw_out[d, :, cs] = -jnp.exp(wv)
            a = _sigmoid(a0_ref[d:d + 1, cs] + jnp.dot(win_a, aup_ref[d, :, cs],
                                                       preferred_element_type=f32))
            key = k * (1.0 + (a - 1.0) * ka_ref[:, cs])
            key_out[d, :, cs] = key
            b_out[d, :, cs] = kk * a
            ksum = key if ksum is None else ksum + key
        r_out[:, cs] = r
        v_out[:, cs] = v
        kkv_out[:, cs] = kk
        bonus_out[:, cs] = _mm_x3(r * ksum * rk_ref[:, cs], ones_bd) * v
        g_out[:, cs] = jnp.dot(win_g, gup_ref[:, cs], preferred_element_type=f32)


def _rwkv_feat(p, prm):
    bsz, l, _ = p.shape
    tl = min(128, l)
    nt = l // tl
    nh = l // HALO
    tpb = tl // HALO
    full2 = lambda shape: pl.BlockSpec(shape, lambda b, i: (0,) * len(shape))
    tok = pl.BlockSpec((None, tl, D), lambda b, i: (b, i, 0))
    tok2 = pl.BlockSpec((2, None, tl, D), lambda b, i: (0, b, i, 0))
    sd = jax.ShapeDtypeStruct((bsz, l, D), f32)
    sd2 = jax.ShapeDtypeStruct((2, bsz, l, D), f32)
    return pl.pallas_call(
        functools.partial(_rwkv_feat_kernel, tl=tl, nt=nt),
        grid=(bsz, nt),
        in_specs=[pl.BlockSpec((None, tl, N_RWKV_PAD), lambda b, i: (b, i, 0)),
                  pl.BlockSpec((None, HALO, N_RWKV_PAD), lambda b, i: (b, jnp.maximum(i * tpb - 1, 0), 0)),
                  pl.BlockSpec((None, HALO, N_RWKV_PAD),
                               lambda b, i: (b, jnp.minimum((i + 1) * tpb, nh - 1), 0)),
                  full2((1, N_RWKV_PAD)), full2((2, D)), full2((2, 256, D)), full2((2, D)),
                  full2((2, 256, D)), full2((256, D)), full2((1, D)), full2((1, D)), full2((1, D)),
                  full2((RW_GW, RW_GW))],
        out_specs=[tok, tok, tok, tok, tok, tok2, tok2, tok2],
        out_shape=[sd, sd, sd, sd, sd, sd2, sd2, sd2],
        scratch_shapes=[pltpu.VMEM((tl + 2 * HALO, N_RWKV_PAD), f32)],
        compiler_params=_cparams(2),
        name="rwkv_feat",
    )(p, p, p, prm["mu"], prm["w0"], prm["wup"], prm["a0"], prm["aup"], prm["gup"],
      prm["k_k"], prm["k_a"], prm["r_k"], prm["ones_bd"])


def _bd_expand(y, lane_head):
    yb = y.astype(f32)
    return jnp.concatenate([jnp.where(lane_head == h, yb, 0.0).astype(bf16) for h in range(4)], axis=0)


def _rwkv_scan_kernel(r_ref, v_ref, kk_ref, lw_ref, key_ref, b_ref, s0_ref, y_ref, sfin_ref,
                      s_ref, *, nc):
    c = RW_CHUNK
    d = pl.program_id(0)
    i = pl.program_id(2)
    fwd = d == 0

    @pl.when(i == 0)
    def _():
        s_ref[...] = s0_ref[...]

    row = _iota((c, c), 0)
    col = _iota((c, c), 1)
    sgn = jnp.where(fwd, 1, -1)
    tri = jnp.where((col - row) * sgn <= 0, 1.0, 0.0).astype(bf16)
    t4 = _iota((c, 4 * c), 0)
    j4 = _iota((c, 4 * c), 1) % c
    mask_s = (j4 - t4) * sgn < 0
    mask_i = (j4 - t4) * sgn <= 0
    eye4 = jnp.where(j4 == t4, 1.0, 0.0)
    lane_head = _iota((c, RW_GW), 1) // HEAD
    bd_mask = (_iota((RW_GW, RW_GW), 0) // HEAD) == (_iota((RW_GW, RW_GW), 1) // HEAD)

    def mmbd(x, y):
        return jnp.dot(x.astype(bf16), _bd_expand(y, lane_head), preferred_element_type=f32)

    for g in range(RW_NG):
        cs = slice(g * RW_GW, (g + 1) * RW_GW)
        lw = lw_ref[:, cs]
        cl = _mm_e3(tri, lw)
        tot = jnp.sum(lw, axis=0, keepdims=True)
        g_inv = jnp.exp(-cl)
        g_end = jnp.exp(tot - cl)
        kkv = kk_ref[:, cs]
        bv = b_ref[:, cs]
        key = key_ref[:, cs]
        v = v_ref[:, cs]
        at = -kkv * jnp.exp(cl - lw)
        rt = r_ref[:, cs] * jnp.exp(cl)
        bt = bv * g_inv
        kt = key * g_inv
        ar = jnp.concatenate([at, rt], axis=0).astype(bf16)
        a_b = lax.dot_general(ar, _bd_expand(bt, lane_head), (((1,), (1,)), ((), ())),
                              preferred_element_type=f32)
        a_k = lax.dot_general(ar, _bd_expand(kt, lane_head), (((1,), (1,)), ((), ())),
                              preferred_element_type=f32)
        n_ab = jnp.where(mask_s, a_b[:c], 0.0)
        a_ak = jnp.where(mask_s, a_k[:c], 0.0)
        a_rb = jnp.where(mask_i, a_b[c:], 0.0)
        a_rk = jnp.where(mask_i, a_k[c:], 0.0)
        x = eye4 + n_ab
        m = mmbd(n_ab, n_ab)
        lvl = 2
        while lvl < c:
            if lvl * 2 < c:
                xm = mmbd(jnp.concatenate([x, m], axis=0), m)
                x = x + xm[:c]
                m = xm[c:]
            else:
                x = x + mmbd(x, m)
            lvl *= 2
        s = s_ref[g]
        sa = _mm_nt(ar, s)
        av = mmbd(jnp.concatenate([a_ak, a_rk], axis=0), v)
        q = sa[:c] + av[:c]
        u = mmbd(x, q)
        y_ref[:, cs] = sa[c:] + av[c:] + mmbd(a_rb, u)
        uv_t = jnp.concatenate([u, v], axis=0).T
        bk = jnp.concatenate([bv * g_end, key * g_end], axis=0)
        upd = _mm(uv_t, bk)
        s_ref[g] = s * jnp.exp(tot) + jnp.where(bd_mask, upd, 0.0)

    @pl.when(i == nc - 1)
    def _():
        sfin_ref[...] = s_ref[...]


def _rwkv_scan(r, v, kk, lw, key, bvec, s0):
    bsz, l, _ = r.shape
    c = RW_CHUNK
    nc = l // c

    def cidx(d, i):
        return jnp.where(d == 0, i, nc - 1 - i)

    tok = pl.BlockSpec((None, c, D), lambda d, b, i: (b, cidx(d, i), 0))
    tok2 = pl.BlockSpec((None, None, c, D), lambda d, b, i: (d, b, cidx(d, i), 0))
    st = pl.BlockSpec((None, None, RW_NG, RW_GW, RW_GW), lambda d, b, i: (d, b, 0, 0, 0))
    return pl.pallas_call(
        functools.partial(_rwkv_scan_kernel, nc=nc),
        grid=(2, bsz, nc),
        in_specs=[tok, tok, tok, tok2, tok2, tok2, st],
        out_specs=[tok2, st],
        out_shape=[jax.ShapeDtypeStruct((2, bsz, l, D), f32),
                   jax.ShapeDtypeStruct((2, bsz, RW_NG, RW_GW, RW_GW), f32)],
        scratch_shapes=[pltpu.VMEM((RW_NG, RW_GW, RW_GW), f32)],
        compiler_params=_cparams(3),
        name="rwkv_scan",
    )(r, v, kk, lw, key, bvec, s0)


def _rwkv_out_kernel(y_ref, bonus_ref, g_ref, lnw_ref, lnb_ref, ones_ref, o_ref):
    ones_bd = ones_ref[...]
    for g in range(RW_NG):
        cs = slice(g * RW_GW, (g + 1) * RW_GW)
        y = y_ref[0, :, cs] + y_ref[1, :, cs]
        mean = _mm_x3(y, ones_bd) * (1.0 / HEAD)
        yc = y - mean
        var = _mm_x3(yc * yc, ones_bd) * (1.0 / HEAD)
        yn = yc * lax.rsqrt(var + GN_EPS) * lnw_ref[:, cs] + lnb_ref[:, cs]
        o_ref[:, cs] = ((yn + bonus_ref[:, cs]) * g_ref[:, cs]).astype(o_ref.dtype)


def _rwkv_out(y2, bonus, g, ln_w, ln_b, ones_bd):
    _, bsz, l, _ = y2.shape
    tl = min(256, l)
    tok = pl.BlockSpec((None, tl, D), lambda b, i: (b, i, 0))
    vec = pl.BlockSpec((1, D), lambda b, i: (0, 0))
    return pl.pallas_call(
        _rwkv_out_kernel,
        grid=(bsz, l // tl),
        in_specs=[pl.BlockSpec((2, None, tl, D), lambda b, i: (0, b, i, 0)), tok, tok, vec, vec,
                  pl.BlockSpec((RW_GW, RW_GW), lambda b, i: (0, 0))],
        out_specs=tok,
        out_shape=jax.ShapeDtypeStruct((bsz, l, D), bf16),
        compiler_params=_cparams(2),
        name="rwkv_out",
    )(y2, bonus, g, ln_w, ln_b, ones_bd)


def _ssd_kernel(p_ref, prev_ref, next_ref, cw_ref, cb_ref, dtb_ref, a_ref, dsk_ref, e_ref, eq_ref,
                h0_ref, y_ref, hfin_ref, pe_ref, xbc_ref, cole_ref, cumt_ref, dtt_ref, h_ref, *, nc):
    q = SSM_Q
    d = pl.program_id(0)
    i = pl.program_id(2)
    fwd = d == 0
    t = jnp.where(fwd, i, nc - 1 - i)

    @pl.when(i == 0)
    def _():
        h_ref[...] = h0_ref[...]

    zero = jnp.zeros((HALO, D_XBC), f32)
    pe_ref[0:HALO, :] = jnp.where(t > 0, prev_ref[:, D:D + D_XBC], zero)
    pe_ref[HALO:HALO + q, :] = p_ref[:, D:D + D_XBC]
    pe_ref[HALO + q:2 * HALO + q, :] = jnp.where(t < nc - 1, next_ref[:, D:D + D_XBC], zero)
    for n in range(D_XBC // 128):
        cs = slice(n * 128, (n + 1) * 128)
        xc = cb_ref[:, cs]
        for tap in range(CONV_W):
            r0 = HALO - CONV_LEFT + tap
            xc = xc + pe_ref[r0:r0 + q, cs] * cw_ref[tap:tap + 1, cs]
        xbc_ref[:, cs] = _silu(xc)

    row = _iota((q, q), 0)
    col = _iota((q, q), 1)
    sgn = jnp.where(fwd, 1, -1)
    low = (col - row) * sgn <= 0
    tri = jnp.where(low, 1.0, 0.0).astype(bf16)
    tri_t = jnp.where((row - col) * sgn <= 0, 1.0, 0.0).astype(bf16)

    dt = _softplus(p_ref[:, SS_OFF_DT:SS_OFF_DT + 128] + dtb_ref[...])
    dta = dt * a_ref[...]
    cum = _mm_e3(tri, dta)
    tot = jnp.sum(dta, axis=0, keepdims=True)
    cumt_ref[...] = _mm_x3(dta.T, tri_t)
    dtt_ref[...] = dt.T
    e_d = e_ref[...]
    dec_e = _mm_x3(jnp.exp(cum), e_d)
    toend_e = _mm_x3(jnp.exp(tot - cum) * dt, e_d)
    tot_e = _mm_x3(jnp.broadcast_to(jnp.exp(tot), (8, 128)), e_d)[0:1]
    cole_ref[...] = _mm_x3(cum, eq_ref[...])
    lane = _iota((q, 128), 1)

    for g in range(SSM_G):
        bg = xbc_ref[:, D + g * SSM_N:D + (g + 1) * SSM_N]
        cg = xbc_ref[:, D + SSM_G * SSM_N + g * SSM_N:D + SSM_G * SSM_N + (g + 1) * SSM_N]
        cb = _mm_nt(cg, bg)
        gs = slice(g * 512, (g + 1) * 512)
        hg = h_ref[g]
        y_off = _mm(cg, hg) * dec_e[:, gs]
        for pr in range(4):
            ls = []
            for hh in range(2):
                h = g * 8 + pr * 2 + hh
                rowv = cumt_ref[pl.ds(d * 32 + h, 1), :]
                dtr = dtt_ref[pl.ds(d * 32 + h, 1), :]
                seg = cole_ref[:, h * q:(h + 1) * q] - rowv
                ls.append(jnp.where(low, jnp.exp(jnp.where(low, seg, 0.0)), 0.0) * cb * dtr)
            lp = jnp.concatenate(ls, axis=1)
            ps = slice(g * 512 + pr * 128, g * 512 + (pr + 1) * 128)
            xp = xbc_ref[:, ps]
            bd2 = jnp.concatenate([jnp.where(lane < SSM_P, xp, 0.0), jnp.where(lane >= SSM_P, xp, 0.0)],
                                  axis=0)
            yd = _mm(lp, bd2)
            y_ref[:, ps] = yd + y_off[:, pr * 128:(pr + 1) * 128] + dsk_ref[:, ps] * xp
        xs = xbc_ref[:, gs] * toend_e[:, gs]
        states = _mm(bg.T, xs)
        h_ref[g] = hg * tot_e[:, gs] + states

    @pl.when(i == nc - 1)
    def _():
        hfin_ref[...] = h_ref[...]


def _ssd(p, prm, h0):
    bsz, l, _ = p.shape
    q = SSM_Q
    nc = l // q
    nh = l // HALO
    tpb = q // HALO

    def cidx(d, i):
        return jnp.where(d == 0, i, nc - 1 - i)

    full = lambda shape: pl.BlockSpec(shape, lambda d, b, i: (0,) * len(shape))
    st = pl.BlockSpec((None, None, SSM_G, SSM_N, 8 * SSM_P), lambda d, b, i: (d, b, 0, 0, 0))
    return pl.pallas_call(
        functools.partial(_ssd_kernel, nc=nc),
        grid=(2, bsz, nc),
        in_specs=[pl.BlockSpec((None, q, N_SSM_PAD), lambda d, b, i: (b, cidx(d, i), 0)),
                  pl.BlockSpec((None, HALO, N_SSM_PAD),
                               lambda d, b, i: (b, jnp.maximum(cidx(d, i) * tpb - 1, 0), 0)),
                  pl.BlockSpec((None, HALO, N_SSM_PAD),
                               lambda d, b, i: (b, jnp.minimum((cidx(d, i) + 1) * tpb, nh - 1), 0)),
                  full((CONV_W, D_XBC)), full((1, D_XBC)), full((1, 128)), full((1, 128)),
                  pl.BlockSpec((None, 1, D), lambda d, b, i: (d, 0, 0)),
                  pl.BlockSpec((None, 128, D), lambda d, b, i: (d, 0, 0)),
                  pl.BlockSpec((None, 128, SSM_HEADS * q), lambda d, b, i: (d, 0, 0)),
                  st],
        out_specs=[pl.BlockSpec((None, None, q, D), lambda d, b, i: (d, b, cidx(d, i), 0)), st],
        out_shape=[jax.ShapeDtypeStruct((2, bsz, l, D), f32),
                   jax.ShapeDtypeStruct((2, bsz, SSM_G, SSM_N, 8 * SSM_P), f32)],
        scratch_shapes=[pltpu.VMEM((q + 2 * HALO, D_XBC), f32), pltpu.VMEM((q, D_XBC), f32),
                        pltpu.VMEM((q, SSM_HEADS * q), f32), pltpu.VMEM((128, q), f32),
                        pltpu.VMEM((128, q), f32), pltpu.VMEM((SSM_G, SSM_N, 8 * SSM_P), f32)],
        compiler_params=_cparams(3),
        name="ssd",
    )(p, p, p, prm["cw"], prm["cb"], prm["dtb"], prm["a"], prm["dsk"], prm["e"], prm["eq"], h0)


def _ssd_out_kernel(y_ref, z_ref, nw_ref, o_ref):
    gw = D // SSM_G
    for g in range(SSM_G):
        cs = slice(g * gw, (g + 1) * gw)
        yg = (y_ref[0, :, cs] + y_ref[1, :, cs]) * _silu(z_ref[:, cs])
        ms = jnp.mean(yg * yg, axis=-1, keepdims=True)
        o_ref[:, cs] = (yg * lax.rsqrt(ms + EPS) * nw_ref[:, cs]).astype(o_ref.dtype)


def _ssd_out(y2, p, norm_w):
    _, bsz, l, _ = y2.shape
    tl = min(256, l)
    tok = pl.BlockSpec((None, tl, D), lambda b, i: (b, i, 0))
    return pl.pallas_call(
        _ssd_out_kernel,
        grid=(bsz, l // tl),
        in_specs=[pl.BlockSpec((2, None, tl, D), lambda b, i: (0, b, i, 0)), tok,
                  pl.BlockSpec((1, D), lambda b, i: (0, 0))],
        out_specs=tok,
        out_shape=jax.ShapeDtypeStruct((bsz, l, D), bf16),
        compiler_params=_cparams(2),
        name="ssd_out",
    )(y2, p, norm_w)


def _merge_kernel(ya_ref, yb_ref, yc_ref, sa_ref, sb_ref, sc_ref, wa_ref, wb_ref, wc_ref, o_ref):
    acc = None
    for y_ref, s_ref, w_ref in ((ya_ref, sa_ref, wa_ref), (yb_ref, sb_ref, wb_ref), (yc_ref, sc_ref, wc_ref)):
        t = s_ref[...].astype(f32) * jnp.dot(y_ref[...], w_ref[...], preferred_element_type=f32)
        acc = t if acc is None else acc + t
    o_ref[...] = acc.astype(o_ref.dtype)


def _merge(ys, sig, ws):
    m = ys[0].shape[0]
    tm = min(512, m)
    tn = 512
    nj = D // tn
    ysp = pl.BlockSpec((tm, D), lambda i, j: (i, 0))
    wsp = pl.BlockSpec((D, tn), lambda i, j: (0, j))
    ssp = [pl.BlockSpec((tm, tn), functools.partial(lambda i, j, k: (i, k * nj + j), k=k)) for k in range(3)]
    return pl.pallas_call(
        _merge_kernel,
        grid=(m // tm, nj),
        in_specs=[ysp, ysp, ysp] + ssp + [wsp, wsp, wsp],
        out_specs=pl.BlockSpec((tm, tn), lambda i, j: (i, j)),
        out_shape=jax.ShapeDtypeStruct((m, D), bf16),
        compiler_params=_cparams(2),
        name="merge",
    )(ys[0], ys[1], ys[2], sig, sig, sig, ws[0], ws[1], ws[2])


def _wo_kernel(m_ref, w_ref, h_ref, g_ref, o_ref):
    o_ref[...] = h_ref[...] + g_ref[...] * jnp.dot(m_ref[...], w_ref[...], preferred_element_type=f32)


def _wo_residual(m, w_o, h, mod4, row_of_batch, k_gate, *, transposed):
    bsz, l, _ = h.shape
    if transposed:
        rows = l // GRID_W
        hv = h.reshape(bsz, rows, GRID_W * D)
        tl, nt = rows, GRID_W
        hspec = pl.BlockSpec((None, tl, D), lambda b, i: (b, 0, i))
        oshape = jax.ShapeDtypeStruct((bsz, rows, GRID_W * D), f32)
    else:
        tl = min(512, l)
        nt = l // tl
        hv = h
        hspec = pl.BlockSpec((None, tl, D), lambda b, i: (b, i, 0))
        oshape = jax.ShapeDtypeStruct((bsz, l, D), f32)
    out = pl.pallas_call(
        _wo_kernel,
        grid=(bsz, nt),
        in_specs=[pl.BlockSpec((None, tl, D), lambda b, i: (b, i, 0)),
                  pl.BlockSpec((D, D), lambda b, i: (0, 0)),
                  hspec,
                  pl.BlockSpec((None, None, 1, D), lambda b, i: (row_of_batch(b), k_gate, 0, 0))],
        out_specs=hspec,
        out_shape=oshape,
        compiler_params=_cparams(2),
        name="wo_residual",
    )(m, w_o, hv, mod4)
    return out.reshape(bsz, l, D)


def _router_kernel(x_ref, g_ref, sc_ref, sh_ref, r_ref, u_ref, gates_ref):
    x = x_ref[...]
    ms = jnp.mean(x * x, axis=-1, keepdims=True)
    u = x * lax.rsqrt(ms + EPS) * g_ref[...] * (1.0 + sc_ref[...]) + sh_ref[...]
    u_ref[...] = u.astype(u_ref.dtype)
    u0, u1, u2 = _split3(u)
    r0, r1, r2 = _split3(r_ref[...])
    dot = lambda a, b: jnp.dot(a, b, preferred_element_type=f32)
    logits = (dot(u0, r0) + (dot(u0, r1) + dot(u1, r0))
              + (dot(u1, r1) + dot(u0, r2) + dot(u2, r0)))
    lane = _iota(logits.shape, 1)
    neg = jnp.float32(-jnp.inf)
    lg = jnp.where(lane < N_EXPERTS, logits, neg)
    m1 = jnp.max(lg, axis=-1, keepdims=True)
    i1 = jnp.min(jnp.where(lg == m1, lane, 128), axis=-1, keepdims=True)
    lg2 = jnp.where(lane == i1, neg, lg)
    m2 = jnp.max(lg2, axis=-1, keepdims=True)
    i2 = jnp.min(jnp.where(lg2 == m2, lane, 128), axis=-1, keepdims=True)
    e2 = jnp.exp(m2 - m1)
    den = 1.0 + e2
    gates_ref[...] = jnp.where(lane == i1, 1.0 / den, 0.0) + jnp.where(lane == i2, e2 / den, 0.0)


def _norm_router(h, gain, mod4, row_of_batch, k_sc, k_sh, router_pad):
    bsz, l, _ = h.shape
    tl = min(512, l)
    tok = pl.BlockSpec((None, tl, D), lambda b, i: (b, i, 0))
    return pl.pallas_call(
        _router_kernel,
        grid=(bsz, l // tl),
        in_specs=[tok,
                  pl.BlockSpec((1, D), lambda b, i: (0, 0)),
                  pl.BlockSpec((None, None, 1, D), lambda b, i: (row_of_batch(b), k_sc, 0, 0)),
                  pl.BlockSpec((None, None, 1, D), lambda b, i: (row_of_batch(b), k_sh, 0, 0)),
                  pl.BlockSpec((D, 128), lambda b, i: (0, 0))],
        out_specs=[tok, pl.BlockSpec((None, tl, 128), lambda b, i: (b, i, 0))],
        out_shape=[jax.ShapeDtypeStruct((bsz, l, D), bf16), jax.ShapeDtypeStruct((bsz, l, 128), f32)],
        compiler_params=_cparams(2),
        name="norm_router",
    )(h, gain.reshape(1, D), mod4, mod4, router_pad)


def _up_kernel(*refs, gated):
    if gated:
        u_ref, w1_ref, w3_ref, gt_ref, o_ref = refs
    else:
        u_ref, w1_ref, w3_ref, o_ref = refs
    u = u_ref[...]
    a = jnp.dot(u, w1_ref[...], preferred_element_type=f32)
    b = jnp.dot(u, w3_ref[...], preferred_element_type=f32)
    act = _silu(a) * b
    if gated:
        e = pl.program_id(1)
        gt = gt_ref[...]
        lane = _iota(gt.shape, 1)
        act = act * jnp.sum(jnp.where(lane == e, gt, 0.0), axis=-1, keepdims=True)
    o_ref[...] = act.astype(o_ref.dtype)


def _swiglu_up(u, w1, w3, gates=None):
    m = u.shape[0]
    ne, _, ff = w1.shape
    tm = min(1024, m)
    tn = 512
    nj = ff // tn
    gated = gates is not None
    wspec = pl.BlockSpec((None, D, tn), lambda i, e, j: (e, 0, j))
    in_specs = [pl.BlockSpec((tm, D), lambda i, e, j: (i, 0)), wspec, wspec]
    args = [u, w1, w3]
    if gated:
        in_specs.append(pl.BlockSpec((tm, 128), lambda i, e, j: (i, 0)))
        args.append(gates)
    return pl.pallas_call(
        functools.partial(_up_kernel, gated=gated),
        grid=(m // tm, ne, nj),
        in_specs=in_specs,
        out_specs=pl.BlockSpec((tm, tn), lambda i, e, j: (i, e * nj + j)),
        out_shape=jax.ShapeDtypeStruct((m, ne * ff), bf16),
        compiler_params=_cparams(3),
        name="swiglu_up",
    )(*args)


def _down_kernel(a_ref, w_ref, h_ref, g_ref, o_ref, acc_ref, *, nk):
    k = pl.program_id(2)

    @pl.when(k == 0)
    def _():
        acc_ref[...] = jnp.zeros_like(acc_ref)

    acc_ref[...] += jnp.dot(a_ref[...], w_ref[...], preferred_element_type=f32)

    @pl.when(k == nk - 1)
    def _():
        o_ref[...] = h_ref[...] + g_ref[...] * acc_ref[...]


def _down_residual(act, w2, h, mod4, row_of_batch, k_gate):
    bsz, l, kk = act.shape
    tl = min(512, l)
    tk = 1024
    nk = kk // tk
    return pl.pallas_call(
        functools.partial(_down_kernel, nk=nk),
        grid=(bsz, l // tl, nk),
        in_specs=[pl.BlockSpec((None, tl, tk), lambda b, i, k: (b, i, k)),
                  pl.BlockSpec((tk, D), lambda b, i, k: (k, 0)),
                  pl.BlockSpec((None, tl, D), lambda b, i, k: (b, i, 0)),
                  pl.BlockSpec((None, None, 1, D), lambda b, i, k: (row_of_batch(b), k_gate, 0, 0))],
        out_specs=pl.BlockSpec((None, tl, D), lambda b, i, k: (b, i, 0)),
        out_shape=jax.ShapeDtypeStruct((bsz, l, D), f32),
        scratch_shapes=[pltpu.VMEM((tl, D), f32)],
        compiler_params=_cparams(3),
        name="down_residual",
    )(act, w2, h, mod4)


def _block_ones(n, bs):
    idx = jnp.arange(n) // bs
    return (idx[:, None] == idx[None, :]).astype(bf16)


def _pad_cols(w, n):
    return jnp.pad(w, ((0, 0), (0, n - w.shape[1])))


def _rwkv_params(li, mu, w0, w_up, a0, a_up, g_up, k_k, k_a, r_k, ln_w, ln_b):
    wup = jnp.zeros((2, 256, D), f32)
    aup = jnp.zeros((2, 256, D), f32)
    for d in range(2):
        wup = wup.at[d, d * W_LORA:(d + 1) * W_LORA].set(w_up[li, d])
        o = (RW_OFF_G - 2 * A_LORA) - RW_OFF_A + d * A_LORA
        aup = aup.at[d, o:o + A_LORA].set(a_up[li, d])
    return {
        "mu": jnp.pad(mu[li], (0, N_RWKV_PAD - N_RWKV)).reshape(1, N_RWKV_PAD),
        "w0": w0[li], "wup": wup.astype(bf16), "a0": a0[li], "aup": aup.astype(bf16),
        "gup": g_up[li].astype(bf16), "k_k": k_k[li].reshape(1, D), "k_a": k_a[li].reshape(1, D),
        "r_k": r_k[li].reshape(1, D), "ln_w": ln_w[li].reshape(1, D), "ln_b": ln_b[li].reshape(1, D),
        "ones_bd": _block_ones(RW_GW, HEAD),
    }


def _ssd_params(li, conv_w, conv_b, a_log, dt_bias, d_skip):
    a = -jnp.exp(a_log[li].astype(f32))
    hid = jnp.arange(D) // SSM_P
    hq = jnp.arange(SSM_HEADS * SSM_Q) // SSM_Q
    sel = jnp.arange(128)[:, None]
    e = jnp.stack([(sel == hid[None, :] + d * SSM_HEADS) for d in range(2)]).astype(bf16)
    eq = jnp.stack([(sel == hq[None, :] + d * SSM_HEADS) for d in range(2)]).astype(bf16)
    return {
        "cw": conv_w[li], "cb": conv_b[li].reshape(1, D_XBC),
        "dtb": jnp.pad(dt_bias[li].reshape(1, 2 * SSM_HEADS), ((0, 0), (0, 64))),
        "a": jnp.pad(a.reshape(1, 2 * SSM_HEADS), ((0, 0), (0, 64))),
        "dsk": jnp.stack([jnp.repeat(d_skip[li], SSM_P), jnp.zeros((D,), f32)]).reshape(2, 1, D),
        "e": e, "eq": eq,
    }


def kernel(x, c, ctx, c_ctx, ada_w, ada_b, norm_mix, norm_ffn, norm_final, w_in, lru_conv_w, lru_conv_b, lru_gate_w, lru_gate_b, lru_lambda, rwkv_mu, rwkv_w0, rwkv_w_up, rwkv_a0, rwkv_a_up, rwkv_g_up, rwkv_k_k, rwkv_k_a, rwkv_r_k, rwkv_ln_w, rwkv_ln_b, ssm_conv_w, ssm_conv_b, ssm_a_log, ssm_dt_bias, ssm_d, ssm_norm_w, w_out_lru, w_out_rwkv, w_out_ssm, w_o, ffn_w1, ffn_w3, ffn_w2, moe_router, moe_w1, moe_w3, moe_w2):
    bsz, l, _ = x.shape
    depth = ada_w.shape[0]
    off_lru = 3 * D
    off_rwkv = off_lru + 2 * D
    off_ssm = off_rwkv + N_RWKV
    cvec = jnp.zeros((8, D), f32).at[:bsz].set(c).at[bsz].set(c_ctx)
    lat_row = lambda b: b
    ctx_row = lambda b: bsz

    h_lat, h_ctx = x, ctx
    for li in range(depth):
        last = li == depth - 1
        odd = li % 2 == 1
        mod4 = _ada(cvec, ada_w[li], ada_b[li]).reshape(8, 6, 1, D)

        wl = w_in[li]
        w_gate = wl[:, :off_lru].astype(bf16)
        w_lru = wl[:, off_lru:off_rwkv].astype(bf16)
        w_rw = _pad_cols(wl[:, off_rwkv:off_ssm], N_RWKV_PAD).astype(bf16)
        w_ss = _pad_cols(wl[:, off_ssm:], N_SSM_PAD).astype(bf16)
        lru_gw = [jnp.concatenate([lru_gate_w[li, d, 0], lru_gate_w[li, d, 1]], axis=-1).astype(bf16)
                  for d in range(2)]
        rp = _rwkv_params(li, rwkv_mu, rwkv_w0, rwkv_w_up, rwkv_a0, rwkv_a_up, rwkv_g_up, rwkv_k_k,
                          rwkv_k_a, rwkv_r_k, rwkv_ln_w, rwkv_ln_b)
        sp = _ssd_params(li, ssm_conv_w, ssm_conv_b, ssm_a_log, ssm_dt_bias, ssm_d)
        w_outs = [w_out_lru[li].astype(bf16), w_out_rwkv[li].astype(bf16), w_out_ssm[li].astype(bf16)]
        w_o_b = w_o[li].astype(bf16)

        def token_mix(u, states, need_out):
            lx = u.shape[1]
            um = u.reshape(bsz * lx, D)
            p_lru = _proj(um, w_lru, tn=1024).reshape(bsz, lx, 2 * D)
            p_rw = _proj(um, w_rw, tn=768).reshape(bsz, lx, N_RWKV_PAD)
            p_ss = _proj(um, w_ss, tn=768).reshape(bsz, lx, N_SSM_PAD)
            lru_s, rw_s, ss_s = states
            cw, cb = lru_conv_w[li], lru_conv_b[li].reshape(1, D)
            hb, hl_b = _lru_pass(p_lru, cw, cb, lru_gw[1], lru_gate_b[li, 1], lru_lambda[li, 1].reshape(1, D),
                                 lru_s[1], None, reverse=True)
            ya, hl_f = _lru_pass(p_lru, cw, cb, lru_gw[0], lru_gate_b[li, 0], lru_lambda[li, 0].reshape(1, D),
                                 lru_s[0], hb, reverse=False)
            r, v, kk, g, bonus, lw, key, bvec = _rwkv_feat(p_rw, rp)
            y2, rw_fin = _rwkv_scan(r, v, kk, lw, key, bvec, rw_s)
            ys2, ss_fin = _ssd(p_ss, sp, ss_s)
            new_states = ((hl_f, hl_b), rw_fin, ss_fin)
            if not need_out:
                return None, new_states
            yb = _rwkv_out(y2, bonus, g, rp["ln_w"], rp["ln_b"], rp["ones_bd"])
            yc = _ssd_out(ys2, p_ss, ssm_norm_w[li].reshape(1, D))
            sig = _proj(um, w_gate, tn=1024, act="sigmoid", out_dtype=bf16)
            m = _merge([ya.reshape(-1, D), yb.reshape(-1, D), yc.reshape(-1, D)], sig, w_outs)
            return m.reshape(bsz, lx, D), new_states

        zero_states = ((jnp.zeros((bsz, 1, D), f32), jnp.zeros((bsz, 1, D), f32)),
                       jnp.zeros((2, bsz, RW_NG, RW_GW, RW_GW), f32),
                       jnp.zeros((2, bsz, SSM_G, SSM_N, 8 * SSM_P), f32))

        u_ctx = _norm(h_ctx, norm_mix[li], mod4, ctx_row, 1, 0)
        u_lat = _norm(h_lat, norm_mix[li], mod4, lat_row, 1, 0, transposed=odd)
        m_ctx, ctx_states = token_mix(u_ctx, zero_states, not last)
        m_lat, _ = token_mix(u_lat, ctx_states, True)
        h_lat = _wo_residual(m_lat, w_o_b, h_lat, mod4, lat_row, 2, transposed=odd)
        if not last:
            h_ctx = _wo_residual(m_ctx, w_o_b, h_ctx, mod4, ctx_row, 2, transposed=False)

        j = li // 2
        streams = [(h_lat, lat_row)] + ([] if last else [(h_ctx, ctx_row)])
        outs = []
        for h, row_fn in streams:
            lx = h.shape[1]
            if not odd:
                v = _norm(h, norm_ffn[li], mod4, row_fn, 4, 3)
                act = _swiglu_up(v.reshape(bsz * lx, D), ffn_w1[j].astype(bf16)[None],
                                 ffn_w3[j].astype(bf16)[None])
                w2 = ffn_w2[j].astype(bf16)
            else:
                router_pad = _pad_cols(moe_router[j], 128)
                v, gates = _norm_router(h, norm_ffn[li], mod4, row_fn, 4, 3, router_pad)
                act = _swiglu_up(v.reshape(bsz * lx, D), moe_w1[j].astype(bf16), moe_w3[j].astype(bf16),
                                 gates.reshape(bsz * lx, 128))
                w2 = moe_w2[j].astype(bf16).reshape(N_EXPERTS * D_FF_EXPERT, D)
            outs.append(_down_residual(act.reshape(bsz, lx, -1), w2, h, mod4, row_fn, 5))
        h_lat = outs[0]
        if not last:
            h_ctx = outs[1]
    return _final_norm(h_lat, norm_final)
```

```python
import functools
import math

import jax
import jax.numpy as jnp
from jax import lax
from jax.experimental import pallas as pl
from jax.experimental.pallas import tpu as pltpu

f32 = jnp.float32
bf16 = jnp.bfloat16

D = 2048
GRID_W = 64
EPS = 1e-6
CONV_W = 4
CONV_LEFT = 2
HALO = 8

LRU_BLOCKS = 16
LRU_BS = D // LRU_BLOCKS
LRU_C = 8.0

HEADS = 32
HEAD = 64
W_LORA = 96
A_LORA = 96
G_LORA = 256
GN_EPS = 64e-5
N_RWKV = 3 * D + 2 * W_LORA + 2 * A_LORA + G_LORA
N_RWKV_PAD = 6912
RW_OFF_W = 3 * D
RW_OFF_A = 3 * D + 128
RW_OFF_G = 3 * D + 2 * W_LORA + 2 * A_LORA
RW_CHUNK = 64
RW_GW = 256
RW_NG = D // RW_GW

SSM_HEADS = 32
SSM_P = 64
SSM_N = 128
SSM_G = 4
SSM_Q = 128
D_XBC = D + 2 * SSM_G * SSM_N
N_SSM = D + D_XBC + 2 * SSM_HEADS
N_SSM_PAD = 5376
SS_OFF_DT = D + D_XBC

D_FF = 3 * D
N_EXPERTS = 8
D_FF_EXPERT = D_FF // 2

VMEM_LIMIT = 56 * 1024 * 1024


def _cparams(n_axes, vmem=VMEM_LIMIT):
    return pltpu.CompilerParams(dimension_semantics=("arbitrary",) * n_axes, vmem_limit_bytes=vmem)


def _mm(a, b):
    return jnp.dot(a.astype(bf16), b.astype(bf16), preferred_element_type=f32)


def _mm_nt(a, b):
    return lax.dot_general(a.astype(bf16), b.astype(bf16), (((1,), (1,)), ((), ())),
                           preferred_element_type=f32)


def _split3(x):
    x0 = x.astype(bf16)
    r = x - x0.astype(f32)
    x1 = r.astype(bf16)
    r = r - x1.astype(f32)
    return x0, x1, r.astype(bf16)


def _mm_x3(x, e):
    x0, x1, x2 = _split3(x)
    return (jnp.dot(x0, e, preferred_element_type=f32) + jnp.dot(x1, e, preferred_element_type=f32)
            + jnp.dot(x2, e, preferred_element_type=f32))


def _mm_e3(e, x):
    x0, x1, x2 = _split3(x)
    return (jnp.dot(e, x0, preferred_element_type=f32) + jnp.dot(e, x1, preferred_element_type=f32)
            + jnp.dot(e, x2, preferred_element_type=f32))


def _softplus(x):
    return jnp.maximum(x, 0.0) + jnp.log1p(jnp.exp(-jnp.abs(x)))


def _sigmoid(x):
    return jax.nn.sigmoid(x)


def _silu(x):
    return x * jax.nn.sigmoid(x)


def _iota(shape, dim):
    return lax.broadcasted_iota(jnp.int32, shape, dim)


def _ada_kernel(c_ref, w_ref, b_ref, o_ref):
    cv = c_ref[...]
    o_ref[...] = _mm(_silu(cv), w_ref[...]) + b_ref[...]


def _ada(cvec, w, b):
    n = w.shape[1]
    tn = 1536
    return pl.pallas_call(
        _ada_kernel,
        grid=(n // tn,),
        in_specs=[pl.BlockSpec((8, D), lambda j: (0, 0)),
                  pl.BlockSpec((D, tn), lambda j: (0, j)),
                  pl.BlockSpec((1, tn), lambda j: (0, j))],
        out_specs=pl.BlockSpec((8, tn), lambda j: (0, j)),
        out_shape=jax.ShapeDtypeStruct((8, n), f32),
        compiler_params=_cparams(1),
        name="ada",
    )(cvec, w, b.reshape(1, n))


def _norm_kernel(x_ref, g_ref, sc_ref, sh_ref, o_ref):
    x = x_ref[...]
    ms = jnp.mean(x * x, axis=-1, keepdims=True)
    xn = x * lax.rsqrt(ms + EPS) * g_ref[...]
    o_ref[...] = (xn * (1.0 + sc_ref[...]) + sh_ref[...]).astype(o_ref.dtype)


def _norm(h, gain, mod4, row_of_batch, k_sc, k_sh, *, transposed=False, out_dtype=bf16):
    bsz, l, _ = h.shape
    if transposed:
        rows = l // GRID_W
        hin = h.reshape(bsz, rows, GRID_W * D)
        tl = rows
        nt = GRID_W
        in_spec = pl.BlockSpec((None, tl, D), lambda b, i: (b, 0, i))
    else:
        tl = min(512, l)
        nt = l // tl
        hin = h
        in_spec = pl.BlockSpec((None, tl, D), lambda b, i: (b, i, 0))
    return pl.pallas_call(
        _norm_kernel,
        grid=(bsz, nt),
        in_specs=[in_spec,
                  pl.BlockSpec((1, D), lambda b, i: (0, 0)),
                  pl.BlockSpec((None, None, 1, D), lambda b, i: (row_of_batch(b), k_sc, 0, 0)),
                  pl.BlockSpec((None, None, 1, D), lambda b, i: (row_of_batch(b), k_sh, 0, 0))],
        out_specs=pl.BlockSpec((None, tl, D), lambda b, i: (b, i, 0)),
        out_shape=jax.ShapeDtypeStruct((bsz, l, D), out_dtype),
        compiler_params=_cparams(2),
        name="norm",
    )(hin, gain.reshape(1, D), mod4, mod4)


def _plain_norm_kernel(x_ref, g_ref, o_ref):
    x = x_ref[...]
    ms = jnp.mean(x * x, axis=-1, keepdims=True)
    o_ref[...] = (x * lax.rsqrt(ms + EPS) * g_ref[...]).astype(o_ref.dtype)


def _final_norm(h, gain):
    bsz, l, _ = h.shape
    tl = min(512, l)
    return pl.pallas_call(
        _plain_norm_kernel,
        grid=(bsz, l // tl),
        in_specs=[pl.BlockSpec((None, tl, D), lambda b, i: (b, i, 0)),
                  pl.BlockSpec((1, D), lambda b, i: (0, 0))],
        out_specs=pl.BlockSpec((None, tl, D), lambda b, i: (b, i, 0)),
        out_shape=jax.ShapeDtypeStruct((bsz, l, D), f32),
        compiler_params=_cparams(2),
        name="final_norm",
    )(h, gain.reshape(1, D))


def _proj_kernel(u_ref, w_ref, o_ref, *, act):
    acc = jnp.dot(u_ref[...], w_ref[...], preferred_element_type=f32)
    if act == "sigmoid":
        acc = _sigmoid(acc)
    o_ref[...] = acc.astype(o_ref.dtype)


def _proj(u, w, *, tn, act=None, out_dtype=f32):
    m, k = u.shape
    n = w.shape[1]
    tm = min(1024, m)
    return pl.pallas_call(
        functools.partial(_proj_kernel, act=act),
        grid=(m // tm, n // tn),
        in_specs=[pl.BlockSpec((tm, k), lambda i, j: (i, 0)),
                  pl.BlockSpec((k, tn), lambda i, j: (0, j))],
        out_specs=pl.BlockSpec((tm, tn), lambda i, j: (i, j)),
        out_shape=jax.ShapeDtypeStruct((m, n), out_dtype),
        compiler_params=_cparams(2),
        name="proj",
    )(u, w)


def _lru_kernel(*refs, reverse, final, tl, nt):
    if final:
        (x_ref, prev_ref, next_ref, gate_ref, hb_ref, cw_ref, cb_ref, gw_ref, gb_ref, lam_ref, h0_ref,
         out_ref, hlast_ref, xe_ref, a_ref, bx_ref, hs_ref, h_ref) = refs
    else:
        (x_ref, prev_ref, next_ref, cw_ref, cb_ref, gw_ref, gb_ref, lam_ref, h0_ref,
         out_ref, hlast_ref, xe_ref, a_ref, bx_ref, hs_ref, h_ref) = refs
    i = pl.program_id(1)
    t = (nt - 1 - i) if reverse else i

    @pl.when(i == 0)
    def _():
        h_ref[...] = h0_ref[...]

    zero = jnp.zeros((HALO, D), f32)
    xe_ref[0:HALO, :] = jnp.where(t > 0, prev_ref[...], zero)
    xe_ref[HALO:HALO + tl, :] = x_ref[...]
    xe_ref[HALO + tl:2 * HALO + tl, :] = jnp.where(t < nt - 1, next_ref[...], zero)

    for n in range(LRU_BLOCKS):
        cs = slice(n * LRU_BS, (n + 1) * LRU_BS)
        xc = cb_ref[:, cs]
        for tap in range(CONV_W):
            r0 = HALO - CONV_LEFT + tap
            xc = xc + xe_ref[r0:r0 + tl, cs] * cw_ref[tap:tap + 1, cs]
        g = _mm(xc, gw_ref[n])
        rec = _sigmoid(g[:, :LRU_BS] + gb_ref[0:1, cs])
        inp = _sigmoid(g[:, LRU_BS:] + gb_ref[1:2, cs])
        log_a = -LRU_C * rec * _softplus(-lam_ref[:, cs])
        a_ref[:, cs] = jnp.exp(log_a)
        th = jnp.tanh(log_a)
        bx_ref[:, cs] = jnp.sqrt(-2.0 * th / (1.0 - th)) * inp * xc

    def body(s, h):
        tt = (tl - 1 - s) if reverse else s
        h = a_ref[pl.ds(tt, 1), :] * h + bx_ref[pl.ds(tt, 1), :]
        hs_ref[pl.ds(tt, 1), :] = h
        return h

    h = lax.fori_loop(0, tl, body, h_ref[...], unroll=8)
    h_ref[...] = h

    @pl.when(i == nt - 1)
    def _():
        hlast_ref[...] = h

    if final:
        for n in range(LRU_BLOCKS):
            cs = slice(n * LRU_BS, (n + 1) * LRU_BS)
            y = (hs_ref[:, cs] + hb_ref[:, cs]) * jax.nn.gelu(gate_ref[:, cs])
            out_ref[:, cs] = y.astype(out_ref.dtype)
    else:
        out_ref[...] = hs_ref[...]


def _lru_pass(p, cw, cb, gw, gb, lam, h0, hb, *, reverse):
    bsz, l, _ = p.shape
    tl = min(256, l)
    nt = l // tl
    final = hb is not None
    nh = l // HALO
    tpb = tl // HALO

    def tmap(i):
        return (nt - 1 - i) if reverse else i

    main = pl.BlockSpec((None, tl, D), lambda b, i: (b, tmap(i), 0))
    in_specs = [main,
                pl.BlockSpec((None, HALO, D), lambda b, i: (b, jnp.maximum(tmap(i) * tpb - 1, 0), 0)),
                pl.BlockSpec((None, HALO, D), lambda b, i: (b, jnp.minimum((tmap(i) + 1) * tpb, nh - 1), 0))]
    args = [p, p, p]
    if final:
        in_specs += [pl.BlockSpec((None, tl, D), lambda b, i: (b, tmap(i), 1)), main]
        args += [p, hb]
    in_specs += [pl.BlockSpec((CONV_W, D), lambda b, i: (0, 0)),
                 pl.BlockSpec((1, D), lambda b, i: (0, 0)),
                 pl.BlockSpec((LRU_BLOCKS, LRU_BS, 2 * LRU_BS), lambda b, i: (0, 0, 0)),
                 pl.BlockSpec((2, D), lambda b, i: (0, 0)),
                 pl.BlockSpec((1, D), lambda b, i: (0, 0)),
                 pl.BlockSpec((None, 1, D), lambda b, i: (b, 0, 0))]
    args += [cw, cb, gw, gb, lam, h0]
    out, hlast = pl.pallas_call(
        functools.partial(_lru_kernel, reverse=reverse, final=final, tl=tl, nt=nt),
        grid=(bsz, nt),
        in_specs=in_specs,
        out_specs=[main, pl.BlockSpec((None, 1, D), lambda b, i: (b, 0, 0))],
        out_shape=[jax.ShapeDtypeStruct((bsz, l, D), bf16 if final else f32),
                   jax.ShapeDtypeStruct((bsz, 1, D), f32)],
        scratch_shapes=[pltpu.VMEM((tl + 2 * HALO, D), f32), pltpu.VMEM((tl, D), f32),
                        pltpu.VMEM((tl, D), f32), pltpu.VMEM((tl, D), f32), pltpu.VMEM((1, D), f32)],
        compiler_params=_cparams(2),
        name="lru_fwd" if final else "lru_bwd",
    )(*args)
    return out, hlast


def _rwkv_feat_kernel(p_ref, prev_ref, next_ref, mu_ref, w0_ref, wup_ref, a0_ref, aup_ref, gup_ref,
                      kk_ref, ka_ref, rk_ref, ones_ref,
                      r_out, v_out, kkv_out, g_out, bonus_out, lw_out, key_out, b_out,
                      pe_ref, *, tl, nt):
    i = pl.program_id(1)
    zero = jnp.zeros((HALO, N_RWKV_PAD), f32)
    pe_ref[0:HALO, :] = jnp.where(i > 0, prev_ref[...], zero)
    pe_ref[HALO:HALO + tl, :] = p_ref[...]
    pe_ref[HALO + tl:2 * HALO + tl, :] = jnp.where(i < nt - 1, next_ref[...], zero)

    def shifted(c0, width):
        cs = slice(c0, c0 + width)
        p = pe_ref[HALO:HALO + tl, cs]
        nb = 0.5 * (pe_ref[HALO - 1:HALO - 1 + tl, cs] + pe_ref[HALO + 1:HALO + 1 + tl, cs])
        return p + mu_ref[:, cs] * (nb - p)

    win_w = jnp.tanh(shifted(RW_OFF_W, 256)).astype(bf16)
    win_a = shifted(RW_OFF_A, 256).astype(bf16)
    win_g = _sigmoid(shifted(RW_OFF_G, 256)).astype(bf16)
    ones_bd = ones_ref[...]

    for g in range(RW_NG):
        cs = slice(g * RW_GW, (g + 1) * RW_GW)
        r = shifted(g * RW_GW, RW_GW)
        k = shifted(D + g * RW_GW, RW_GW)
        v = shifted(2 * D + g * RW_GW, RW_GW)
        kf = k * kk_ref[:, cs]
        kk = kf * lax.rsqrt(_mm_x3(kf * kf, ones_bd) + 1e-12)
        ksum = None
        for d in range(2):
            wv = -_softplus(-(w0_ref[d:d + 1, cs] + jnp.dot(win_w, wup_ref[d, :, cs],
                                                              preferred_element_type=f32))) - 0.5
            lw_out[d, :, cs] = -jnp.exp(wv)
            a = _sigmoid(a0_ref[d:d + 1, cs] + jnp.dot(win_a, aup_ref[d, :, cs],
                                                       preferred_element_type=f32))
            key = k * (1.0 + (a - 1.0) * ka_ref[:, cs])
            key_out[d, :, cs] = key
            b_out[d, :, cs] = kk * a
            ksum = key if ksum is None else ksum + key
        r_out[:, cs] = r
        v_out[:, cs] = v
        kkv_out[:, cs] = kk
        bonus_out[:, cs] = _mm_x3(r * ksum * rk_ref[:, cs], ones_bd) * v
        g_out[:, cs] = jnp.dot(win_g, gup_ref[:, cs], preferred_element_type=f32)


def _rwkv_feat(p, prm):
    bsz, l, _ = p.shape
    tl = min(128, l)
    nt = l // tl
    nh = l // HALO
    tpb = tl // HALO
    full2 = lambda shape: pl.BlockSpec(shape, lambda b, i: (0,) * len(shape))
    tok = pl.BlockSpec((None, tl, D), lambda b, i: (b, i, 0))
    tok2 = pl.BlockSpec((2, None, tl, D), lambda b, i: (0, b, i, 0))
    sd = jax.ShapeDtypeStruct((bsz, l, D), f32)
    sd2 = jax.ShapeDtypeStruct((2, bsz, l, D), f32)
    return pl.pallas_call(
        functools.partial(_rwkv_feat_kernel, tl=tl, nt=nt),
        grid=(bsz, nt),
        in_specs=[pl.BlockSpec((None, tl, N_RWKV_PAD), lambda b, i: (b, i, 0)),
                  pl.BlockSpec((None, HALO, N_RWKV_PAD), lambda b, i: (b, jnp.maximum(i * tpb - 1, 0), 0)),
                  pl.BlockSpec((None, HALO, N_RWKV_PAD),
                               lambda b, i: (b, jnp.minimum((i + 1) * tpb, nh - 1), 0)),
                  full2((1, N_RWKV_PAD)), full2((2, D)), full2((2, 256, D)), full2((2, D)),
                  full2((2, 256, D)), full2((256, D)), full2((1, D)), full2((1, D)), full2((1, D)),
                  full2((RW_GW, RW_GW))],
        out_specs=[tok, tok, tok, tok, tok, tok2, tok2, tok2],
        out_shape=[sd, sd, sd, sd, sd, sd2, sd2, sd2],
        scratch_shapes=[pltpu.VMEM((tl + 2 * HALO, N_RWKV_PAD), f32)],
        compiler_params=_cparams(2),
        name="rwkv_feat",
    )(p, p, p, prm["mu"], prm["w0"], prm["wup"], prm["a0"], prm["aup"], prm["gup"],
      prm["k_k"], prm["k_a"], prm["r_k"], prm["ones_bd"])


def _bd_expand(y, lane_head):
    yb = y.astype(f32)
    return jnp.concatenate([jnp.where(lane_head == h, yb, 0.0).astype(bf16) for h in range(4)], axis=0)


def _rwkv_scan_kernel(r_ref, v_ref, kk_ref, lw_ref, key_ref, b_ref, s0_ref, y_ref, sfin_ref,
                      s_ref, *, nc):
    c = RW_CHUNK
    d = pl.program_id(0)
    i = pl.program_id(2)
    fwd = d == 0

    @pl.when(i == 0)
    def _():
        s_ref[...] = s0_ref[...]

    row = _iota((c, c), 0)
    col = _iota((c, c), 1)
    sgn = jnp.where(fwd, 1, -1)
    tri = jnp.where((col - row) * sgn <= 0, 1.0, 0.0).astype(bf16)
    t4 = _iota((c, 4 * c), 0)
    j4 = _iota((c, 4 * c), 1) % c
    mask_s = (j4 - t4) * sgn < 0
    mask_i = (j4 - t4) * sgn <= 0
    eye4 = jnp.where(j4 == t4, 1.0, 0.0)
    lane_head = _iota((c, RW_GW), 1) // HEAD
    bd_mask = (_iota((RW_GW, RW_GW), 0) // HEAD) == (_iota((RW_GW, RW_GW), 1) // HEAD)

    def mmbd(x, y):
        return jnp.dot(x.astype(bf16), _bd_expand(y, lane_head), preferred_element_type=f32)

    groups = range(RW_NG)
    sl = [slice(g * RW_GW, (g + 1) * RW_GW) for g in groups]
    lw = [lw_ref[:, sl[g]] for g in groups]
    cl = [_mm_e3(tri, lw[g]) for g in groups]
    tot = [jnp.sum(lw[g], axis=0, keepdims=True) for g in groups]
    v = [v_ref[:, sl[g]] for g in groups]
    ar, a_b, a_k, bk = [], [], [], []
    for g in groups:
        g_inv = jnp.exp(-cl[g])
        g_end = jnp.exp(tot[g] - cl[g])
        bv = b_ref[:, sl[g]]
        key = key_ref[:, sl[g]]
        at = -kk_ref[:, sl[g]] * jnp.exp(cl[g] - lw[g])
        rt = r_ref[:, sl[g]] * jnp.exp(cl[g])
        ar.append(jnp.concatenate([at, rt], axis=0).astype(bf16))
        a_b.append(lax.dot_general(ar[g], _bd_expand(bv * g_inv, lane_head), (((1,), (1,)), ((), ())),
                                   preferred_element_type=f32))
        a_k.append(lax.dot_general(ar[g], _bd_expand(key * g_inv, lane_head), (((1,), (1,)), ((), ())),
                                   preferred_element_type=f32))
        bk.append(jnp.concatenate([bv * g_end, key * g_end], axis=0).astype(bf16))
    n_ab = [jnp.where(mask_s, a_b[g][:c], 0.0) for g in groups]
    a_rb = [jnp.where(mask_i, a_b[g][c:], 0.0) for g in groups]
    a_kk = [jnp.concatenate([jnp.where(mask_s, a_k[g][:c], 0.0), jnp.where(mask_i, a_k[g][c:], 0.0)],
                            axis=0) for g in groups]
    x = [eye4 + n_ab[g] for g in groups]
    m = [mmbd(n_ab[g], n_ab[g]) for g in groups]
    lvl = 2
    while lvl < c:
        if lvl * 2 < c:
            xm = [mmbd(jnp.concatenate([x[g], m[g]], axis=0), m[g]) for g in groups]
            x = [x[g] + xm[g][:c] for g in groups]
            m = [xm[g][c:] for g in groups]
        else:
            x = [x[g] + mmbd(x[g], m[g]) for g in groups]
        lvl *= 2
    s = [s_ref[g] for g in groups]
    sa = [_mm_nt(ar[g], s[g]) for g in groups]
    av = [mmbd(a_kk[g], v[g]) for g in groups]
    u = [mmbd(x[g], sa[g][:c] + av[g][:c]) for g in groups]
    y = [sa[g][c:] + av[g][c:] + mmbd(a_rb[g], u[g]) for g in groups]
    upd = [_mm(jnp.concatenate([u[g], v[g]], axis=0).T, bk[g]) for g in groups]
    for g in groups:
        y_ref[:, sl[g]] = y[g]
        s_ref[g] = s[g] * jnp.exp(tot[g]) + jnp.where(bd_mask, upd[g], 0.0)

    @pl.when(i == nc - 1)
    def _():
        sfin_ref[...] = s_ref[...]


def _rwkv_scan(r, v, kk, lw, key, bvec, s0):
    bsz, l, _ = r.shape
    c = RW_CHUNK
    nc = l // c

    def cidx(d, i):
        return jnp.where(d == 0, i, nc - 1 - i)

    tok = pl.BlockSpec((None, c, D), lambda d, b, i: (b, cidx(d, i), 0))
    tok2 = pl.BlockSpec((None, None, c, D), lambda d, b, i: (d, b, cidx(d, i), 0))
    st = pl.BlockSpec((None, None, RW_NG, RW_GW, RW_GW), lambda d, b, i: (d, b, 0, 0, 0))
    return pl.pallas_call(
        functools.partial(_rwkv_scan_kernel, nc=nc),
        grid=(2, bsz, nc),
        in_specs=[tok, tok, tok, tok2, tok2, tok2, st],
        out_specs=[tok2, st],
        out_shape=[jax.ShapeDtypeStruct((2, bsz, l, D), f32),
                   jax.ShapeDtypeStruct((2, bsz, RW_NG, RW_GW, RW_GW), f32)],
        scratch_shapes=[pltpu.VMEM((RW_NG, RW_GW, RW_GW), f32)],
        compiler_params=_cparams(3),
        name="rwkv_scan",
    )(r, v, kk, lw, key, bvec, s0)


def _rwkv_out_kernel(y_ref, bonus_ref, g_ref, lnw_ref, lnb_ref, ones_ref, o_ref):
    ones_bd = ones_ref[...]
    for g in range(RW_NG):
        cs = slice(g * RW_GW, (g + 1) * RW_GW)
        y = y_ref[0, :, cs] + y_ref[1, :, cs]
        mean = _mm_x3(y, ones_bd) * (1.0 / HEAD)
        yc = y - mean
        var = _mm_x3(yc * yc, ones_bd) * (1.0 / HEAD)
        yn = yc * lax.rsqrt(var + GN_EPS) * lnw_ref[:, cs] + lnb_ref[:, cs]
        o_ref[:, cs] = ((yn + bonus_ref[:, cs]) * g_ref[:, cs]).astype(o_ref.dtype)


def _rwkv_out(y2, bonus, g, ln_w, ln_b, ones_bd):
    _, bsz, l, _ = y2.shape
    tl = min(256, l)
    tok = pl.BlockSpec((None, tl, D), lambda b, i: (b, i, 0))
    vec = pl.BlockSpec((1, D), lambda b, i: (0, 0))
    return pl.pallas_call(
        _rwkv_out_kernel,
        grid=(bsz, l // tl),
        in_specs=[pl.BlockSpec((2, None, tl, D), lambda b, i: (0, b, i, 0)), tok, tok, vec, vec,
                  pl.BlockSpec((RW_GW, RW_GW), lambda b, i: (0, 0))],
        out_specs=tok,
        out_shape=jax.ShapeDtypeStruct((bsz, l, D), bf16),
        compiler_params=_cparams(2),
        name="rwkv_out",
    )(y2, bonus, g, ln_w, ln_b, ones_bd)


def _ssd_kernel(p_ref, prev_ref, next_ref, cw_ref, cb_ref, dtb_ref, a_ref, dsk_ref, e_ref, eq_ref,
                h0_ref, y_ref, hfin_ref, pe_ref, xbc_ref, cole_ref, cumt_ref, dtt_ref, h_ref, *, nc):
    q = SSM_Q
    d = pl.program_id(0)
    i = pl.program_id(2)
    fwd = d == 0
    t = jnp.where(fwd, i, nc - 1 - i)

    @pl.when(i == 0)
    def _():
        h_ref[...] = h0_ref[...]

    zero = jnp.zeros((HALO, D_XBC), f32)
    pe_ref[0:HALO, :] = jnp.where(t > 0, prev_ref[:, D:D + D_XBC], zero)
    pe_ref[HALO:HALO + q, :] = p_ref[:, D:D + D_XBC]
    pe_ref[HALO + q:2 * HALO + q, :] = jnp.where(t < nc - 1, next_ref[:, D:D + D_XBC], zero)
    for n in range(D_XBC // 128):
        cs = slice(n * 128, (n + 1) * 128)
        xc = cb_ref[:, cs]
        for tap in range(CONV_W):
            r0 = HALO - CONV_LEFT + tap
            xc = xc + pe_ref[r0:r0 + q, cs] * cw_ref[tap:tap + 1, cs]
        xbc_ref[:, cs] = _silu(xc)

    row = _iota((q, q), 0)
    col = _iota((q, q), 1)
    sgn = jnp.where(fwd, 1, -1)
    low = (col - row) * sgn <= 0
    tri = jnp.where(low, 1.0, 0.0).astype(bf16)
    tri_t = jnp.where((row - col) * sgn <= 0, 1.0, 0.0).astype(bf16)

    dt = _softplus(p_ref[:, SS_OFF_DT:SS_OFF_DT + 128] + dtb_ref[...])
    dta = dt * a_ref[...]
    cum = _mm_e3(tri, dta)
    tot = jnp.sum(dta, axis=0, keepdims=True)
    cumt_ref[...] = _mm_x3(dta.T, tri_t)
    dtt_ref[...] = dt.T
    e_d = e_ref[...]
    dec_e = _mm_x3(jnp.exp(cum), e_d)
    toend_e = _mm_x3(jnp.exp(tot - cum) * dt, e_d)
    tot_e = _mm_x3(jnp.broadcast_to(jnp.exp(tot), (8, 128)), e_d)[0:1]
    cole_ref[...] = _mm_x3(cum, eq_ref[...])
    lane = _iota((q, 128), 1)

    for g in range(SSM_G):
        bg = xbc_ref[:, D + g * SSM_N:D + (g + 1) * SSM_N]
        cg = xbc_ref[:, D + SSM_G * SSM_N + g * SSM_N:D + SSM_G * SSM_N + (g + 1) * SSM_N]
        cb = _mm_nt(cg, bg)
        gs = slice(g * 512, (g + 1) * 512)
        hg = h_ref[g]
        y_off = _mm(cg, hg) * dec_e[:, gs]
        for pr in range(4):
            ls = []
            for hh in range(2):
                h = g * 8 + pr * 2 + hh
                rowv = cumt_ref[pl.ds(d * 32 + h, 1), :]
                dtr = dtt_ref[pl.ds(d * 32 + h, 1), :]
                seg = cole_ref[:, h * q:(h + 1) * q] - rowv
                ls.append(jnp.where(low, jnp.exp(jnp.where(low, seg, 0.0)), 0.0) * cb * dtr)
            lp = jnp.concatenate(ls, axis=1)
            ps = slice(g * 512 + pr * 128, g * 512 + (pr + 1) * 128)
            xp = xbc_ref[:, ps]
            bd2 = jnp.concatenate([jnp.where(lane < SSM_P, xp, 0.0), jnp.where(lane >= SSM_P, xp, 0.0)],
                                  axis=0)
            yd = _mm(lp, bd2)
            y_ref[:, ps] = yd + y_off[:, pr * 128:(pr + 1) * 128] + dsk_ref[:, ps] * xp
        xs = xbc_ref[:, gs] * toend_e[:, gs]
        states = _mm(bg.T, xs)
        h_ref[g] = hg * tot_e[:, gs] + states

    @pl.when(i == nc - 1)
    def _():
        hfin_ref[...] = h_ref[...]


def _ssd(p, prm, h0):
    bsz, l, _ = p.shape
    q = SSM_Q
    nc = l // q
    nh = l // HALO
    tpb = q // HALO

    def cidx(d, i):
        return jnp.where(d == 0, i, nc - 1 - i)

    full = lambda shape: pl.BlockSpec(shape, lambda d, b, i: (0,) * len(shape))
    st = pl.BlockSpec((None, None, SSM_G, SSM_N, 8 * SSM_P), lambda d, b, i: (d, b, 0, 0, 0))
    return pl.pallas_call(
        functools.partial(_ssd_kernel, nc=nc),
        grid=(2, bsz, nc),
        in_specs=[pl.BlockSpec((None, q, N_SSM_PAD), lambda d, b, i: (b, cidx(d, i), 0)),
                  pl.BlockSpec((None, HALO, N_SSM_PAD),
                               lambda d, b, i: (b, jnp.maximum(cidx(d, i) * tpb - 1, 0), 0)),
                  pl.BlockSpec((None, HALO, N_SSM_PAD),
                               lambda d, b, i: (b, jnp.minimum((cidx(d, i) + 1) * tpb, nh - 1), 0)),
                  full((CONV_W, D_XBC)), full((1, D_XBC)), full((1, 128)), full((1, 128)),
                  pl.BlockSpec((None, 1, D), lambda d, b, i: (d, 0, 0)),
                  pl.BlockSpec((None, 128, D), lambda d, b, i: (d, 0, 0)),
                  pl.BlockSpec((None, 128, SSM_HEADS * q), lambda d, b, i: (d, 0, 0)),
                  st],
        out_specs=[pl.BlockSpec((None, None, q, D), lambda d, b, i: (d, b, cidx(d, i), 0)), st],
        out_shape=[jax.ShapeDtypeStruct((2, bsz, l, D), f32),
                   jax.ShapeDtypeStruct((2, bsz, SSM_G, SSM_N, 8 * SSM_P), f32)],
        scratch_shapes=[pltpu.VMEM((q + 2 * HALO, D_XBC), f32), pltpu.VMEM((q, D_XBC), f32),
                        pltpu.VMEM((q, SSM_HEADS * q), f32), pltpu.VMEM((128, q), f32),
                        pltpu.VMEM((128, q), f32), pltpu.VMEM((SSM_G, SSM_N, 8 * SSM_P), f32)],
        compiler_params=_cparams(3),
        name="ssd",
    )(p, p, p, prm["cw"], prm["cb"], prm["dtb"], prm["a"], prm["dsk"], prm["e"], prm["eq"], h0)


def _ssd_out_kernel(y_ref, z_ref, nw_ref, o_ref):
    gw = D // SSM_G
    for g in range(SSM_G):
        cs = slice(g * gw, (g + 1) * gw)
        yg = (y_ref[0, :, cs] + y_ref[1, :, cs]) * _silu(z_ref[:, cs])
        ms = jnp.mean(yg * yg, axis=-1, keepdims=True)
        o_ref[:, cs] = (yg * lax.rsqrt(ms + EPS) * nw_ref[:, cs]).astype(o_ref.dtype)


def _ssd_out(y2, p, norm_w):
    _, bsz, l, _ = y2.shape
    tl = min(256, l)
    tok = pl.BlockSpec((None, tl, D), lambda b, i: (b, i, 0))
    return pl.pallas_call(
        _ssd_out_kernel,
        grid=(bsz, l // tl),
        in_specs=[pl.BlockSpec((2, None, tl, D), lambda b, i: (0, b, i, 0)), tok,
                  pl.BlockSpec((1, D), lambda b, i: (0, 0))],
        out_specs=tok,
        out_shape=jax.ShapeDtypeStruct((bsz, l, D), bf16),
        compiler_params=_cparams(2),
        name="ssd_out",
    )(y2, p, norm_w)


def _merge_kernel(ya_ref, yb_ref, yc_ref, sa_ref, sb_ref, sc_ref, wa_ref, wb_ref, wc_ref, o_ref):
    acc = None
    for y_ref, s_ref, w_ref in ((ya_ref, sa_ref, wa_ref), (yb_ref, sb_ref, wb_ref), (yc_ref, sc_ref, wc_ref)):
        t = s_ref[...].astype(f32) * jnp.dot(y_ref[...], w_ref[...], preferred_element_type=f32)
        acc = t if acc is None else acc + t
    o_ref[...] = acc.astype(o_ref.dtype)


def _merge(ys, sig, ws):
    m = ys[0].shape[0]
    tm = min(512, m)
    tn = 512
    nj = D // tn
    ysp = pl.BlockSpec((tm, D), lambda i, j: (i, 0))
    wsp = pl.BlockSpec((D, tn), lambda i, j: (0, j))
    ssp = [pl.BlockSpec((tm, tn), functools.partial(lambda i, j, k: (i, k * nj + j), k=k)) for k in range(3)]
    return pl.pallas_call(
        _merge_kernel,
        grid=(m // tm, nj),
        in_specs=[ysp, ysp, ysp] + ssp + [wsp, wsp, wsp],
        out_specs=pl.BlockSpec((tm, tn), lambda i, j: (i, j)),
        out_shape=jax.ShapeDtypeStruct((m, D), bf16),
        compiler_params=_cparams(2),
        name="merge",
    )(ys[0], ys[1], ys[2], sig, sig, sig, ws[0], ws[1], ws[2])


def _wo_kernel(m_ref, w_ref, h_ref, g_ref, o_ref):
    o_ref[...] = h_ref[...] + g_ref[...] * jnp.dot(m_ref[...], w_ref[...], preferred_element_type=f32)


def _wo_residual(m, w_o, h, mod4, row_of_batch, k_gate, *, transposed):
    bsz, l, _ = h.shape
    if transposed:
        rows = l // GRID_W
        hv = h.reshape(bsz, rows, GRID_W * D)
        tl, nt = rows, GRID_W
        hspec = pl.BlockSpec((None, tl, D), lambda b, i: (b, 0, i))
        oshape = jax.ShapeDtypeStruct((bsz, rows, GRID_W * D), f32)
    else:
        tl = min(512, l)
        nt = l // tl
        hv = h
        hspec = pl.BlockSpec((None, tl, D), lambda b, i: (b, i, 0))
        oshape = jax.ShapeDtypeStruct((bsz, l, D), f32)
    out = pl.pallas_call(
        _wo_kernel,
        grid=(bsz, nt),
        in_specs=[pl.BlockSpec((None, tl, D), lambda b, i: (b, i, 0)),
                  pl.BlockSpec((D, D), lambda b, i: (0, 0)),
                  hspec,
                  pl.BlockSpec((None, None, 1, D), lambda b, i: (row_of_batch(b), k_gate, 0, 0))],
        out_specs=hspec,
        out_shape=oshape,
        compiler_params=_cparams(2),
        name="wo_residual",
    )(m, w_o, hv, mod4)
    return out.reshape(bsz, l, D)


def _router_kernel(x_ref, g_ref, sc_ref, sh_ref, r_ref, u_ref, gates_ref):
    x = x_ref[...]
    ms = jnp.mean(x * x, axis=-1, keepdims=True)
    u = x * lax.rsqrt(ms + EPS) * g_ref[...] * (1.0 + sc_ref[...]) + sh_ref[...]
    u_ref[...] = u.astype(u_ref.dtype)
    u0, u1, u2 = _split3(u)
    r0, r1, r2 = _split3(r_ref[...])
    dot = lambda a, b: jnp.dot(a, b, preferred_element_type=f32)
    logits = (dot(u0, r0) + (dot(u0, r1) + dot(u1, r0))
              + (dot(u1, r1) + dot(u0, r2) + dot(u2, r0)))
    lane = _iota(logits.shape, 1)
    neg = jnp.float32(-jnp.inf)
    lg = jnp.where(lane < N_EXPERTS, logits, neg)
    m1 = jnp.max(lg, axis=-1, keepdims=True)
    i1 = jnp.min(jnp.where(lg == m1, lane, 128), axis=-1, keepdims=True)
    lg2 = jnp.where(lane == i1, neg, lg)
    m2 = jnp.max(lg2, axis=-1, keepdims=True)
    i2 = jnp.min(jnp.where(lg2 == m2, lane, 128), axis=-1, keepdims=True)
    e2 = jnp.exp(m2 - m1)
    den = 1.0 + e2
    gates_ref[...] = jnp.where(lane == i1, 1.0 / den, 0.0) + jnp.where(lane == i2, e2 / den, 0.0)


def _norm_router(h, gain, mod4, row_of_batch, k_sc, k_sh, router_pad):
    bsz, l, _ = h.shape
    tl = min(512, l)
    tok = pl.BlockSpec((None, tl, D), lambda b, i: (b, i, 0))
    return pl.pallas_call(
        _router_kernel,
        grid=(bsz, l // tl),
        in_specs=[tok,
                  pl.BlockSpec((1, D), lambda b, i: (0, 0)),
                  pl.BlockSpec((None, None, 1, D), lambda b, i: (row_of_batch(b), k_sc, 0, 0)),
                  pl.BlockSpec((None, None, 1, D), lambda b, i: (row_of_batch(b), k_sh, 0, 0)),
                  pl.BlockSpec((D, 128), lambda b, i: (0, 0))],
        out_specs=[tok, pl.BlockSpec((None, tl, 128), lambda b, i: (b, i, 0))],
        out_shape=[jax.ShapeDtypeStruct((bsz, l, D), bf16), jax.ShapeDtypeStruct((bsz, l, 128), f32)],
        compiler_params=_cparams(2),
        name="norm_router",
    )(h, gain.reshape(1, D), mod4, mod4, router_pad)


def _up_kernel(*refs, gated):
    if gated:
        u_ref, w1_ref, w3_ref, gt_ref, o_ref = refs
    else:
        u_ref, w1_ref, w3_ref, o_ref = refs
    u = u_ref[...]
    a = jnp.dot(u, w1_ref[...], preferred_element_type=f32)
    b = jnp.dot(u, w3_ref[...], preferred_element_type=f32)
    act = _silu(a) * b
    if gated:
        e = pl.program_id(1)
        gt = gt_ref[...]
        lane = _iota(gt.shape, 1)
        act = act * jnp.sum(jnp.where(lane == e, gt, 0.0), axis=-1, keepdims=True)
    o_ref[...] = act.astype(o_ref.dtype)


def _swiglu_up(u, w1, w3, gates=None):
    m = u.shape[0]
    ne, _, ff = w1.shape
    tm = min(1024, m)
    tn = 512
    nj = ff // tn
    gated = gates is not None
    wspec = pl.BlockSpec((None, D, tn), lambda i, e, j: (e, 0, j))
    in_specs = [pl.BlockSpec((tm, D), lambda i, e, j: (i, 0)), wspec, wspec]
    args = [u, w1, w3]
    if gated:
        in_specs.append(pl.BlockSpec((tm, 128), lambda i, e, j: (i, 0)))
        args.append(gates)
    return pl.pallas_call(
        functools.partial(_up_kernel, gated=gated),
        grid=(m // tm, ne, nj),
        in_specs=in_specs,
        out_specs=pl.BlockSpec((tm, tn), lambda i, e, j: (i, e * nj + j)),
        out_shape=jax.ShapeDtypeStruct((m, ne * ff), bf16),
        compiler_params=_cparams(3),
        name="swiglu_up",
    )(*args)


def _down_kernel(a_ref, w_ref, h_ref, g_ref, o_ref, acc_ref, *, nk):
    k = pl.program_id(2)

    @pl.when(k == 0)
    def _():
        acc_ref[...] = jnp.zeros_like(acc_ref)

    acc_ref[...] += jnp.dot(a_ref[...], w_ref[...], preferred_element_type=f32)

    @pl.when(k == nk - 1)
    def _():
        o_ref[...] = h_ref[...] + g_ref[...] * acc_ref[...]


def _down_residual(act, w2, h, mod4, row_of_batch, k_gate):
    bsz, l, kk = act.shape
    tl = min(512, l)
    tk = 1024
    nk = kk // tk
    return pl.pallas_call(
        functools.partial(_down_kernel, nk=nk),
        grid=(bsz, l // tl, nk),
        in_specs=[pl.BlockSpec((None, tl, tk), lambda b, i, k: (b, i, k)),
                  pl.BlockSpec((tk, D), lambda b, i, k: (k, 0)),
                  pl.BlockSpec((None, tl, D), lambda b, i, k: (b, i, 0)),
                  pl.BlockSpec((None, None, 1, D), lambda b, i, k: (row_of_batch(b), k_gate, 0, 0))],
        out_specs=pl.BlockSpec((None, tl, D), lambda b, i, k: (b, i, 0)),
        out_shape=jax.ShapeDtypeStruct((bsz, l, D), f32),
        scratch_shapes=[pltpu.VMEM((tl, D), f32)],
        compiler_params=_cparams(3),
        name="down_residual",
    )(act, w2, h, mod4)


def _block_ones(n, bs):
    idx = jnp.arange(n) // bs
    return (idx[:, None] == idx[None, :]).astype(bf16)


def _pad_cols(w, n):
    return jnp.pad(w, ((0, 0), (0, n - w.shape[1])))


def _rwkv_params(li, mu, w0, w_up, a0, a_up, g_up, k_k, k_a, r_k, ln_w, ln_b):
    wup = jnp.zeros((2, 256, D), f32)
    aup = jnp.zeros((2, 256, D), f32)
    for d in range(2):
        wup = wup.at[d, d * W_LORA:(d + 1) * W_LORA].set(w_up[li, d])
        o = (RW_OFF_G - 2 * A_LORA) - RW_OFF_A + d * A_LORA
        aup = aup.at[d, o:o + A_LORA].set(a_up[li, d])
    return {
        "mu": jnp.pad(mu[li], (0, N_RWKV_PAD - N_RWKV)).reshape(1, N_RWKV_PAD),
        "w0": w0[li], "wup": wup.astype(bf16), "a0": a0[li], "aup": aup.astype(bf16),
        "gup": g_up[li].astype(bf16), "k_k": k_k[li].reshape(1, D), "k_a": k_a[li].reshape(1, D),
        "r_k": r_k[li].reshape(1, D), "ln_w": ln_w[li].reshape(1, D), "ln_b": ln_b[li].reshape(1, D),
        "ones_bd": _block_ones(RW_GW, HEAD),
    }


def _ssd_params(li, conv_w, conv_b, a_log, dt_bias, d_skip):
    a = -jnp.exp(a_log[li].astype(f32))
    hid = jnp.arange(D) // SSM_P
    hq = jnp.arange(SSM_HEADS * SSM_Q) // SSM_Q
    sel = jnp.arange(128)[:, None]
    e = jnp.stack([(sel == hid[None, :] + d * SSM_HEADS) for d in range(2)]).astype(bf16)
    eq = jnp.stack([(sel == hq[None, :] + d * SSM_HEADS) for d in range(2)]).astype(bf16)
    return {
        "cw": conv_w[li], "cb": conv_b[li].reshape(1, D_XBC),
        "dtb": jnp.pad(dt_bias[li].reshape(1, 2 * SSM_HEADS), ((0, 0), (0, 64))),
        "a": jnp.pad(a.reshape(1, 2 * SSM_HEADS), ((0, 0), (0, 64))),
        "dsk": jnp.stack([jnp.repeat(d_skip[li], SSM_P), jnp.zeros((D,), f32)]).reshape(2, 1, D),
        "e": e, "eq": eq,
    }


def kernel(x, c, ctx, c_ctx, ada_w, ada_b, norm_mix, norm_ffn, norm_final, w_in, lru_conv_w, lru_conv_b, lru_gate_w, lru_gate_b, lru_lambda, rwkv_mu, rwkv_w0, rwkv_w_up, rwkv_a0, rwkv_a_up, rwkv_g_up, rwkv_k_k, rwkv_k_a, rwkv_r_k, rwkv_ln_w, rwkv_ln_b, ssm_conv_w, ssm_conv_b, ssm_a_log, ssm_dt_bias, ssm_d, ssm_norm_w, w_out_lru, w_out_rwkv, w_out_ssm, w_o, ffn_w1, ffn_w3, ffn_w2, moe_router, moe_w1, moe_w3, moe_w2):
    bsz, l, _ = x.shape
    depth = ada_w.shape[0]
    off_lru = 3 * D
    off_rwkv = off_lru + 2 * D
    off_ssm = off_rwkv + N_RWKV
    cvec = jnp.zeros((8, D), f32).at[:bsz].set(c).at[bsz].set(c_ctx)
    lat_row = lambda b: b
    ctx_row = lambda b: bsz

    h_lat, h_ctx = x, ctx
    for li in range(depth):
        last = li == depth - 1
        odd = li % 2 == 1
        mod4 = _ada(cvec, ada_w[li], ada_b[li]).reshape(8, 6, 1, D)

        wl = w_in[li]
        w_gate = wl[:, :off_lru].astype(bf16)
        w_lru = wl[:, off_lru:off_rwkv].astype(bf16)
        w_rw = _pad_cols(wl[:, off_rwkv:off_ssm], N_RWKV_PAD).astype(bf16)
        w_ss = _pad_cols(wl[:, off_ssm:], N_SSM_PAD).astype(bf16)
        lru_gw = [jnp.concatenate([lru_gate_w[li, d, 0], lru_gate_w[li, d, 1]], axis=-1).astype(bf16)
                  for d in range(2)]
        rp = _rwkv_params(li, rwkv_mu, rwkv_w0, rwkv_w_up, rwkv_a0, rwkv_a_up, rwkv_g_up, rwkv_k_k,
                          rwkv_k_a, rwkv_r_k, rwkv_ln_w, rwkv_ln_b)
        sp = _ssd_params(li, ssm_conv_w, ssm_conv_b, ssm_a_log, ssm_dt_bias, ssm_d)
        w_outs = [w_out_lru[li].astype(bf16), w_out_rwkv[li].astype(bf16), w_out_ssm[li].astype(bf16)]
        w_o_b = w_o[li].astype(bf16)

        def token_mix(u, states, need_out):
            lx = u.shape[1]
            um = u.reshape(bsz * lx, D)
            p_lru = _proj(um, w_lru, tn=1024).reshape(bsz, lx, 2 * D)
            p_rw = _proj(um, w_rw, tn=768).reshape(bsz, lx, N_RWKV_PAD)
            p_ss = _proj(um, w_ss, tn=768).reshape(bsz, lx, N_SSM_PAD)
            lru_s, rw_s, ss_s = states
            cw, cb = lru_conv_w[li], lru_conv_b[li].reshape(1, D)
            hb, hl_b = _lru_pass(p_lru, cw, cb, lru_gw[1], lru_gate_b[li, 1], lru_lambda[li, 1].reshape(1, D),
                                 lru_s[1], None, reverse=True)
            ya, hl_f = _lru_pass(p_lru, cw, cb, lru_gw[0], lru_gate_b[li, 0], lru_lambda[li, 0].reshape(1, D),
                                 lru_s[0], hb, reverse=False)
            r, v, kk, g, bonus, lw, key, bvec = _rwkv_feat(p_rw, rp)
            y2, rw_fin = _rwkv_scan(r, v, kk, lw, key, bvec, rw_s)
            ys2, ss_fin = _ssd(p_ss, sp, ss_s)
            new_states = ((hl_f, hl_b), rw_fin, ss_fin)
            if not need_out:
                return None, new_states
            yb = _rwkv_out(y2, bonus, g, rp["ln_w"], rp["ln_b"], rp["ones_bd"])
            yc = _ssd_out(ys2, p_ss, ssm_norm_w[li].reshape(1, D))
            sig = _proj(um, w_gate, tn=1024, act="sigmoid", out_dtype=bf16)
            m = _merge([ya.reshape(-1, D), yb.reshape(-1, D), yc.reshape(-1, D)], sig, w_outs)
            return m.reshape(bsz, lx, D), new_states

        zero_states = ((jnp.zeros((bsz, 1, D), f32), jnp.zeros((bsz, 1, D), f32)),
                       jnp.zeros((2, bsz, RW_NG, RW_GW, RW_GW), f32),
                       jnp.zeros((2, bsz, SSM_G, SSM_N, 8 * SSM_P), f32))

        u_ctx = _norm(h_ctx, norm_mix[li], mod4, ctx_row, 1, 0)
        u_lat = _norm(h_lat, norm_mix[li], mod4, lat_row, 1, 0, transposed=odd)
        m_ctx, ctx_states = token_mix(u_ctx, zero_states, not last)
        m_lat, _ = token_mix(u_lat, ctx_states, True)
        h_lat = _wo_residual(m_lat, w_o_b, h_lat, mod4, lat_row, 2, transposed=odd)
        if not last:
            h_ctx = _wo_residual(m_ctx, w_o_b, h_ctx, mod4, ctx_row, 2, transposed=False)

        j = li // 2
        streams = [(h_lat, lat_row)] + ([] if last else [(h_ctx, ctx_row)])
        outs = []
        for h, row_fn in streams:
            lx = h.shape[1]
            if not odd:
                v = _norm(h, norm_ffn[li], mod4, row_fn, 4, 3)
                act = _swiglu_up(v.reshape(bsz * lx, D), ffn_w1[j].astype(bf16)[None],
                                 ffn_w3[j].astype(bf16)[None])
                w2 = ffn_w2[j].astype(bf16)
            else:
                router_pad = _pad_cols(moe_router[j], 128)
                v, gates = _norm_router(h, norm_ffn[li], mod4, row_fn, 4, 3, router_pad)
                act = _swiglu_up(v.reshape(bsz * lx, D), moe_w1[j].astype(bf16), moe_w3[j].astype(bf16),
                                 gates.reshape(bsz * lx, 128))
                w2 = moe_w2[j].astype(bf16).reshape(N_EXPERTS * D_FF_EXPERT, D)
            outs.append(_down_residual(act.reshape(bsz, lx, -1), w2, h, mod4, row_fn, 5))
        h_lat = outs[0]
        if not last:
            h_ctx = outs[1]
    return _final_norm(h_lat, norm_final)
```

```python
import functools
import math

import jax
import jax.numpy as jnp
from jax import lax
from jax.experimental import pallas as pl
from jax.experimental.pallas import tpu as pltpu

f32 = jnp.float32
bf16 = jnp.bfloat16

D = 2048
GRID_W = 64
EPS = 1e-6
CONV_W = 4
CONV_LEFT = 2
HALO = 8

LRU_BLOCKS = 16
LRU_BS = D // LRU_BLOCKS
LRU_C = 8.0

HEADS = 32
HEAD = 64
W_LORA = 96
A_LORA = 96
G_LORA = 256
GN_EPS = 64e-5
N_RWKV = 3 * D + 2 * W_LORA + 2 * A_LORA + G_LORA
N_RWKV_PAD = 7168
RW_OFF_W = 3 * D
RW_OFF_A = 3 * D + 128
RW_OFF_G = 3 * D + 2 * W_LORA + 2 * A_LORA
RW_CHUNK = 64
RW_GW = 256
RW_NG = D // RW_GW

SSM_HEADS = 32
SSM_P = 64
SSM_N = 128
SSM_G = 4
SSM_Q = 128
D_XBC = D + 2 * SSM_G * SSM_N
N_SSM = D + D_XBC + 2 * SSM_HEADS
N_SSM_PAD = 5376
SS_OFF_DT = D + D_XBC

D_FF = 3 * D
N_EXPERTS = 8
D_FF_EXPERT = D_FF // 2

VMEM_LIMIT = 56 * 1024 * 1024


def _cparams(n_axes, vmem=VMEM_LIMIT):
    return pltpu.CompilerParams(dimension_semantics=("arbitrary",) * n_axes, vmem_limit_bytes=vmem)


def _mm(a, b):
    return jnp.dot(a.astype(bf16), b.astype(bf16), preferred_element_type=f32)


def _mm_nt(a, b):
    return lax.dot_general(a.astype(bf16), b.astype(bf16), (((1,), (1,)), ((), ())),
                           preferred_element_type=f32)


def _split3(x):
    x0 = x.astype(bf16)
    r = x - x0.astype(f32)
    x1 = r.astype(bf16)
    r = r - x1.astype(f32)
    return x0, x1, r.astype(bf16)


def _mm_x3(x, e):
    x0, x1, x2 = _split3(x)
    return (jnp.dot(x0, e, preferred_element_type=f32) + jnp.dot(x1, e, preferred_element_type=f32)
            + jnp.dot(x2, e, preferred_element_type=f32))


def _mm_e3(e, x):
    x0, x1, x2 = _split3(x)
    return (jnp.dot(e, x0, preferred_element_type=f32) + jnp.dot(e, x1, preferred_element_type=f32)
            + jnp.dot(e, x2, preferred_element_type=f32))


def _softplus(x):
    return jnp.maximum(x, 0.0) + jnp.log1p(jnp.exp(-jnp.abs(x)))


def _sigmoid(x):
    return jax.nn.sigmoid(x)


def _silu(x):
    return x * jax.nn.sigmoid(x)


def _iota(shape, dim):
    return lax.broadcasted_iota(jnp.int32, shape, dim)


def _ada_kernel(c_ref, w_ref, b_ref, o_ref):
    cv = c_ref[...]
    o_ref[...] = _mm(_silu(cv), w_ref[...]) + b_ref[...]


def _ada(cvec, w, b):
    n = w.shape[1]
    tn = 1536
    return pl.pallas_call(
        _ada_kernel,
        grid=(n // tn,),
        in_specs=[pl.BlockSpec((8, D), lambda j: (0, 0)),
                  pl.BlockSpec((D, tn), lambda j: (0, j)),
                  pl.BlockSpec((1, tn), lambda j: (0, j))],
        out_specs=pl.BlockSpec((8, tn), lambda j: (0, j)),
        out_shape=jax.ShapeDtypeStruct((8, n), f32),
        compiler_params=_cparams(1),
        name="ada",
    )(cvec, w, b.reshape(1, n))


def _norm_kernel(x_ref, g_ref, sc_ref, sh_ref, o_ref):
    x = x_ref[...]
    ms = jnp.mean(x * x, axis=-1, keepdims=True)
    xn = x * lax.rsqrt(ms + EPS) * g_ref[...]
    o_ref[...] = (xn * (1.0 + sc_ref[...]) + sh_ref[...]).astype(o_ref.dtype)


def _norm(h, gain, mod4, row_of_batch, k_sc, k_sh, *, transposed=False, out_dtype=bf16):
    bsz, l, _ = h.shape
    if transposed:
        rows = l // GRID_W
        hin = h.reshape(bsz, rows, GRID_W * D)
        tl = rows
        nt = GRID_W
        in_spec = pl.BlockSpec((None, tl, D), lambda b, i: (b, 0, i))
    else:
        tl = min(512, l)
        nt = l // tl
        hin = h
        in_spec = pl.BlockSpec((None, tl, D), lambda b, i: (b, i, 0))
    return pl.pallas_call(
        _norm_kernel,
        grid=(bsz, nt),
        in_specs=[in_spec,
                  pl.BlockSpec((1, D), lambda b, i: (0, 0)),
                  pl.BlockSpec((None, None, 1, D), lambda b, i: (row_of_batch(b), k_sc, 0, 0)),
                  pl.BlockSpec((None, None, 1, D), lambda b, i: (row_of_batch(b), k_sh, 0, 0))],
        out_specs=pl.BlockSpec((None, tl, D), lambda b, i: (b, i, 0)),
        out_shape=jax.ShapeDtypeStruct((bsz, l, D), out_dtype),
        compiler_params=_cparams(2),
        name="norm",
    )(hin, gain.reshape(1, D), mod4, mod4)


def _plain_norm_kernel(x_ref, g_ref, o_ref):
    x = x_ref[...]
    ms = jnp.mean(x * x, axis=-1, keepdims=True)
    o_ref[...] = (x * lax.rsqrt(ms + EPS) * g_ref[...]).astype(o_ref.dtype)


def _final_norm(h, gain):
    bsz, l, _ = h.shape
    tl = min(512, l)
    return pl.pallas_call(
        _plain_norm_kernel,
        grid=(bsz, l // tl),
        in_specs=[pl.BlockSpec((None, tl, D), lambda b, i: (b, i, 0)),
                  pl.BlockSpec((1, D), lambda b, i: (0, 0))],
        out_specs=pl.BlockSpec((None, tl, D), lambda b, i: (b, i, 0)),
        out_shape=jax.ShapeDtypeStruct((bsz, l, D), f32),
        compiler_params=_cparams(2),
        name="final_norm",
    )(h, gain.reshape(1, D))


def _proj_kernel(u_ref, w_ref, o_ref, *, act):
    acc = jnp.dot(u_ref[...], w_ref[...].astype(bf16), preferred_element_type=f32)
    if act == "sigmoid":
        acc = _sigmoid(acc)
    o_ref[...] = acc.astype(o_ref.dtype)


def _proj(u, w, *, tn, col0=0, n=None, act=None, out_dtype=f32):
    m, k = u.shape
    n = w.shape[1] if n is None else n
    assert col0 % tn == 0 and n % tn == 0 and col0 + n <= w.shape[1]
    off = col0 // tn
    tm = min(1024, m)
    return pl.pallas_call(
        functools.partial(_proj_kernel, act=act),
        grid=(m // tm, n // tn),
        in_specs=[pl.BlockSpec((tm, k), lambda i, j: (i, 0)),
                  pl.BlockSpec((k, tn), lambda i, j: (0, j + off))],
        out_specs=pl.BlockSpec((tm, tn), lambda i, j: (i, j)),
        out_shape=jax.ShapeDtypeStruct((m, n), out_dtype),
        compiler_params=_cparams(2),
        name="proj",
    )(u, w)


def _lru_kernel(*refs, reverse, final, tl, nt):
    if final:
        (x_ref, prev_ref, next_ref, gate_ref, hb_ref, cw_ref, cb_ref, gw_ref, gb_ref, lam_ref, h0_ref,
         out_ref, hlast_ref, xe_ref, a_ref, bx_ref, hs_ref, h_ref) = refs
    else:
        (x_ref, prev_ref, next_ref, cw_ref, cb_ref, gw_ref, gb_ref, lam_ref, h0_ref,
         out_ref, hlast_ref, xe_ref, a_ref, bx_ref, hs_ref, h_ref) = refs
    i = pl.program_id(1)
    t = (nt - 1 - i) if reverse else i

    @pl.when(i == 0)
    def _():
        h_ref[...] = h0_ref[...]

    zero = jnp.zeros((HALO, D), f32)
    xe_ref[0:HALO, :] = jnp.where(t > 0, prev_ref[...], zero)
    xe_ref[HALO:HALO + tl, :] = x_ref[...]
    xe_ref[HALO + tl:2 * HALO + tl, :] = jnp.where(t < nt - 1, next_ref[...], zero)

    for n in range(LRU_BLOCKS):
        cs = slice(n * LRU_BS, (n + 1) * LRU_BS)
        xc = cb_ref[:, cs]
        for tap in range(CONV_W):
            r0 = HALO - CONV_LEFT + tap
            xc = xc + xe_ref[r0:r0 + tl, cs] * cw_ref[tap:tap + 1, cs]
        g = _mm(xc, gw_ref[n])
        rec = _sigmoid(g[:, :LRU_BS] + gb_ref[0:1, cs])
        inp = _sigmoid(g[:, LRU_BS:] + gb_ref[1:2, cs])
        log_a = -LRU_C * rec * _softplus(-lam_ref[:, cs])
        a_ref[:, cs] = jnp.exp(log_a)
        th = jnp.tanh(log_a)
        bx_ref[:, cs] = jnp.sqrt(-2.0 * th / (1.0 - th)) * inp * xc

    def body(s, h):
        tt = (tl - 1 - s) if reverse else s
        h = a_ref[pl.ds(tt, 1), :] * h + bx_ref[pl.ds(tt, 1), :]
        hs_ref[pl.ds(tt, 1), :] = h
        return h

    h = lax.fori_loop(0, tl, body, h_ref[...], unroll=8)
    h_ref[...] = h

    @pl.when(i == nt - 1)
    def _():
        hlast_ref[...] = h

    if final:
        for n in range(LRU_BLOCKS):
            cs = slice(n * LRU_BS, (n + 1) * LRU_BS)
            y = (hs_ref[:, cs] + hb_ref[:, cs]) * jax.nn.gelu(gate_ref[:, cs])
            out_ref[:, cs] = y.astype(out_ref.dtype)
    else:
        out_ref[...] = hs_ref[...]


def _lru_pass(p, cw, cb, gw, gb, lam, h0, hb, *, reverse):
    bsz, l, _ = p.shape
    tl = min(256, l)
    nt = l // tl
    final = hb is not None
    nh = l // HALO
    tpb = tl // HALO

    def tmap(i):
        return (nt - 1 - i) if reverse else i

    main = pl.BlockSpec((None, tl, D), lambda b, i: (b, tmap(i), 0))
    in_specs = [main,
                pl.BlockSpec((None, HALO, D), lambda b, i: (b, jnp.maximum(tmap(i) * tpb - 1, 0), 0)),
                pl.BlockSpec((None, HALO, D), lambda b, i: (b, jnp.minimum((tmap(i) + 1) * tpb, nh - 1), 0))]
    args = [p, p, p]
    if final:
        in_specs += [pl.BlockSpec((None, tl, D), lambda b, i: (b, tmap(i), 1)), main]
        args += [p, hb]
    in_specs += [pl.BlockSpec((CONV_W, D), lambda b, i: (0, 0)),
                 pl.BlockSpec((1, D), lambda b, i: (0, 0)),
                 pl.BlockSpec((LRU_BLOCKS, LRU_BS, 2 * LRU_BS), lambda b, i: (0, 0, 0)),
                 pl.BlockSpec((2, D), lambda b, i: (0, 0)),
                 pl.BlockSpec((1, D), lambda b, i: (0, 0)),
                 pl.BlockSpec((None, 1, D), lambda b, i: (b, 0, 0))]
    args += [cw, cb, gw, gb, lam, h0]
    out, hlast = pl.pallas_call(
        functools.partial(_lru_kernel, reverse=reverse, final=final, tl=tl, nt=nt),
        grid=(bsz, nt),
        in_specs=in_specs,
        out_specs=[main, pl.BlockSpec((None, 1, D), lambda b, i: (b, 0, 0))],
        out_shape=[jax.ShapeDtypeStruct((bsz, l, D), bf16 if final else f32),
                   jax.ShapeDtypeStruct((bsz, 1, D), f32)],
        scratch_shapes=[pltpu.VMEM((tl + 2 * HALO, D), f32), pltpu.VMEM((tl, D), f32),
                        pltpu.VMEM((tl, D), f32), pltpu.VMEM((tl, D), f32), pltpu.VMEM((1, D), f32)],
        compiler_params=_cparams(2),
        name="lru_fwd" if final else "lru_bwd",
    )(*args)
    return out, hlast


def _rwkv_feat_kernel(p_ref, prev_ref, next_ref, mu_ref, w0_ref, wup_ref, a0_ref, aup_ref, gup_ref,
                      kk_ref, ka_ref, rk_ref, ones_ref,
                      r_out, v_out, kkv_out, g_out, bonus_out, lw_out, key_out, b_out,
                      pe_ref, *, tl, nt):
    i = pl.program_id(1)
    zero = jnp.zeros((HALO, N_RWKV_PAD), f32)
    pe_ref[0:HALO, :] = jnp.where(i > 0, prev_ref[...], zero)
    pe_ref[HALO:HALO + tl, :] = p_ref[...]
    pe_ref[HALO + tl:2 * HALO + tl, :] = jnp.where(i < nt - 1, next_ref[...], zero)

    def shifted(c0, width):
        cs = slice(c0, c0 + width)
        p = pe_ref[HALO:HALO + tl, cs]
        nb = 0.5 * (pe_ref[HALO - 1:HALO - 1 + tl, cs] + pe_ref[HALO + 1:HALO + 1 + tl, cs])
        return p + mu_ref[:, cs] * (nb - p)

    win_w = jnp.tanh(shifted(RW_OFF_W, 256)).astype(bf16)
    win_a = shifted(RW_OFF_A, 256).astype(bf16)
    win_g = _sigmoid(shifted(RW_OFF_G, 256)).astype(bf16)
    ones_bd = ones_ref[...]

    for g in range(RW_NG):
        cs = slice(g * RW_GW, (g + 1) * RW_GW)
        r = shifted(g * RW_GW, RW_GW)
        k = shifted(D + g * RW_GW, RW_GW)
        v = shifted(2 * D + g * RW_GW, RW_GW)
        kf = k * kk_ref[:, cs]
        kk = kf * lax.rsqrt(_mm_x3(kf * kf, ones_bd) + 1e-12)
        ksum = None
        for d in range(2):
            wv = -_softplus(-(w0_ref[d:d + 1, cs] + jnp.dot(win_w, wup_ref[d, :, cs],
                                                              preferred_element_type=f32))) - 0.5
            lw_out[d, :, cs] = -jnp.exp(wv)
            a = _sigmoid(a0_ref[d:d + 1, cs] + jnp.dot(win_a, aup_ref[d, :, cs],
                                                       preferred_element_type=f32))
            key = k * (1.0 + (a - 1.0) * ka_ref[:, cs])
            key_out[d, :, cs] = key
            b_out[d, :, cs] = kk * a
            ksum = key if ksum is None else ksum + key
        r_out[:, cs] = r
        v_out[:, cs] = v
        kkv_out[:, cs] = kk
        bonus_out[:, cs] = _mm_x3(r * ksum * rk_ref[:, cs], ones_bd) * v
        g_out[:, cs] = jnp.dot(win_g, gup_ref[:, cs], preferred_element_type=f32)


def _rwkv_feat(p, prm):
    bsz, l, _ = p.shape
    tl = min(128, l)
    nt = l // tl
    nh = l // HALO
    tpb = tl // HALO
    full2 = lambda shape: pl.BlockSpec(shape, lambda b, i: (0,) * len(shape))
    tok = pl.BlockSpec((None, tl, D), lambda b, i: (b, i, 0))
    tok2 = pl.BlockSpec((2, None, tl, D), lambda b, i: (0, b, i, 0))
    sd = jax.ShapeDtypeStruct((bsz, l, D), f32)
    sd2 = jax.ShapeDtypeStruct((2, bsz, l, D), f32)
    return pl.pallas_call(
        functools.partial(_rwkv_feat_kernel, tl=tl, nt=nt),
        grid=(bsz, nt),
        in_specs=[pl.BlockSpec((None, tl, N_RWKV_PAD), lambda b, i: (b, i, 0)),
                  pl.BlockSpec((None, HALO, N_RWKV_PAD), lambda b, i: (b, jnp.maximum(i * tpb - 1, 0), 0)),
                  pl.BlockSpec((None, HALO, N_RWKV_PAD),
                               lambda b, i: (b, jnp.minimum((i + 1) * tpb, nh - 1), 0)),
                  full2((1, N_RWKV_PAD)), full2((2, D)), full2((2, 256, D)), full2((2, D)),
                  full2((2, 256, D)), full2((256, D)), full2((1, D)), full2((1, D)), full2((1, D)),
                  full2((RW_GW, RW_GW))],
        out_specs=[tok, tok, tok, tok, tok, tok2, tok2, tok2],
        out_shape=[sd, sd, sd, sd, sd, sd2, sd2, sd2],
        scratch_shapes=[pltpu.VMEM((tl + 2 * HALO, N_RWKV_PAD), f32)],
        compiler_params=_cparams(2),
        name="rwkv_feat",
    )(p, p, p, prm["mu"], prm["w0"], prm["wup"], prm["a0"], prm["aup"], prm["gup"],
      prm["k_k"], prm["k_a"], prm["r_k"], prm["ones_bd"])


def _bd_expand(y, lane_head):
    yb = y.astype(f32)
    return jnp.concatenate([jnp.where(lane_head == h, yb, 0.0).astype(bf16) for h in range(4)], axis=0)


def _rwkv_scan_kernel(r_ref, v_ref, kk_ref, lw_ref, key_ref, b_ref, s0_ref, y_ref, sfin_ref,
                      s_ref, *, nc):
    c = RW_CHUNK
    d = pl.program_id(0)
    i = pl.program_id(2)
    fwd = d == 0

    @pl.when(i == 0)
    def _():
        s_ref[...] = s0_ref[...]

    row = _iota((c, c), 0)
    col = _iota((c, c), 1)
    sgn = jnp.where(fwd, 1, -1)
    tri = jnp.where((col - row) * sgn <= 0, 1.0, 0.0).astype(bf16)
    t4 = _iota((c, 4 * c), 0)
    j4 = _iota((c, 4 * c), 1) % c
    mask_s = (j4 - t4) * sgn < 0
    mask_i = (j4 - t4) * sgn <= 0
    eye4 = jnp.where(j4 == t4, 1.0, 0.0)
    lane_head = _iota((c, RW_GW), 1) // HEAD
    bd_mask = (_iota((RW_GW, RW_GW), 0) // HEAD) == (_iota((RW_GW, RW_GW), 1) // HEAD)

    def mmbd(x, y):
        return jnp.dot(x.astype(bf16), _bd_expand(y, lane_head), preferred_element_type=f32)

    groups = range(RW_NG)
    sl = [slice(g * RW_GW, (g + 1) * RW_GW) for g in groups]
    lw = [lw_ref[:, sl[g]] for g in groups]
    cl = [_mm_e3(tri, lw[g]) for g in groups]
    tot = [jnp.sum(lw[g], axis=0, keepdims=True) for g in groups]
    v = [v_ref[:, sl[g]] for g in groups]
    ar, a_b, a_k, bk = [], [], [], []
    for g in groups:
        g_inv = jnp.exp(-cl[g])
        g_end = jnp.exp(tot[g] - cl[g])
        bv = b_ref[:, sl[g]]
        key = key_ref[:, sl[g]]
        at = -kk_ref[:, sl[g]] * jnp.exp(cl[g] - lw[g])
        rt = r_ref[:, sl[g]] * jnp.exp(cl[g])
        ar.append(jnp.concatenate([at, rt], axis=0).astype(bf16))
        a_b.append(lax.dot_general(ar[g], _bd_expand(bv * g_inv, lane_head), (((1,), (1,)), ((), ())),
                                   preferred_element_type=f32))
        a_k.append(lax.dot_general(ar[g], _bd_expand(key * g_inv, lane_head), (((1,), (1,)), ((), ())),
                                   preferred_element_type=f32))
        bk.append(jnp.concatenate([bv * g_end, key * g_end], axis=0).astype(bf16))
    n_ab = [jnp.where(mask_s, a_b[g][:c], 0.0) for g in groups]
    a_rb = [jnp.where(mask_i, a_b[g][c:], 0.0) for g in groups]
    a_kk = [jnp.concatenate([jnp.where(mask_s, a_k[g][:c], 0.0), jnp.where(mask_i, a_k[g][c:], 0.0)],
                            axis=0) for g in groups]
    x = [eye4 + n_ab[g] for g in groups]
    m = [mmbd(n_ab[g], n_ab[g]) for g in groups]
    lvl = 2
    while lvl < c:
        if lvl * 2 < c:
            xm = [mmbd(jnp.concatenate([x[g], m[g]], axis=0), m[g]) for g in groups]
            x = [x[g] + xm[g][:c] for g in groups]
            m = [xm[g][c:] for g in groups]
        else:
            x = [x[g] + mmbd(x[g], m[g]) for g in groups]
        lvl *= 2
    s = [s_ref[g] for g in groups]
    sa = [_mm_nt(ar[g], s[g]) for g in groups]
    av = [mmbd(a_kk[g], v[g]) for g in groups]
    u = [mmbd(x[g], sa[g][:c] + av[g][:c]) for g in groups]
    y = [sa[g][c:] + av[g][c:] + mmbd(a_rb[g], u[g]) for g in groups]
    upd = [_mm(jnp.concatenate([u[g], v[g]], axis=0).T, bk[g]) for g in groups]
    for g in groups:
        y_ref[:, sl[g]] = y[g]
        s_ref[g] = s[g] * jnp.exp(tot[g]) + jnp.where(bd_mask, upd[g], 0.0)

    @pl.when(i == nc - 1)
    def _():
        sfin_ref[...] = s_ref[...]


def _rwkv_scan(r, v, kk, lw, key, bvec, s0):
    bsz, l, _ = r.shape
    c = RW_CHUNK
    nc = l // c

    def cidx(d, i):
        return jnp.where(d == 0, i, nc - 1 - i)

    tok = pl.BlockSpec((None, c, D), lambda d, b, i: (b, cidx(d, i), 0))
    tok2 = pl.BlockSpec((None, None, c, D), lambda d, b, i: (d, b, cidx(d, i), 0))
    st = pl.BlockSpec((None, None, RW_NG, RW_GW, RW_GW), lambda d, b, i: (d, b, 0, 0, 0))
    return pl.pallas_call(
        functools.partial(_rwkv_scan_kernel, nc=nc),
        grid=(2, bsz, nc),
        in_specs=[tok, tok, tok, tok2, tok2, tok2, st],
        out_specs=[tok2, st],
        out_shape=[jax.ShapeDtypeStruct((2, bsz, l, D), f32),
                   jax.ShapeDtypeStruct((2, bsz, RW_NG, RW_GW, RW_GW), f32)],
        scratch_shapes=[pltpu.VMEM((RW_NG, RW_GW, RW_GW), f32)],
        compiler_params=_cparams(3),
        name="rwkv_scan",
    )(r, v, kk, lw, key, bvec, s0)


def _rwkv_out_kernel(y_ref, bonus_ref, g_ref, lnw_ref, lnb_ref, ones_ref, o_ref):
    ones_bd = ones_ref[...]
    for g in range(RW_NG):
        cs = slice(g * RW_GW, (g + 1) * RW_GW)
        y = y_ref[0, :, cs] + y_ref[1, :, cs]
        mean = _mm_x3(y, ones_bd) * (1.0 / HEAD)
        yc = y - mean
        var = _mm_x3(yc * yc, ones_bd) * (1.0 / HEAD)
        yn = yc * lax.rsqrt(var + GN_EPS) * lnw_ref[:, cs] + lnb_ref[:, cs]
        o_ref[:, cs] = ((yn + bonus_ref[:, cs]) * g_ref[:, cs]).astype(o_ref.dtype)


def _rwkv_out(y2, bonus, g, ln_w, ln_b, ones_bd):
    _, bsz, l, _ = y2.shape
    tl = min(256, l)
    tok = pl.BlockSpec((None, tl, D), lambda b, i: (b, i, 0))
    vec = pl.BlockSpec((1, D), lambda b, i: (0, 0))
    return pl.pallas_call(
        _rwkv_out_kernel,
        grid=(bsz, l // tl),
        in_specs=[pl.BlockSpec((2, None, tl, D), lambda b, i: (0, b, i, 0)), tok, tok, vec, vec,
                  pl.BlockSpec((RW_GW, RW_GW), lambda b, i: (0, 0))],
        out_specs=tok,
        out_shape=jax.ShapeDtypeStruct((bsz, l, D), bf16),
        compiler_params=_cparams(2),
        name="rwkv_out",
    )(y2, bonus, g, ln_w, ln_b, ones_bd)


def _ssd_kernel(p_ref, prev_ref, next_ref, cw_ref, cb_ref, dtb_ref, a_ref, dsk_ref, e_ref, eq_ref,
                h0_ref, y_ref, hfin_ref, pe_ref, xbc_ref, cole_ref, cumt_ref, dtt_ref, h_ref, *, nc):
    q = SSM_Q
    d = pl.program_id(0)
    i = pl.program_id(2)
    fwd = d == 0
    t = jnp.where(fwd, i, nc - 1 - i)

    @pl.when(i == 0)
    def _():
        h_ref[...] = h0_ref[...]

    zero = jnp.zeros((HALO, D_XBC), f32)
    pe_ref[0:HALO, :] = jnp.where(t > 0, prev_ref[:, D:D + D_XBC], zero)
    pe_ref[HALO:HALO + q, :] = p_ref[:, D:D + D_XBC]
    pe_ref[HALO + q:2 * HALO + q, :] = jnp.where(t < nc - 1, next_ref[:, D:D + D_XBC], zero)
    for n in range(D_XBC // 128):
        cs = slice(n * 128, (n + 1) * 128)
        xc = cb_ref[:, cs]
        for tap in range(CONV_W):
            r0 = HALO - CONV_LEFT + tap
            xc = xc + pe_ref[r0:r0 + q, cs] * cw_ref[tap:tap + 1, cs]
        xbc_ref[:, cs] = _silu(xc)

    row = _iota((q, q), 0)
    col = _iota((q, q), 1)
    sgn = jnp.where(fwd, 1, -1)
    low = (col - row) * sgn <= 0
    tri = jnp.where(low, 1.0, 0.0).astype(bf16)
    tri_t = jnp.where((row - col) * sgn <= 0, 1.0, 0.0).astype(bf16)

    dt = _softplus(p_ref[:, SS_OFF_DT:SS_OFF_DT + 128] + dtb_ref[...])
    dta = dt * a_ref[...]
    cum = _mm_e3(tri, dta)
    tot = jnp.sum(dta, axis=0, keepdims=True)
    cumt_ref[...] = _mm_x3(dta.T, tri_t)
    dtt_ref[...] = dt.T
    e_d = e_ref[...]
    dec_e = _mm_x3(jnp.exp(cum), e_d)
    toend_e = _mm_x3(jnp.exp(tot - cum) * dt, e_d)
    tot_e = _mm_x3(jnp.broadcast_to(jnp.exp(tot), (8, 128)), e_d)[0:1]
    cole_ref[...] = _mm_x3(cum, eq_ref[...])
    lane = _iota((q, 128), 1)

    for g in range(SSM_G):
        bg = xbc_ref[:, D + g * SSM_N:D + (g + 1) * SSM_N]
        cg = xbc_ref[:, D + SSM_G * SSM_N + g * SSM_N:D + SSM_G * SSM_N + (g + 1) * SSM_N]
        cb = _mm_nt(cg, bg)
        gs = slice(g * 512, (g + 1) * 512)
        hg = h_ref[g]
        y_off = _mm(cg, hg) * dec_e[:, gs]
        for pr in range(4):
            ls = []
            for hh in range(2):
                h = g * 8 + pr * 2 + hh
                rowv = cumt_ref[pl.ds(d * 32 + h, 1), :]
                dtr = dtt_ref[pl.ds(d * 32 + h, 1), :]
                seg = cole_ref[:, h * q:(h + 1) * q] - rowv
                ls.append(jnp.where(low, jnp.exp(jnp.where(low, seg, 0.0)), 0.0) * cb * dtr)
            lp = jnp.concatenate(ls, axis=1)
            ps = slice(g * 512 + pr * 128, g * 512 + (pr + 1) * 128)
            xp = xbc_ref[:, ps]
            bd2 = jnp.concatenate([jnp.where(lane < SSM_P, xp, 0.0), jnp.where(lane >= SSM_P, xp, 0.0)],
                                  axis=0)
            yd = _mm(lp, bd2)
            y_ref[:, ps] = yd + y_off[:, pr * 128:(pr + 1) * 128] + dsk_ref[:, ps] * xp
        xs = xbc_ref[:, gs] * toend_e[:, gs]
        states = _mm(bg.T, xs)
        h_ref[g] = hg * tot_e[:, gs] + states

    @pl.when(i == nc - 1)
    def _():
        hfin_ref[...] = h_ref[...]


def _ssd(p, prm, h0):
    bsz, l, _ = p.shape
    q = SSM_Q
    nc = l // q
    nh = l // HALO
    tpb = q // HALO

    def cidx(d, i):
        return jnp.where(d == 0, i, nc - 1 - i)

    full = lambda shape: pl.BlockSpec(shape, lambda d, b, i: (0,) * len(shape))
    st = pl.BlockSpec((None, None, SSM_G, SSM_N, 8 * SSM_P), lambda d, b, i: (d, b, 0, 0, 0))
    return pl.pallas_call(
        functools.partial(_ssd_kernel, nc=nc),
        grid=(2, bsz, nc),
        in_specs=[pl.BlockSpec((None, q, N_SSM_PAD), lambda d, b, i: (b, cidx(d, i), 0)),
                  pl.BlockSpec((None, HALO, N_SSM_PAD),
                               lambda d, b, i: (b, jnp.maximum(cidx(d, i) * tpb - 1, 0), 0)),
                  pl.BlockSpec((None, HALO, N_SSM_PAD),
                               lambda d, b, i: (b, jnp.minimum((cidx(d, i) + 1) * tpb, nh - 1), 0)),
                  full((CONV_W, D_XBC)), full((1, D_XBC)), full((1, 128)), full((1, 128)),
                  pl.BlockSpec((None, 1, D), lambda d, b, i: (d, 0, 0)),
                  pl.BlockSpec((None, 128, D), lambda d, b, i: (d, 0, 0)),
                  pl.BlockSpec((None, 128, SSM_HEADS * q), lambda d, b, i: (d, 0, 0)),
                  st],
        out_specs=[pl.BlockSpec((None, None, q, D), lambda d, b, i: (d, b, cidx(d, i), 0)), st],
        out_shape=[jax.ShapeDtypeStruct((2, bsz, l, D), f32),
                   jax.ShapeDtypeStruct((2, bsz, SSM_G, SSM_N, 8 * SSM_P), f32)],
        scratch_shapes=[pltpu.VMEM((q + 2 * HALO, D_XBC), f32), pltpu.VMEM((q, D_XBC), f32),
                        pltpu.VMEM((q, SSM_HEADS * q), f32), pltpu.VMEM((128, q), f32),
                        pltpu.VMEM((128, q), f32), pltpu.VMEM((SSM_G, SSM_N, 8 * SSM_P), f32)],
        compiler_params=_cparams(3),
        name="ssd",
    )(p, p, p, prm["cw"], prm["cb"], prm["dtb"], prm["a"], prm["dsk"], prm["e"], prm["eq"], h0)


def _ssd_out_kernel(y_ref, z_ref, nw_ref, o_ref):
    gw = D // SSM_G
    for g in range(SSM_G):
        cs = slice(g * gw, (g + 1) * gw)
        yg = (y_ref[0, :, cs] + y_ref[1, :, cs]) * _silu(z_ref[:, cs])
        ms = jnp.mean(yg * yg, axis=-1, keepdims=True)
        o_ref[:, cs] = (yg * lax.rsqrt(ms + EPS) * nw_ref[:, cs]).astype(o_ref.dtype)


def _ssd_out(y2, p, norm_w):
    _, bsz, l, _ = y2.shape
    tl = min(256, l)
    tok = pl.BlockSpec((None, tl, D), lambda b, i: (b, i, 0))
    return pl.pallas_call(
        _ssd_out_kernel,
        grid=(bsz, l // tl),
        in_specs=[pl.BlockSpec((2, None, tl, D), lambda b, i: (0, b, i, 0)), tok,
                  pl.BlockSpec((1, D), lambda b, i: (0, 0))],
        out_specs=tok,
        out_shape=jax.ShapeDtypeStruct((bsz, l, D), bf16),
        compiler_params=_cparams(2),
        name="ssd_out",
    )(y2, p, norm_w)


def _merge_kernel(ya_ref, yb_ref, yc_ref, sa_ref, sb_ref, sc_ref, wa_ref, wb_ref, wc_ref, o_ref):
    acc = None
    for y_ref, s_ref, w_ref in ((ya_ref, sa_ref, wa_ref), (yb_ref, sb_ref, wb_ref), (yc_ref, sc_ref, wc_ref)):
        t = s_ref[...].astype(f32) * jnp.dot(y_ref[...], w_ref[...], preferred_element_type=f32)
        acc = t if acc is None else acc + t
    o_ref[...] = acc.astype(o_ref.dtype)


def _merge(ys, sig, ws):
    m = ys[0].shape[0]
    tm = min(512, m)
    tn = 512
    nj = D // tn
    ysp = pl.BlockSpec((tm, D), lambda i, j: (i, 0))
    wsp = pl.BlockSpec((D, tn), lambda i, j: (0, j))
    ssp = [pl.BlockSpec((tm, tn), functools.partial(lambda i, j, k: (i, k * nj + j), k=k)) for k in range(3)]
    return pl.pallas_call(
        _merge_kernel,
        grid=(m // tm, nj),
        in_specs=[ysp, ysp, ysp] + ssp + [wsp, wsp, wsp],
        out_specs=pl.BlockSpec((tm, tn), lambda i, j: (i, j)),
        out_shape=jax.ShapeDtypeStruct((m, D), bf16),
        compiler_params=_cparams(2),
        name="merge",
    )(ys[0], ys[1], ys[2], sig, sig, sig, ws[0], ws[1], ws[2])


def _wo_kernel(m_ref, w_ref, h_ref, g_ref, o_ref):
    o_ref[...] = h_ref[...] + g_ref[...] * jnp.dot(m_ref[...], w_ref[...], preferred_element_type=f32)


def _wo_residual(m, w_o, h, mod4, row_of_batch, k_gate, *, transposed):
    bsz, l, _ = h.shape
    if transposed:
        rows = l // GRID_W
        hv = h.reshape(bsz, rows, GRID_W * D)
        tl, nt = rows, GRID_W
        hspec = pl.BlockSpec((None, tl, D), lambda b, i: (b, 0, i))
        oshape = jax.ShapeDtypeStruct((bsz, rows, GRID_W * D), f32)
    else:
        tl = min(512, l)
        nt = l // tl
        hv = h
        hspec = pl.BlockSpec((None, tl, D), lambda b, i: (b, i, 0))
        oshape = jax.ShapeDtypeStruct((bsz, l, D), f32)
    out = pl.pallas_call(
        _wo_kernel,
        grid=(bsz, nt),
        in_specs=[pl.BlockSpec((None, tl, D), lambda b, i: (b, i, 0)),
                  pl.BlockSpec((D, D), lambda b, i: (0, 0)),
                  hspec,
                  pl.BlockSpec((None, None, 1, D), lambda b, i: (row_of_batch(b), k_gate, 0, 0))],
        out_specs=hspec,
        out_shape=oshape,
        compiler_params=_cparams(2),
        name="wo_residual",
    )(m, w_o, hv, mod4)
    return out.reshape(bsz, l, D)


def _router_kernel(x_ref, g_ref, sc_ref, sh_ref, r_ref, u_ref, gates_ref):
    x = x_ref[...]
    ms = jnp.mean(x * x, axis=-1, keepdims=True)
    u = (x * lax.rsqrt(ms + EPS) * g_ref[...]) * (1.0 + sc_ref[...]) + sh_ref[...]
    u_ref[...] = u.astype(u_ref.dtype)
    u0, u1, u2 = _split3(u)
    r0, r1, r2 = _split3(r_ref[...])
    dot = lambda a, b: jnp.dot(a, b, preferred_element_type=f32)
    logits = (dot(u0, r0) + (dot(u0, r1) + dot(u1, r0))
              + (dot(u1, r1) + dot(u0, r2) + dot(u2, r0)))
    lane = _iota(logits.shape, 1)
    neg = jnp.float32(-jnp.inf)
    lg = jnp.where(lane < N_EXPERTS, logits, neg)
    m1 = jnp.max(lg, axis=-1, keepdims=True)
    i1 = jnp.min(jnp.where(lg == m1, lane, 128), axis=-1, keepdims=True)
    lg2 = jnp.where(lane == i1, neg, lg)
    m2 = jnp.max(lg2, axis=-1, keepdims=True)
    i2 = jnp.min(jnp.where(lg2 == m2, lane, 128), axis=-1, keepdims=True)
    e2 = jnp.exp(m2 - m1)
    den = 1.0 + e2
    gates_ref[...] = jnp.where(lane == i1, 1.0 / den, 0.0) + jnp.where(lane == i2, e2 / den, 0.0)


def _norm_router(h, gain, mod4, row_of_batch, k_sc, k_sh, router_pad, out_dtype=bf16):
    bsz, l, _ = h.shape
    tl = min(512, l)
    tok = pl.BlockSpec((None, tl, D), lambda b, i: (b, i, 0))
    return pl.pallas_call(
        _router_kernel,
        grid=(bsz, l // tl),
        in_specs=[tok,
                  pl.BlockSpec((1, D), lambda b, i: (0, 0)),
                  pl.BlockSpec((None, None, 1, D), lambda b, i: (row_of_batch(b), k_sc, 0, 0)),
                  pl.BlockSpec((None, None, 1, D), lambda b, i: (row_of_batch(b), k_sh, 0, 0)),
                  pl.BlockSpec((D, 128), lambda b, i: (0, 0))],
        out_specs=[tok, pl.BlockSpec((None, tl, 128), lambda b, i: (b, i, 0))],
        out_shape=[jax.ShapeDtypeStruct((bsz, l, D), out_dtype), jax.ShapeDtypeStruct((bsz, l, 128), f32)],
        compiler_params=_cparams(2),
        name="norm_router",
    )(h, gain.reshape(1, D), mod4, mod4, router_pad)


def _up_kernel(*refs, gated):
    if gated:
        u_ref, w1_ref, w3_ref, gt_ref, o_ref = refs
    else:
        u_ref, w1_ref, w3_ref, o_ref = refs
    u = u_ref[...]
    a = jnp.dot(u, w1_ref[...].astype(bf16), preferred_element_type=f32)
    b = jnp.dot(u, w3_ref[...].astype(bf16), preferred_element_type=f32)
    act = _silu(a) * b
    if gated:
        e = pl.program_id(1)
        gt = gt_ref[...]
        lane = _iota(gt.shape, 1)
        act = act * jnp.sum(jnp.where(lane == e, gt, 0.0), axis=-1, keepdims=True)
    o_ref[...] = act.astype(o_ref.dtype)


def _swiglu_up(u, w1, w3, gates=None):
    m = u.shape[0]
    ne, _, ff = w1.shape
    tm = min(1024, m)
    tn = 512
    nj = ff // tn
    gated = gates is not None
    wspec = pl.BlockSpec((None, D, tn), lambda i, e, j: (e, 0, j))
    in_specs = [pl.BlockSpec((tm, D), lambda i, e, j: (i, 0)), wspec, wspec]
    args = [u, w1, w3]
    if gated:
        in_specs.append(pl.BlockSpec((tm, 128), lambda i, e, j: (i, 0)))
        args.append(gates)
    return pl.pallas_call(
        functools.partial(_up_kernel, gated=gated),
        grid=(m // tm, ne, nj),
        in_specs=in_specs,
        out_specs=pl.BlockSpec((tm, tn), lambda i, e, j: (i, e * nj + j)),
        out_shape=jax.ShapeDtypeStruct((m, ne * ff), bf16),
        compiler_params=_cparams(3),
        name="swiglu_up",
    )(*args)


def _down_kernel(a_ref, w_ref, h_ref, g_ref, o_ref, acc_ref, *, nk):
    k = pl.program_id(2)

    @pl.when(k == 0)
    def _():
        acc_ref[...] = jnp.zeros_like(acc_ref)

    acc_ref[...] += jnp.dot(a_ref[...], w_ref[...].astype(bf16), preferred_element_type=f32)

    @pl.when(k == nk - 1)
    def _():
        o_ref[...] = h_ref[...] + g_ref[...] * acc_ref[...]


def _down_residual(act, w2, h, mod4, row_of_batch, k_gate):
    bsz, l, kk = act.shape
    tl = min(512, l)
    tk = 1024
    nk = kk // tk
    return pl.pallas_call(
        functools.partial(_down_kernel, nk=nk),
        grid=(bsz, l // tl, nk),
        in_specs=[pl.BlockSpec((None, tl, tk), lambda b, i, k: (b, i, k)),
                  pl.BlockSpec((tk, D), lambda b, i, k: (k, 0)),
                  pl.BlockSpec((None, tl, D), lambda b, i, k: (b, i, 0)),
                  pl.BlockSpec((None, None, 1, D), lambda b, i, k: (row_of_batch(b), k_gate, 0, 0))],
        out_specs=pl.BlockSpec((None, tl, D), lambda b, i, k: (b, i, 0)),
        out_shape=jax.ShapeDtypeStruct((bsz, l, D), f32),
        scratch_shapes=[pltpu.VMEM((tl, D), f32)],
        compiler_params=_cparams(3),
        name="down_residual",
    )(act, w2, h, mod4)


MOE_TM = 512
MOE_TT = 512
NACT_LANE = 127


def _moe_count_kernel(gates_ref, pre_ref, cnt_ref, carry_ref):
    i = pl.program_id(0)

    @pl.when(i == 0)
    def _():
        carry_ref[...] = jnp.zeros_like(carry_ref)

    tt = gates_ref.shape[0]
    a = jnp.where(gates_ref[...] > 0.0, 1.0, 0.0)
    strict = jnp.where(_iota((tt, tt), 1) < _iota((tt, tt), 0), 1.0, 0.0).astype(bf16)
    carry = carry_ref[0:1, :]
    pre_ref[...] = jnp.dot(strict, a.astype(bf16), preferred_element_type=f32) + carry
    carry_ref[...] = jnp.broadcast_to(carry + jnp.sum(a, axis=0, keepdims=True), carry_ref.shape)
    cnt_ref[...] = carry_ref[...]


def _moe_place_kernel(gates_ref, pre_ref, cnt_ref, pos_ref, gv_ref, tile_ref, *, n_tiles):
    tt = gates_ref.shape[0]
    tm = float(MOE_TM)
    lane1 = _iota((1, 128), 1)
    cnt = cnt_ref[0:1, :]
    gsz = jnp.floor((cnt + (tm - 1.0)) * (1.0 / tm)) * tm
    upper = jnp.where(_iota((128, 128), 0) < _iota((128, 128), 1), 1.0, 0.0).astype(bf16)
    base = _mm_x3(jnp.broadcast_to(gsz, (8, 128)), upper)[0:1]
    gates = gates_ref[...]
    act = gates > 0.0
    pos = base + pre_ref[...]
    big = jnp.float32(1e9)
    p_lo = jnp.min(jnp.where(act, pos, big), axis=-1, keepdims=True)
    p_hi = jnp.max(jnp.where(act, pos, -1.0), axis=-1, keepdims=True)
    g_lo = jnp.sum(jnp.where(act & (pos == p_lo), gates, 0.0), axis=-1, keepdims=True)
    g_hi = jnp.where(p_hi != p_lo,
                     jnp.sum(jnp.where(act & (pos == p_hi), gates, 0.0), axis=-1, keepdims=True), 0.0)
    lane = _iota((tt, 128), 1)
    posf = jnp.where(lane == 0, p_lo, 0.0) + jnp.where(lane == 1, p_hi, 0.0)
    pos_ref[...] = posf.T[0:8].astype(jnp.int32)
    gv_ref[...] = jnp.where(lane == 0, g_lo, 0.0) + jnp.where(lane == 1, g_hi, 0.0)
    end = base + gsz
    start_j = lane1.astype(f32) * tm
    texp = jnp.zeros((1, 128), f32)
    for e in range(N_EXPERTS - 1):
        end_e = jnp.sum(jnp.where(lane1 == e, end, 0.0), axis=-1, keepdims=True)
        texp = texp + jnp.where(start_j >= end_e, 1.0, 0.0)
    total = jnp.sum(jnp.where(lane1 == N_EXPERTS - 1, end, 0.0), axis=-1, keepdims=True)
    table = jnp.where(lane1 == NACT_LANE, total * (1.0 / tm), texp)
    tile_ref[...] = jnp.broadcast_to(table, (8, 128)).astype(jnp.int32)


def _moe_scatter_kernel(pos_ref, v_ref, xs_in_ref, xs_ref, sem, *, dump_base):
    del xs_in_ref
    tt = v_ref.shape[0]

    def row_copy(r, dst_row):
        return pltpu.make_async_copy(v_ref.at[pl.ds(r, 1)], xs_ref.at[pl.ds(dst_row, 1)], sem)

    def body(r, carry):
        p0 = pos_ref[0, r]
        p1 = pos_ref[1, r]
        p1 = jnp.where(p1 == p0, dump_base + r, p1)
        row_copy(r, p0).start()
        row_copy(r, p1).start()
        return carry

    lax.fori_loop(0, tt, body, 0)

    def wait_body(r, carry):
        row_copy(0, 0).wait()
        row_copy(0, 0).wait()
        return carry

    lax.fori_loop(0, tt, wait_body, 0)


def _moe_gup_kernel(tile_ref, x_ref, w1_ref, w3_ref, o_ref):
    t = pl.program_id(1)

    @pl.when(t < tile_ref[NACT_LANE])
    def _():
        x = x_ref[...].astype(bf16)
        a = jnp.dot(x, w1_ref[...].astype(bf16), preferred_element_type=f32)
        b = jnp.dot(x, w3_ref[...].astype(bf16), preferred_element_type=f32)
        o_ref[...] = (_silu(a) * b).astype(o_ref.dtype)

    @pl.when(t >= tile_ref[NACT_LANE])
    def _():
        o_ref[...] = jnp.zeros_like(o_ref)


def _moe_gdown_kernel(tile_ref, a_ref, w_ref, o_ref):
    t = pl.program_id(1)

    @pl.when(t < tile_ref[NACT_LANE])
    def _():
        o_ref[...] = jnp.dot(a_ref[...], w_ref[...].astype(bf16), preferred_element_type=f32)

    @pl.when(t >= tile_ref[NACT_LANE])
    def _():
        o_ref[...] = jnp.zeros_like(o_ref)


def _moe_combine_kernel(pos_ref, gv_ref, h_ref, gf_ref, ys_ref, o_ref, buf0, buf1, sem):
    tt = h_ref.shape[0]

    def row_copy(src_row, buf, r):
        return pltpu.make_async_copy(ys_ref.at[pl.ds(src_row, 1)], buf.at[pl.ds(r, 1)], sem)

    def body(r, carry):
        row_copy(pos_ref[0, r], buf0, r).start()
        row_copy(pos_ref[1, r], buf1, r).start()
        return carry

    lax.fori_loop(0, tt, body, 0)

    def wait_body(r, carry):
        row_copy(0, buf0, 0).wait()
        row_copy(0, buf1, 0).wait()
        return carry

    lax.fori_loop(0, tt, wait_body, 0)
    gv = gv_ref[...]
    lane = _iota(gv.shape, 1)
    g0 = jnp.sum(jnp.where(lane == 0, gv, 0.0), axis=-1, keepdims=True)
    g1 = jnp.sum(jnp.where(lane == 1, gv, 0.0), axis=-1, keepdims=True)
    o_ref[...] = h_ref[...] + gf_ref[...] * (g0 * buf0[...] + g1 * buf1[...])


def _moe_sparse(v32, gates, h, w1, w3, w2, mod4, row_of_batch, k_gate):
    bsz, l, _ = h.shape
    n_tok = bsz * l
    tt = min(MOE_TT, n_tok)
    n_tt = n_tok // tt
    n_tiles = (2 * n_tok) // MOE_TM + N_EXPERTS
    rows = n_tiles * MOE_TM
    ff = w1.shape[2]
    vf = v32.reshape(n_tok, D)
    gf = gates.reshape(n_tok, 128)

    pre, cnt = pl.pallas_call(
        _moe_count_kernel,
        grid=(n_tt,),
        in_specs=[pl.BlockSpec((tt, 128), lambda i: (i, 0))],
        out_specs=[pl.BlockSpec((tt, 128), lambda i: (i, 0)), pl.BlockSpec((8, 128), lambda i: (0, 0))],
        out_shape=[jax.ShapeDtypeStruct((n_tok, 128), f32), jax.ShapeDtypeStruct((8, 128), f32)],
        scratch_shapes=[pltpu.VMEM((8, 128), f32)],
        compiler_params=_cparams(1),
        name="moe_count",
    )(gf)

    pos, gv, table = pl.pallas_call(
        functools.partial(_moe_place_kernel, n_tiles=n_tiles),
        grid=(n_tt,),
        in_specs=[pl.BlockSpec((tt, 128), lambda i: (i, 0)), pl.BlockSpec((tt, 128), lambda i: (i, 0)),
                  pl.BlockSpec((8, 128), lambda i: (0, 0))],
        out_specs=[pl.BlockSpec((8, tt), lambda i: (0, i)), pl.BlockSpec((tt, 128), lambda i: (i, 0)),
                   pl.BlockSpec((8, 128), lambda i: (0, 0))],
        out_shape=[jax.ShapeDtypeStruct((8, n_tok), jnp.int32), jax.ShapeDtypeStruct((n_tok, 128), f32),
                   jax.ShapeDtypeStruct((8, 128), jnp.int32)],
        compiler_params=_cparams(1),
        name="moe_place",
    )(gf, pre, cnt)
    tile_tab = table[0]

    smem_pos = pl.BlockSpec((8, tt), lambda i: (0, i), memory_space=pltpu.SMEM)
    xs = pl.pallas_call(
        functools.partial(_moe_scatter_kernel, dump_base=rows),
        grid=(n_tt,),
        in_specs=[smem_pos, pl.BlockSpec((tt, D), lambda i: (i, 0)), pl.BlockSpec(memory_space=pl.ANY)],
        out_specs=pl.BlockSpec(memory_space=pl.ANY),
        out_shape=jax.ShapeDtypeStruct((rows + tt, D), f32),
        scratch_shapes=[pltpu.SemaphoreType.DMA],
        input_output_aliases={2: 0},
        compiler_params=_cparams(1),
        name="moe_scatter",
    )(pos, vf, jnp.zeros((rows + tt, D), f32))

    assert n_tiles < NACT_LANE
    tn = 512
    nj = ff // tn
    last_tile = lambda tab: jnp.maximum(tab[NACT_LANE] - 1, 0)
    act = pl.pallas_call(
        _moe_gup_kernel,
        grid_spec=pltpu.PrefetchScalarGridSpec(
            num_scalar_prefetch=1,
            grid=(nj, n_tiles),
            in_specs=[pl.BlockSpec((MOE_TM, D), lambda j, t, tab: (jnp.minimum(t, last_tile(tab)), 0)),
                      pl.BlockSpec((None, D, tn), lambda j, t, tab: (tab[jnp.minimum(t, last_tile(tab))], 0, j)),
                      pl.BlockSpec((None, D, tn), lambda j, t, tab: (tab[jnp.minimum(t, last_tile(tab))], 0, j))],
            out_specs=pl.BlockSpec((MOE_TM, tn), lambda j, t, tab: (t, j))),
        out_shape=jax.ShapeDtypeStruct((rows, ff), bf16),
        compiler_params=_cparams(2),
        name="moe_up",
    )(tile_tab, xs, w1, w3)

    tnd = 1024
    ys = pl.pallas_call(
        _moe_gdown_kernel,
        grid_spec=pltpu.PrefetchScalarGridSpec(
            num_scalar_prefetch=1,
            grid=(D // tnd, n_tiles),
            in_specs=[pl.BlockSpec((MOE_TM, ff), lambda j, t, tab: (jnp.minimum(t, last_tile(tab)), 0)),
                      pl.BlockSpec((None, ff, tnd), lambda j, t, tab: (tab[jnp.minimum(t, last_tile(tab))], 0, j))],
            out_specs=pl.BlockSpec((MOE_TM, tnd), lambda j, t, tab: (t, j))),
        out_shape=jax.ShapeDtypeStruct((rows, D), f32),
        compiler_params=_cparams(2),
        name="moe_down",
    )(tile_tab, act, w2)

    tpb = l // tt if l >= tt else 1
    out = pl.pallas_call(
        _moe_combine_kernel,
        grid=(n_tt,),
        in_specs=[smem_pos, pl.BlockSpec((tt, 128), lambda i: (i, 0)), pl.BlockSpec((tt, D), lambda i: (i, 0)),
                  pl.BlockSpec((None, None, 1, D), lambda i: (row_of_batch(i // tpb), k_gate, 0, 0)),
                  pl.BlockSpec(memory_space=pl.ANY)],
        out_specs=pl.BlockSpec((tt, D), lambda i: (i, 0)),
        out_shape=jax.ShapeDtypeStruct((n_tok, D), f32),
        scratch_shapes=[pltpu.VMEM((tt, D), f32), pltpu.VMEM((tt, D), f32), pltpu.SemaphoreType.DMA],
        compiler_params=_cparams(1),
        name="moe_combine",
    )(pos, gv, h.reshape(n_tok, D), mod4, ys)
    return out.reshape(bsz, l, D)


def _block_ones(n, bs):
    idx = jnp.arange(n) // bs
    return (idx[:, None] == idx[None, :]).astype(bf16)


def _pad_cols(w, n):
    return jnp.pad(w, ((0, 0), (0, n - w.shape[1])))


def _rwkv_params(li, mu, w0, w_up, a0, a_up, g_up, k_k, k_a, r_k, ln_w, ln_b):
    wup = jnp.zeros((2, 256, D), f32)
    aup = jnp.zeros((2, 256, D), f32)
    for d in range(2):
        wup = wup.at[d, d * W_LORA:(d + 1) * W_LORA].set(w_up[li, d])
        o = (RW_OFF_G - 2 * A_LORA) - RW_OFF_A + d * A_LORA
        aup = aup.at[d, o:o + A_LORA].set(a_up[li, d])
    return {
        "mu": jnp.pad(mu[li], (0, N_RWKV_PAD - N_RWKV)).reshape(1, N_RWKV_PAD),
        "w0": w0[li], "wup": wup.astype(bf16), "a0": a0[li], "aup": aup.astype(bf16),
        "gup": g_up[li].astype(bf16), "k_k": k_k[li].reshape(1, D), "k_a": k_a[li].reshape(1, D),
        "r_k": r_k[li].reshape(1, D), "ln_w": ln_w[li].reshape(1, D), "ln_b": ln_b[li].reshape(1, D),
        "ones_bd": _block_ones(RW_GW, HEAD),
    }


def _ssd_params(li, conv_w, conv_b, a_log, dt_bias, d_skip):
    a = -jnp.exp(a_log[li].astype(f32))
    hid = jnp.arange(D) // SSM_P
    hq = jnp.arange(SSM_HEADS * SSM_Q) // SSM_Q
    sel = jnp.arange(128)[:, None]
    e = jnp.stack([(sel == hid[None, :] + d * SSM_HEADS) for d in range(2)]).astype(bf16)
    eq = jnp.stack([(sel == hq[None, :] + d * SSM_HEADS) for d in range(2)]).astype(bf16)
    return {
        "cw": conv_w[li], "cb": conv_b[li].reshape(1, D_XBC),
        "dtb": jnp.pad(dt_bias[li].reshape(1, 2 * SSM_HEADS), ((0, 0), (0, 64))),
        "a": jnp.pad(a.reshape(1, 2 * SSM_HEADS), ((0, 0), (0, 64))),
        "dsk": jnp.stack([jnp.repeat(d_skip[li], SSM_P), jnp.zeros((D,), f32)]).reshape(2, 1, D),
        "e": e, "eq": eq,
    }


def kernel(x, c, ctx, c_ctx, ada_w, ada_b, norm_mix, norm_ffn, norm_final, w_in, lru_conv_w, lru_conv_b, lru_gate_w, lru_gate_b, lru_lambda, rwkv_mu, rwkv_w0, rwkv_w_up, rwkv_a0, rwkv_a_up, rwkv_g_up, rwkv_k_k, rwkv_k_a, rwkv_r_k, rwkv_ln_w, rwkv_ln_b, ssm_conv_w, ssm_conv_b, ssm_a_log, ssm_dt_bias, ssm_d, ssm_norm_w, w_out_lru, w_out_rwkv, w_out_ssm, w_o, ffn_w1, ffn_w3, ffn_w2, moe_router, moe_w1, moe_w3, moe_w2):
    bsz, l, _ = x.shape
    depth = ada_w.shape[0]
    off_lru = 3 * D
    off_rwkv = off_lru + 2 * D
    off_ssm = off_rwkv + N_RWKV
    cvec = jnp.zeros((8, D), f32).at[:bsz].set(c).at[bsz].set(c_ctx)
    lat_row = lambda b: b
    ctx_row = lambda b: bsz

    h_lat, h_ctx = x, ctx
    for li in range(depth):
        last = li == depth - 1
        odd = li % 2 == 1
        mod4 = _ada(cvec, ada_w[li], ada_b[li]).reshape(8, 6, 1, D)

        wl = w_in[li]
        w_ss = _pad_cols(wl[:, off_ssm:], N_SSM_PAD)
        lru_gw = [jnp.concatenate([lru_gate_w[li, d, 0], lru_gate_w[li, d, 1]], axis=-1).astype(bf16)
                  for d in range(2)]
        rp = _rwkv_params(li, rwkv_mu, rwkv_w0, rwkv_w_up, rwkv_a0, rwkv_a_up, rwkv_g_up, rwkv_k_k,
                          rwkv_k_a, rwkv_r_k, rwkv_ln_w, rwkv_ln_b)
        sp = _ssd_params(li, ssm_conv_w, ssm_conv_b, ssm_a_log, ssm_dt_bias, ssm_d)
        w_outs = [w_out_lru[li].astype(bf16), w_out_rwkv[li].astype(bf16), w_out_ssm[li].astype(bf16)]
        w_o_b = w_o[li].astype(bf16)

        def token_mix(u, states, need_out):
            lx = u.shape[1]
            um = u.reshape(bsz * lx, D)
            p_lru = _proj(um, wl, tn=1024, col0=off_lru, n=2 * D).reshape(bsz, lx, 2 * D)
            p_rw = _proj(um, wl, tn=512, col0=off_rwkv, n=N_RWKV_PAD).reshape(bsz, lx, N_RWKV_PAD)
            p_ss = _proj(um, w_ss, tn=768).reshape(bsz, lx, N_SSM_PAD)
            lru_s, rw_s, ss_s = states
            cw, cb = lru_conv_w[li], lru_conv_b[li].reshape(1, D)
            hb, hl_b = _lru_pass(p_lru, cw, cb, lru_gw[1], lru_gate_b[li, 1], lru_lambda[li, 1].reshape(1, D),
                                 lru_s[1], None, reverse=True)
            ya, hl_f = _lru_pass(p_lru, cw, cb, lru_gw[0], lru_gate_b[li, 0], lru_lambda[li, 0].reshape(1, D),
                                 lru_s[0], hb, reverse=False)
            r, v, kk, g, bonus, lw, key, bvec = _rwkv_feat(p_rw, rp)
            y2, rw_fin = _rwkv_scan(r, v, kk, lw, key, bvec, rw_s)
            ys2, ss_fin = _ssd(p_ss, sp, ss_s)
            new_states = ((hl_f, hl_b), rw_fin, ss_fin)
            if not need_out:
                return None, new_states
            yb = _rwkv_out(y2, bonus, g, rp["ln_w"], rp["ln_b"], rp["ones_bd"])
            yc = _ssd_out(ys2, p_ss, ssm_norm_w[li].reshape(1, D))
            sig = _proj(um, wl, tn=1024, col0=0, n=off_lru, act="sigmoid", out_dtype=bf16)
            m = _merge([ya.reshape(-1, D), yb.reshape(-1, D), yc.reshape(-1, D)], sig, w_outs)
            return m.reshape(bsz, lx, D), new_states

        zero_states = ((jnp.zeros((bsz, 1, D), f32), jnp.zeros((bsz, 1, D), f32)),
                       jnp.zeros((2, bsz, RW_NG, RW_GW, RW_GW), f32),
                       jnp.zeros((2, bsz, SSM_G, SSM_N, 8 * SSM_P), f32))

        u_ctx = _norm(h_ctx, norm_mix[li], mod4, ctx_row, 1, 0)
        u_lat = _norm(h_lat, norm_mix[li], mod4, lat_row, 1, 0, transposed=odd)
        m_ctx, ctx_states = token_mix(u_ctx, zero_states, not last)
        m_lat, _ = token_mix(u_lat, ctx_states, True)
        h_lat = _wo_residual(m_lat, w_o_b, h_lat, mod4, lat_row, 2, transposed=odd)
        if not last:
            h_ctx = _wo_residual(m_ctx, w_o_b, h_ctx, mod4, ctx_row, 2, transposed=False)

        j = li // 2
        streams = [(h_lat, lat_row)] + ([] if last else [(h_ctx, ctx_row)])
        outs = []
        for h, row_fn in streams:
            lx = h.shape[1]
            if not odd:
                v = _norm(h, norm_ffn[li], mod4, row_fn, 4, 3)
                act = _swiglu_up(v.reshape(bsz * lx, D), ffn_w1[j][None], ffn_w3[j][None])
                w2 = ffn_w2[j]
            elif row_fn is lat_row:
                router_pad = _pad_cols(moe_router[j], 128)
                v32, gates = _norm_router(h, norm_ffn[li], mod4, row_fn, 4, 3, router_pad, out_dtype=f32)
                outs.append(_moe_sparse(v32, gates, h, moe_w1[j], moe_w3[j], moe_w2[j], mod4, row_fn, 5))
                continue
            else:
                router_pad = _pad_cols(moe_router[j], 128)
                v, gates = _norm_router(h, norm_ffn[li], mod4, row_fn, 4, 3, router_pad)
                act = _swiglu_up(v.reshape(bsz * lx, D), moe_w1[j].astype(bf16), moe_w3[j].astype(bf16),
                                 gates.reshape(bsz * lx, 128))
                w2 = moe_w2[j].astype(bf16).reshape(N_EXPERTS * D_FF_EXPERT, D)
            outs.append(_down_residual(act.reshape(bsz, lx, -1), w2, h, mod4, row_fn, 5))
        h_lat = outs[0]
        if not last:
            h_ctx = outs[1]
    return _final_norm(h_lat, norm_final)
```

```python
import functools
import math

import jax
import jax.numpy as jnp
from jax import lax
from jax.experimental import pallas as pl
from jax.experimental.pallas import tpu as pltpu

f32 = jnp.float32
bf16 = jnp.bfloat16

D = 2048
GRID_W = 64
EPS = 1e-6
CONV_W = 4
CONV_LEFT = 2
HALO = 8

LRU_BLOCKS = 16
LRU_BS = D // LRU_BLOCKS
LRU_C = 8.0

HEADS = 32
HEAD = 64
W_LORA = 96
A_LORA = 96
G_LORA = 256
GN_EPS = 64e-5
N_RWKV = 3 * D + 2 * W_LORA + 2 * A_LORA + G_LORA
N_RWKV_PAD = 7168
RW_OFF_W = 3 * D
RW_OFF_A = 3 * D + 128
RW_OFF_G = 3 * D + 2 * W_LORA + 2 * A_LORA
RW_CHUNK = 64
RW_GW = 256
RW_NG = D // RW_GW

SSM_HEADS = 32
SSM_P = 64
SSM_N = 128
SSM_G = 4
SSM_Q = 128
D_XBC = D + 2 * SSM_G * SSM_N
N_SSM = D + D_XBC + 2 * SSM_HEADS
N_SSM_PAD = 5376
SS_OFF_DT = D + D_XBC

D_FF = 3 * D
N_EXPERTS = 8
D_FF_EXPERT = D_FF // 2

VMEM_LIMIT = 56 * 1024 * 1024


def _cparams(n_axes, vmem=VMEM_LIMIT):
    return pltpu.CompilerParams(dimension_semantics=("arbitrary",) * n_axes, vmem_limit_bytes=vmem)


def _mm(a, b):
    return jnp.dot(a.astype(bf16), b.astype(bf16), preferred_element_type=f32)


def _mm_nt(a, b):
    return lax.dot_general(a.astype(bf16), b.astype(bf16), (((1,), (1,)), ((), ())),
                           preferred_element_type=f32)


def _split3(x):
    x0 = x.astype(bf16)
    r = x - x0.astype(f32)
    x1 = r.astype(bf16)
    r = r - x1.astype(f32)
    return x0, x1, r.astype(bf16)


def _mm_x3(x, e):
    x0, x1, x2 = _split3(x)
    return (jnp.dot(x0, e, preferred_element_type=f32) + jnp.dot(x1, e, preferred_element_type=f32)
            + jnp.dot(x2, e, preferred_element_type=f32))


def _mm_x2(x, e):
    x0 = x.astype(bf16)
    x1 = (x - x0.astype(f32)).astype(bf16)
    return jnp.dot(x0, e, preferred_element_type=f32) + jnp.dot(x1, e, preferred_element_type=f32)


def _mm_e3(e, x):
    x0, x1, x2 = _split3(x)
    return (jnp.dot(e, x0, preferred_element_type=f32) + jnp.dot(e, x1, preferred_element_type=f32)
            + jnp.dot(e, x2, preferred_element_type=f32))


def _softplus(x):
    return jnp.maximum(x, 0.0) + jnp.log1p(jnp.exp(-jnp.abs(x)))


def _sigmoid(x):
    return jax.nn.sigmoid(x)


def _silu(x):
    return x * jax.nn.sigmoid(x)


def _iota(shape, dim):
    return lax.broadcasted_iota(jnp.int32, shape, dim)


def _ada_kernel(c_ref, w_ref, b_ref, o_ref):
    cv = c_ref[...]
    o_ref[...] = _mm(_silu(cv), w_ref[...]) + b_ref[...]


def _ada(cvec, w, b, li):
    depth, _, n = w.shape
    tn = 1536
    return pl.pallas_call(
        _ada_kernel,
        grid=(n // tn,),
        in_specs=[pl.BlockSpec((8, D), lambda j: (0, 0)),
                  pl.BlockSpec((None, D, tn), lambda j: (li, 0, j)),
                  pl.BlockSpec((None, 1, tn), lambda j: (li, 0, j))],
        out_specs=pl.BlockSpec((8, tn), lambda j: (0, j)),
        out_shape=jax.ShapeDtypeStruct((8, n), f32),
        compiler_params=_cparams(1),
        name="ada",
    )(cvec, w, b.reshape(depth, 1, n))


def _norm_kernel(x_ref, g_ref, sc_ref, sh_ref, o_ref):
    x = x_ref[...]
    ms = jnp.mean(x * x, axis=-1, keepdims=True)
    xn = x * lax.rsqrt(ms + EPS) * g_ref[...]
    o_ref[...] = (xn * (1.0 + sc_ref[...]) + sh_ref[...]).astype(o_ref.dtype)


def _norm(h, gain, mod4, row_of_batch, k_sc, k_sh, *, transposed=False, out_dtype=bf16):
    bsz, l, _ = h.shape
    if transposed:
        rows = l // GRID_W
        hin = h.reshape(bsz, rows, GRID_W * D)
        tl = rows
        nt = GRID_W
        in_spec = pl.BlockSpec((None, tl, D), lambda b, i: (b, 0, i))
    else:
        tl = min(512, l)
        nt = l // tl
        hin = h
        in_spec = pl.BlockSpec((None, tl, D), lambda b, i: (b, i, 0))
    return pl.pallas_call(
        _norm_kernel,
        grid=(bsz, nt),
        in_specs=[in_spec,
                  pl.BlockSpec((1, D), lambda b, i: (0, 0)),
                  pl.BlockSpec((None, None, 1, D), lambda b, i: (row_of_batch(b), k_sc, 0, 0)),
                  pl.BlockSpec((None, None, 1, D), lambda b, i: (row_of_batch(b), k_sh, 0, 0))],
        out_specs=pl.BlockSpec((None, tl, D), lambda b, i: (b, i, 0)),
        out_shape=jax.ShapeDtypeStruct((bsz, l, D), out_dtype),
        compiler_params=_cparams(2),
        name="norm",
    )(hin, gain.reshape(1, D), mod4, mod4)


def _plain_norm_kernel(x_ref, g_ref, o_ref):
    x = x_ref[...]
    ms = jnp.mean(x * x, axis=-1, keepdims=True)
    o_ref[...] = (x * lax.rsqrt(ms + EPS) * g_ref[...]).astype(o_ref.dtype)


def _final_norm(h, gain):
    bsz, l, _ = h.shape
    tl = min(512, l)
    return pl.pallas_call(
        _plain_norm_kernel,
        grid=(bsz, l // tl),
        in_specs=[pl.BlockSpec((None, tl, D), lambda b, i: (b, i, 0)),
                  pl.BlockSpec((1, D), lambda b, i: (0, 0))],
        out_specs=pl.BlockSpec((None, tl, D), lambda b, i: (b, i, 0)),
        out_shape=jax.ShapeDtypeStruct((bsz, l, D), f32),
        compiler_params=_cparams(2),
        name="final_norm",
    )(h, gain.reshape(1, D))


def _proj_kernel(u_ref, w_ref, o_ref, wb_ref, *, act):
    @pl.when(pl.program_id(1) == 0)
    def _():
        wb_ref[...] = w_ref[...].astype(bf16)

    acc = jnp.dot(u_ref[...], wb_ref[...], preferred_element_type=f32)
    if act == "sigmoid":
        acc = _sigmoid(acc)
    o_ref[...] = acc.astype(o_ref.dtype)


def _proj(u, w, li, *, tn, col0=0, n=None, act=None, out_dtype=f32):
    m, k = u.shape
    n = w.shape[2] if n is None else n
    assert col0 % tn == 0 and n % tn == 0 and col0 + n <= w.shape[2]
    off = col0 // tn
    tm = min(1024, m)
    return pl.pallas_call(
        functools.partial(_proj_kernel, act=act),
        grid=(n // tn, m // tm),
        in_specs=[pl.BlockSpec((tm, k), lambda j, i: (i, 0)),
                  pl.BlockSpec((None, k, tn), lambda j, i: (li, 0, j + off))],
        out_specs=pl.BlockSpec((tm, tn), lambda j, i: (i, j)),
        out_shape=jax.ShapeDtypeStruct((m, n), out_dtype),
        scratch_shapes=[pltpu.VMEM((k, tn), bf16)],
        compiler_params=_cparams(2),
        name="proj",
    )(u, w)


def _lru_kernel(*refs, reverse, final, tl, nt):
    if final:
        (x_ref, prev_ref, next_ref, gate_ref, hb_ref, cw_ref, cb_ref, gw_ref, gb_ref, lam_ref, h0_ref,
         out_ref, hlast_ref, xe_ref, a_ref, bx_ref, hs_ref, h_ref) = refs
    else:
        (x_ref, prev_ref, next_ref, cw_ref, cb_ref, gw_ref, gb_ref, lam_ref, h0_ref,
         out_ref, hlast_ref, xe_ref, a_ref, bx_ref, hs_ref, h_ref) = refs
    i = pl.program_id(1)
    t = (nt - 1 - i) if reverse else i

    @pl.when(i == 0)
    def _():
        h_ref[...] = h0_ref[...]

    zero = jnp.zeros((HALO, D), f32)
    xe_ref[0:HALO, :] = jnp.where(t > 0, prev_ref[...], zero)
    xe_ref[HALO:HALO + tl, :] = x_ref[...]
    xe_ref[HALO + tl:2 * HALO + tl, :] = jnp.where(t < nt - 1, next_ref[...], zero)

    for n in range(LRU_BLOCKS):
        cs = slice(n * LRU_BS, (n + 1) * LRU_BS)
        xc = cb_ref[:, cs]
        for tap in range(CONV_W):
            r0 = HALO - CONV_LEFT + tap
            xc = xc + xe_ref[r0:r0 + tl, cs] * cw_ref[tap:tap + 1, cs]
        g = _mm(xc, gw_ref[n])
        rec = _sigmoid(g[:, :LRU_BS] + gb_ref[0:1, cs])
        inp = _sigmoid(g[:, LRU_BS:] + gb_ref[1:2, cs])
        log_a = -LRU_C * rec * _softplus(-lam_ref[:, cs])
        a_ref[:, cs] = jnp.exp(log_a)
        th = jnp.tanh(log_a)
        bx_ref[:, cs] = jnp.sqrt(-2.0 * th / (1.0 - th)) * inp * xc

    def body(s, h):
        tt = (tl - 1 - s) if reverse else s
        h = a_ref[pl.ds(tt, 1), :] * h + bx_ref[pl.ds(tt, 1), :]
        hs_ref[pl.ds(tt, 1), :] = h
        return h

    h = lax.fori_loop(0, tl, body, h_ref[...], unroll=8)
    h_ref[...] = h

    @pl.when(i == nt - 1)
    def _():
        hlast_ref[...] = h

    if final:
        for n in range(LRU_BLOCKS):
            cs = slice(n * LRU_BS, (n + 1) * LRU_BS)
            y = (hs_ref[:, cs] + hb_ref[:, cs]) * jax.nn.gelu(gate_ref[:, cs])
            out_ref[:, cs] = y.astype(out_ref.dtype)
    else:
        out_ref[...] = hs_ref[...]


def _lru_pass(p, cw, cb, gw, gb, lam, h0, hb, *, reverse):
    bsz, l, _ = p.shape
    tl = min(256, l)
    nt = l // tl
    final = hb is not None
    nh = l // HALO
    tpb = tl // HALO

    def tmap(i):
        return (nt - 1 - i) if reverse else i

    main = pl.BlockSpec((None, tl, D), lambda b, i: (b, tmap(i), 0))
    in_specs = [main,
                pl.BlockSpec((None, HALO, D), lambda b, i: (b, jnp.maximum(tmap(i) * tpb - 1, 0), 0)),
                pl.BlockSpec((None, HALO, D), lambda b, i: (b, jnp.minimum((tmap(i) + 1) * tpb, nh - 1), 0))]
    args = [p, p, p]
    if final:
        in_specs += [pl.BlockSpec((None, tl, D), lambda b, i: (b, tmap(i), 1)), main]
        args += [p, hb]
    in_specs += [pl.BlockSpec((CONV_W, D), lambda b, i: (0, 0)),
                 pl.BlockSpec((1, D), lambda b, i: (0, 0)),
                 pl.BlockSpec((LRU_BLOCKS, LRU_BS, 2 * LRU_BS), lambda b, i: (0, 0, 0)),
                 pl.BlockSpec((2, D), lambda b, i: (0, 0)),
                 pl.BlockSpec((1, D), lambda b, i: (0, 0)),
                 pl.BlockSpec((None, 1, D), lambda b, i: (b, 0, 0))]
    args += [cw, cb, gw, gb, lam, h0]
    out, hlast = pl.pallas_call(
        functools.partial(_lru_kernel, reverse=reverse, final=final, tl=tl, nt=nt),
        grid=(bsz, nt),
        in_specs=in_specs,
        out_specs=[main, pl.BlockSpec((None, 1, D), lambda b, i: (b, 0, 0))],
        out_shape=[jax.ShapeDtypeStruct((bsz, l, D), bf16 if final else f32),
                   jax.ShapeDtypeStruct((bsz, 1, D), f32)],
        scratch_shapes=[pltpu.VMEM((tl + 2 * HALO, D), f32), pltpu.VMEM((tl, D), f32),
                        pltpu.VMEM((tl, D), f32), pltpu.VMEM((tl, D), f32), pltpu.VMEM((1, D), f32)],
        compiler_params=_cparams(2),
        name="lru_fwd" if final else "lru_bwd",
    )(*args)
    return out, hlast


def _rwkv_feat_kernel(p_ref, prev_ref, next_ref, mu_ref, w0_ref, wup_ref, a0_ref, aup_ref, gup_ref,
                      kk_ref, ka_ref, rk_ref, ones_ref,
                      r_out, v_out, kkv_out, g_out, bonus_out, lw_out, key_out, b_out,
                      pe_ref, *, tl, nt):
    i = pl.program_id(1)
    zero = jnp.zeros((HALO, N_RWKV_PAD), f32)
    pe_ref[0:HALO, :] = jnp.where(i > 0, prev_ref[...], zero)
    pe_ref[HALO:HALO + tl, :] = p_ref[...]
    pe_ref[HALO + tl:2 * HALO + tl, :] = jnp.where(i < nt - 1, next_ref[...], zero)

    def shifted(c0, width):
        cs = slice(c0, c0 + width)
        p = pe_ref[HALO:HALO + tl, cs]
        nb = 0.5 * (pe_ref[HALO - 1:HALO - 1 + tl, cs] + pe_ref[HALO + 1:HALO + 1 + tl, cs])
        return p + mu_ref[:, cs] * (nb - p)

    win_w = jnp.tanh(shifted(RW_OFF_W, 256)).astype(bf16)
    win_a = shifted(RW_OFF_A, 256).astype(bf16)
    win_g = _sigmoid(shifted(RW_OFF_G, 256)).astype(bf16)
    ones_bd = ones_ref[...]

    for g in range(RW_NG):
        cs = slice(g * RW_GW, (g + 1) * RW_GW)
        r = shifted(g * RW_GW, RW_GW)
        k = shifted(D + g * RW_GW, RW_GW)
        v = shifted(2 * D + g * RW_GW, RW_GW)
        kf = k * kk_ref[:, cs]
        kk = kf * lax.rsqrt(_mm_x2(kf * kf, ones_bd) + 1e-12)
        ksum = None
        for d in range(2):
            wv = -_softplus(-(w0_ref[d:d + 1, cs] + jnp.dot(win_w, wup_ref[d, :, cs],
                                                              preferred_element_type=f32))) - 0.5
            lw_out[d, :, cs] = -jnp.exp(wv)
            a = _sigmoid(a0_ref[d:d + 1, cs] + jnp.dot(win_a, aup_ref[d, :, cs],
                                                       preferred_element_type=f32))
            key = k * (1.0 + (a - 1.0) * ka_ref[:, cs])
            key_out[d, :, cs] = key
            b_out[d, :, cs] = kk * a
            ksum = key if ksum is None else ksum + key
        r_out[:, cs] = r
        v_out[:, cs] = v
        kkv_out[:, cs] = kk
        bonus_out[:, cs] = _mm_x2(r * ksum * rk_ref[:, cs], ones_bd) * v
        g_out[:, cs] = jnp.dot(win_g, gup_ref[:, cs], preferred_element_type=f32)


def _rwkv_feat(p, prm):
    bsz, l, _ = p.shape
    tl = min(128, l)
    nt = l // tl
    nh = l // HALO
    tpb = tl // HALO
    full2 = lambda shape: pl.BlockSpec(shape, lambda b, i: (0,) * len(shape))
    tok = pl.BlockSpec((None, tl, D), lambda b, i: (b, i, 0))
    tok2 = pl.BlockSpec((2, None, tl, D), lambda b, i: (0, b, i, 0))
    sd = jax.ShapeDtypeStruct((bsz, l, D), f32)
    sd2 = jax.ShapeDtypeStruct((2, bsz, l, D), f32)
    return pl.pallas_call(
        functools.partial(_rwkv_feat_kernel, tl=tl, nt=nt),
        grid=(bsz, nt),
        in_specs=[pl.BlockSpec((None, tl, N_RWKV_PAD), lambda b, i: (b, i, 0)),
                  pl.BlockSpec((None, HALO, N_RWKV_PAD), lambda b, i: (b, jnp.maximum(i * tpb - 1, 0), 0)),
                  pl.BlockSpec((None, HALO, N_RWKV_PAD),
                               lambda b, i: (b, jnp.minimum((i + 1) * tpb, nh - 1), 0)),
                  full2((1, N_RWKV_PAD)), full2((2, D)), full2((2, 256, D)), full2((2, D)),
                  full2((2, 256, D)), full2((256, D)), full2((1, D)), full2((1, D)), full2((1, D)),
                  full2((RW_GW, RW_GW))],
        out_specs=[tok, tok, tok, tok, tok, tok2, tok2, tok2],
        out_shape=[sd, sd, sd, sd, sd, sd2, sd2, sd2],
        scratch_shapes=[pltpu.VMEM((tl + 2 * HALO, N_RWKV_PAD), f32)],
        compiler_params=_cparams(2),
        name="rwkv_feat",
    )(p, p, p, prm["mu"], prm["w0"], prm["wup"], prm["a0"], prm["aup"], prm["gup"],
      prm["k_k"], prm["k_a"], prm["r_k"], prm["ones_bd"])


def _bd_expand(y, lane_head):
    yb = y.astype(f32)
    return jnp.concatenate([jnp.where(lane_head == h, yb, 0.0).astype(bf16) for h in range(4)], axis=0)


def _rwkv_scan_kernel(r_ref, v_ref, kk_ref, lw_ref, key_ref, b_ref, s0_ref, y_ref, sfin_ref,
                      s_ref, *, nc):
    c = RW_CHUNK
    d = pl.program_id(0)
    i = pl.program_id(2)
    fwd = d == 0

    @pl.when(i == 0)
    def _():
        s_ref[...] = s0_ref[...]

    row = _iota((c, c), 0)
    col = _iota((c, c), 1)
    sgn = jnp.where(fwd, 1, -1)
    tri = jnp.where((col - row) * sgn <= 0, 1.0, 0.0).astype(bf16)
    t4 = _iota((c, 4 * c), 0)
    j4 = _iota((c, 4 * c), 1) % c
    mask_s = (j4 - t4) * sgn < 0
    mask_i = (j4 - t4) * sgn <= 0
    eye4 = jnp.where(j4 == t4, 1.0, 0.0)
    lane_head = _iota((c, RW_GW), 1) // HEAD
    bd_mask = (_iota((RW_GW, RW_GW), 0) // HEAD) == (_iota((RW_GW, RW_GW), 1) // HEAD)

    def mmbd(x, y):
        return jnp.dot(x.astype(bf16), _bd_expand(y, lane_head), preferred_element_type=f32)

    groups = range(RW_NG)
    sl = [slice(g * RW_GW, (g + 1) * RW_GW) for g in groups]
    lw = [lw_ref[:, sl[g]] for g in groups]
    cl = [_mm_e3(tri, lw[g]) for g in groups]
    tot = [jnp.sum(lw[g], axis=0, keepdims=True) for g in groups]
    v = [v_ref[:, sl[g]] for g in groups]
    ar, a_b, a_k, bk = [], [], [], []
    for g in groups:
        g_inv = jnp.exp(-cl[g])
        g_end = jnp.exp(tot[g] - cl[g])
        bv = b_ref[:, sl[g]]
        key = key_ref[:, sl[g]]
        at = -kk_ref[:, sl[g]] * jnp.exp(cl[g] - lw[g])
        rt = r_ref[:, sl[g]] * jnp.exp(cl[g])
        ar.append(jnp.concatenate([at, rt], axis=0).astype(bf16))
        a_b.append(lax.dot_general(ar[g], _bd_expand(bv * g_inv, lane_head), (((1,), (1,)), ((), ())),
                                   preferred_element_type=f32))
        a_k.append(lax.dot_general(ar[g], _bd_expand(key * g_inv, lane_head), (((1,), (1,)), ((), ())),
                                   preferred_element_type=f32))
        bk.append(jnp.concatenate([bv * g_end, key * g_end], axis=0).astype(bf16))
    n_ab = [jnp.where(mask_s, a_b[g][:c], 0.0) for g in groups]
    a_rb = [jnp.where(mask_i, a_b[g][c:], 0.0) for g in groups]
    a_kk = [jnp.concatenate([jnp.where(mask_s, a_k[g][:c], 0.0), jnp.where(mask_i, a_k[g][c:], 0.0)],
                            axis=0) for g in groups]
    x = [eye4 + n_ab[g] for g in groups]
    m = [mmbd(n_ab[g], n_ab[g]) for g in groups]
    lvl = 2
    while lvl < c:
        if lvl * 2 < c:
            xm = [mmbd(jnp.concatenate([x[g], m[g]], axis=0), m[g]) for g in groups]
            x = [x[g] + xm[g][:c] for g in groups]
            m = [xm[g][c:] for g in groups]
        else:
            x = [x[g] + mmbd(x[g], m[g]) for g in groups]
        lvl *= 2
    s = [s_ref[g] for g in groups]
    sa = [_mm_nt(ar[g], s[g]) for g in groups]
    av = [mmbd(a_kk[g], v[g]) for g in groups]
    u = [mmbd(x[g], sa[g][:c] + av[g][:c]) for g in groups]
    y = [sa[g][c:] + av[g][c:] + mmbd(a_rb[g], u[g]) for g in groups]
    upd = [_mm(jnp.concatenate([u[g], v[g]], axis=0).T, bk[g]) for g in groups]
    for g in groups:
        y_ref[:, sl[g]] = y[g]
        s_ref[g] = s[g] * jnp.exp(tot[g]) + jnp.where(bd_mask, upd[g], 0.0)

    @pl.when(i == nc - 1)
    def _():
        sfin_ref[...] = s_ref[...]


def _rwkv_scan(r, v, kk, lw, key, bvec, s0):
    bsz, l, _ = r.shape
    c = RW_CHUNK
    nc = l // c

    def cidx(d, i):
        return jnp.where(d == 0, i, nc - 1 - i)

    tok = pl.BlockSpec((None, c, D), lambda d, b, i: (b, cidx(d, i), 0))
    tok2 = pl.BlockSpec((None, None, c, D), lambda d, b, i: (d, b, cidx(d, i), 0))
    st = pl.BlockSpec((None, None, RW_NG, RW_GW, RW_GW), lambda d, b, i: (d, b, 0, 0, 0))
    return pl.pallas_call(
        functools.partial(_rwkv_scan_kernel, nc=nc),
        grid=(2, bsz, nc),
        in_specs=[tok, tok, tok, tok2, tok2, tok2, st],
        out_specs=[tok2, st],
        out_shape=[jax.ShapeDtypeStruct((2, bsz, l, D), f32),
                   jax.ShapeDtypeStruct((2, bsz, RW_NG, RW_GW, RW_GW), f32)],
        scratch_shapes=[pltpu.VMEM((RW_NG, RW_GW, RW_GW), f32)],
        compiler_params=_cparams(3),
        name="rwkv_scan",
    )(r, v, kk, lw, key, bvec, s0)


def _rwkv_out_kernel(y_ref, bonus_ref, g_ref, lnw_ref, lnb_ref, ones_ref, o_ref):
    ones_bd = ones_ref[...]
    for g in range(RW_NG):
        cs = slice(g * RW_GW, (g + 1) * RW_GW)
        y = y_ref[0, :, cs] + y_ref[1, :, cs]
        mean = _mm_x2(y, ones_bd) * (1.0 / HEAD)
        yc = y - mean
        var = _mm_x2(yc * yc, ones_bd) * (1.0 / HEAD)
        yn = yc * lax.rsqrt(var + GN_EPS) * lnw_ref[:, cs] + lnb_ref[:, cs]
        o_ref[:, cs] = ((yn + bonus_ref[:, cs]) * g_ref[:, cs]).astype(o_ref.dtype)


def _rwkv_out(y2, bonus, g, ln_w, ln_b, ones_bd):
    _, bsz, l, _ = y2.shape
    tl = min(256, l)
    tok = pl.BlockSpec((None, tl, D), lambda b, i: (b, i, 0))
    vec = pl.BlockSpec((1, D), lambda b, i: (0, 0))
    return pl.pallas_call(
        _rwkv_out_kernel,
        grid=(bsz, l // tl),
        in_specs=[pl.BlockSpec((2, None, tl, D), lambda b, i: (0, b, i, 0)), tok, tok, vec, vec,
                  pl.BlockSpec((RW_GW, RW_GW), lambda b, i: (0, 0))],
        out_specs=tok,
        out_shape=jax.ShapeDtypeStruct((bsz, l, D), bf16),
        compiler_params=_cparams(2),
        name="rwkv_out",
    )(y2, bonus, g, ln_w, ln_b, ones_bd)


def _ssd_prep_kernel(p_ref, prev_ref, next_ref, cw_ref, cb_ref, o_ref, pe_ref, *, tl, nt):
    i = pl.program_id(1)
    w = p_ref.shape[1]
    zero = jnp.zeros((HALO, w), f32)
    pe_ref[0:HALO, :] = jnp.where(i > 0, prev_ref[...], zero)
    pe_ref[HALO:HALO + tl, :] = p_ref[...]
    pe_ref[HALO + tl:2 * HALO + tl, :] = jnp.where(i < nt - 1, next_ref[...], zero)
    for n in range(w // 128):
        cs = slice(n * 128, (n + 1) * 128)
        xc = cb_ref[:, cs]
        for tap in range(CONV_W):
            r0 = HALO - CONV_LEFT + tap
            xc = xc + pe_ref[r0:r0 + tl, cs] * cw_ref[tap:tap + 1, cs]
        o_ref[:, cs] = _silu(xc)


def _ssd_prep(p, cw, cb):
    bsz, l, _ = p.shape
    tl = min(256, l)
    nt = l // tl
    nh = l // HALO
    tpb = tl // HALO
    wc = 1024
    c0 = D // wc
    return pl.pallas_call(
        functools.partial(_ssd_prep_kernel, tl=tl, nt=nt),
        grid=(bsz, nt, D_XBC // wc),
        in_specs=[pl.BlockSpec((None, tl, wc), lambda b, i, c: (b, i, c + c0)),
                  pl.BlockSpec((None, HALO, wc), lambda b, i, c: (b, jnp.maximum(i * tpb - 1, 0), c + c0)),
                  pl.BlockSpec((None, HALO, wc),
                               lambda b, i, c: (b, jnp.minimum((i + 1) * tpb, nh - 1), c + c0)),
                  pl.BlockSpec((CONV_W, wc), lambda b, i, c: (0, c)),
                  pl.BlockSpec((1, wc), lambda b, i, c: (0, c))],
        out_specs=pl.BlockSpec((None, tl, wc), lambda b, i, c: (b, i, c)),
        out_shape=jax.ShapeDtypeStruct((bsz, l, D_XBC), f32),
        scratch_shapes=[pltpu.VMEM((tl + 2 * HALO, wc), f32)],
        compiler_params=_cparams(3),
        name="ssd_prep",
    )(p, p, p, cw, cb)


def _ssd_kernel(xbc_ref, dtw_ref, dtb_ref, a_ref, dsk_ref, e_ref, eq_ref,
                h0_ref, y_ref, hfin_ref, cole_ref, cumt_ref, dtt_ref, h_ref, *, nc):
    q = SSM_Q
    d = pl.program_id(0)
    i = pl.program_id(2)
    fwd = d == 0

    @pl.when(i == 0)
    def _():
        h_ref[...] = h0_ref[...]

    row = _iota((q, q), 0)
    col = _iota((q, q), 1)
    sgn = jnp.where(fwd, 1, -1)
    low = (col - row) * sgn <= 0
    tri = jnp.where(low, 1.0, 0.0).astype(bf16)
    tri_t = jnp.where((row - col) * sgn <= 0, 1.0, 0.0).astype(bf16)

    dt = _softplus(dtw_ref[...] + dtb_ref[...])
    dta = dt * a_ref[...]
    cum = _mm_e3(tri, dta)
    tot = jnp.sum(dta, axis=0, keepdims=True)
    cumt_ref[...] = _mm_x3(dta.T, tri_t)
    dtt_ref[...] = dt.T
    e_d = e_ref[...]
    dec_e = _mm_x2(jnp.exp(cum), e_d)
    toend_e = _mm_x2(jnp.exp(tot - cum) * dt, e_d)
    tot_e = _mm_x2(jnp.broadcast_to(jnp.exp(tot), (8, 128)), e_d)[0:1]
    cole_ref[...] = _mm_x3(cum, eq_ref[...])
    lane = _iota((q, 128), 1)

    for g in range(SSM_G):
        bg = xbc_ref[:, D + g * SSM_N:D + (g + 1) * SSM_N]
        cg = xbc_ref[:, D + SSM_G * SSM_N + g * SSM_N:D + SSM_G * SSM_N + (g + 1) * SSM_N]
        cb = _mm_nt(cg, bg)
        gs = slice(g * 512, (g + 1) * 512)
        hg = h_ref[g]
        y_off = _mm(cg, hg) * dec_e[:, gs]
        for pr in range(4):
            ls = []
            for hh in range(2):
                h = g * 8 + pr * 2 + hh
                rowv = cumt_ref[pl.ds(d * 32 + h, 1), :]
                dtr = dtt_ref[pl.ds(d * 32 + h, 1), :]
                seg = cole_ref[:, h * q:(h + 1) * q] - rowv
                ls.append(jnp.where(low, jnp.exp(jnp.where(low, seg, 0.0)), 0.0) * cb * dtr)
            lp = jnp.concatenate(ls, axis=1)
            ps = slice(g * 512 + pr * 128, g * 512 + (pr + 1) * 128)
            xp = xbc_ref[:, ps]
            bd2 = jnp.concatenate([jnp.where(lane < SSM_P, xp, 0.0), jnp.where(lane >= SSM_P, xp, 0.0)],
                                  axis=0)
            yd = _mm(lp, bd2)
            y_ref[:, ps] = yd + y_off[:, pr * 128:(pr + 1) * 128] + dsk_ref[:, ps] * xp
        xs = xbc_ref[:, gs] * toend_e[:, gs]
        states = _mm(bg.T, xs)
        h_ref[g] = hg * tot_e[:, gs] + states

    @pl.when(i == nc - 1)
    def _():
        hfin_ref[...] = h_ref[...]


def _ssd(p, prm, h0):
    bsz, l, _ = p.shape
    q = SSM_Q
    nc = l // q
    xbc = _ssd_prep(p, prm["cw"], prm["cb"])

    def cidx(d, i):
        return jnp.where(d == 0, i, nc - 1 - i)

    full = lambda shape: pl.BlockSpec(shape, lambda d, b, i: (0,) * len(shape))
    st = pl.BlockSpec((None, None, SSM_G, SSM_N, 8 * SSM_P), lambda d, b, i: (d, b, 0, 0, 0))
    return pl.pallas_call(
        functools.partial(_ssd_kernel, nc=nc),
        grid=(2, bsz, nc),
        in_specs=[pl.BlockSpec((None, q, D_XBC), lambda d, b, i: (b, cidx(d, i), 0)),
                  pl.BlockSpec((None, q, 128), lambda d, b, i: (b, cidx(d, i), SS_OFF_DT // 128)),
                  full((1, 128)), full((1, 128)),
                  pl.BlockSpec((None, 1, D), lambda d, b, i: (d, 0, 0)),
                  pl.BlockSpec((None, 128, D), lambda d, b, i: (d, 0, 0)),
                  pl.BlockSpec((None, 128, SSM_HEADS * q), lambda d, b, i: (d, 0, 0)),
                  st],
        out_specs=[pl.BlockSpec((None, None, q, D), lambda d, b, i: (d, b, cidx(d, i), 0)), st],
        out_shape=[jax.ShapeDtypeStruct((2, bsz, l, D), f32),
                   jax.ShapeDtypeStruct((2, bsz, SSM_G, SSM_N, 8 * SSM_P), f32)],
        scratch_shapes=[pltpu.VMEM((q, SSM_HEADS * q), f32), pltpu.VMEM((128, q), f32),
                        pltpu.VMEM((128, q), f32), pltpu.VMEM((SSM_G, SSM_N, 8 * SSM_P), f32)],
        compiler_params=_cparams(3),
        name="ssd",
    )(xbc, p, prm["dtb"], prm["a"], prm["dsk"], prm["e"], prm["eq"], h0)


def _ssd_out_kernel(y_ref, z_ref, nw_ref, o_ref):
    gw = D // SSM_G
    for g in range(SSM_G):
        cs = slice(g * gw, (g + 1) * gw)
        yg = (y_ref[0, :, cs] + y_ref[1, :, cs]) * _silu(z_ref[:, cs])
        ms = jnp.mean(yg * yg, axis=-1, keepdims=True)
        o_ref[:, cs] = (yg * lax.rsqrt(ms + EPS) * nw_ref[:, cs]).astype(o_ref.dtype)


def _ssd_out(y2, p, norm_w):
    _, bsz, l, _ = y2.shape
    tl = min(256, l)
    tok = pl.BlockSpec((None, tl, D), lambda b, i: (b, i, 0))
    return pl.pallas_call(
        _ssd_out_kernel,
        grid=(bsz, l // tl),
        in_specs=[pl.BlockSpec((2, None, tl, D), lambda b, i: (0, b, i, 0)), tok,
                  pl.BlockSpec((1, D), lambda b, i: (0, 0))],
        out_specs=tok,
        out_shape=jax.ShapeDtypeStruct((bsz, l, D), bf16),
        compiler_params=_cparams(2),
        name="ssd_out",
    )(y2, p, norm_w)


def _merge_kernel(ya_ref, yb_ref, yc_ref, sa_ref, sb_ref, sc_ref, wa_ref, wb_ref, wc_ref, o_ref):
    acc = None
    for y_ref, s_ref, w_ref in ((ya_ref, sa_ref, wa_ref), (yb_ref, sb_ref, wb_ref), (yc_ref, sc_ref, wc_ref)):
        t = s_ref[...].astype(f32) * jnp.dot(y_ref[...], w_ref[...], preferred_element_type=f32)
        acc = t if acc is None else acc + t
    o_ref[...] = acc.astype(o_ref.dtype)


def _merge(ys, sig, ws):
    m = ys[0].shape[0]
    tm = min(512, m)
    tn = 512
    nj = D // tn
    ysp = pl.BlockSpec((tm, D), lambda i, j: (i, 0))
    wsp = pl.BlockSpec((D, tn), lambda i, j: (0, j))
    ssp = [pl.BlockSpec((tm, tn), functools.partial(lambda i, j, k: (i, k * nj + j), k=k)) for k in range(3)]
    return pl.pallas_call(
        _merge_kernel,
        grid=(m // tm, nj),
        in_specs=[ysp, ysp, ysp] + ssp + [wsp, wsp, wsp],
        out_specs=pl.BlockSpec((tm, tn), lambda i, j: (i, j)),
        out_shape=jax.ShapeDtypeStruct((m, D), bf16),
        compiler_params=_cparams(2),
        name="merge",
    )(ys[0], ys[1], ys[2], sig, sig, sig, ws[0], ws[1], ws[2])


def _wo_kernel(m_ref, w_ref, h_ref, g_ref, o_ref):
    o_ref[...] = h_ref[...] + g_ref[...] * jnp.dot(m_ref[...], w_ref[...], preferred_element_type=f32)


def _wo_residual(m, w_o, h, mod4, row_of_batch, k_gate, *, transposed):
    bsz, l, _ = h.shape
    if transposed:
        rows = l // GRID_W
        hv = h.reshape(bsz, rows, GRID_W * D)
        tl, nt = rows, GRID_W
        hspec = pl.BlockSpec((None, tl, D), lambda b, i: (b, 0, i))
        oshape = jax.ShapeDtypeStruct((bsz, rows, GRID_W * D), f32)
    else:
        tl = min(512, l)
        nt = l // tl
        hv = h
        hspec = pl.BlockSpec((None, tl, D), lambda b, i: (b, i, 0))
        oshape = jax.ShapeDtypeStruct((bsz, l, D), f32)
    out = pl.pallas_call(
        _wo_kernel,
        grid=(bsz, nt),
        in_specs=[pl.BlockSpec((None, tl, D), lambda b, i: (b, i, 0)),
                  pl.BlockSpec((D, D), lambda b, i: (0, 0)),
                  hspec,
                  pl.BlockSpec((None, None, 1, D), lambda b, i: (row_of_batch(b), k_gate, 0, 0))],
        out_specs=hspec,
        out_shape=oshape,
        compiler_params=_cparams(2),
        name="wo_residual",
    )(m, w_o, hv, mod4)
    return out.reshape(bsz, l, D)


def _router_kernel(x_ref, g_ref, sc_ref, sh_ref, r_ref, u_ref, gates_ref):
    x = x_ref[...]
    ms = jnp.mean(x * x, axis=-1, keepdims=True)
    u = (x * lax.rsqrt(ms + EPS) * g_ref[...]) * (1.0 + sc_ref[...]) + sh_ref[...]
    u_ref[...] = u.astype(u_ref.dtype)
    u0, u1, u2 = _split3(u)
    r0, r1, r2 = _split3(r_ref[...])
    dot = lambda a, b: jnp.dot(a, b, preferred_element_type=f32)
    logits = (dot(u0, r0) + (dot(u0, r1) + dot(u1, r0))
              + (dot(u1, r1) + dot(u0, r2) + dot(u2, r0)))
    lane = _iota(logits.shape, 1)
    neg = jnp.float32(-jnp.inf)
    lg = jnp.where(lane < N_EXPERTS, logits, neg)
    m1 = jnp.max(lg, axis=-1, keepdims=True)
    i1 = jnp.min(jnp.where(lg == m1, lane, 128), axis=-1, keepdims=True)
    lg2 = jnp.where(lane == i1, neg, lg)
    m2 = jnp.max(lg2, axis=-1, keepdims=True)
    i2 = jnp.min(jnp.where(lg2 == m2, lane, 128), axis=-1, keepdims=True)
    e2 = jnp.exp(m2 - m1)
    den = 1.0 + e2
    gates_ref[...] = jnp.where(lane == i1, 1.0 / den, 0.0) + jnp.where(lane == i2, e2 / den, 0.0)


def _norm_router(h, gain, mod4, row_of_batch, k_sc, k_sh, router_pad, out_dtype=bf16):
    bsz, l, _ = h.shape
    tl = min(512, l)
    tok = pl.BlockSpec((None, tl, D), lambda b, i: (b, i, 0))
    return pl.pallas_call(
        _router_kernel,
        grid=(bsz, l // tl),
        in_specs=[tok,
                  pl.BlockSpec((1, D), lambda b, i: (0, 0)),
                  pl.BlockSpec((None, None, 1, D), lambda b, i: (row_of_batch(b), k_sc, 0, 0)),
                  pl.BlockSpec((None, None, 1, D), lambda b, i: (row_of_batch(b), k_sh, 0, 0)),
                  pl.BlockSpec((D, 128), lambda b, i: (0, 0))],
        out_specs=[tok, pl.BlockSpec((None, tl, 128), lambda b, i: (b, i, 0))],
        out_shape=[jax.ShapeDtypeStruct((bsz, l, D), out_dtype), jax.ShapeDtypeStruct((bsz, l, 128), f32)],
        compiler_params=_cparams(2),
        name="norm_router",
    )(h, gain.reshape(1, D), mod4, mod4, router_pad)


def _up_kernel(*refs, gated):
    if gated:
        u_ref, w1_ref, w3_ref, gt_ref, o_ref, w1b_ref, w3b_ref = refs
    else:
        u_ref, w1_ref, w3_ref, o_ref, w1b_ref, w3b_ref = refs

    @pl.when(pl.program_id(2) == 0)
    def _():
        w1b_ref[...] = w1_ref[...].astype(bf16)
        w3b_ref[...] = w3_ref[...].astype(bf16)

    u = u_ref[...]
    a = jnp.dot(u, w1b_ref[...], preferred_element_type=f32)
    b = jnp.dot(u, w3b_ref[...], preferred_element_type=f32)
    act = _silu(a) * b
    if gated:
        e = pl.program_id(0)
        gt = gt_ref[...]
        lane = _iota(gt.shape, 1)
        act = act * jnp.sum(jnp.where(lane == e, gt, 0.0), axis=-1, keepdims=True)
    o_ref[...] = act.astype(o_ref.dtype)


def _swiglu_up(u, w1, w3, lead, gates=None):
    m = u.shape[0]
    _, ne, _, ff = w1.shape
    tm = min(1024, m)
    tn = 512
    nj = ff // tn
    gated = gates is not None
    wspec = pl.BlockSpec((None, None, D, tn), lambda e, j, i: (lead, e, 0, j))
    in_specs = [pl.BlockSpec((tm, D), lambda e, j, i: (i, 0)), wspec, wspec]
    args = [u, w1, w3]
    if gated:
        in_specs.append(pl.BlockSpec((tm, 128), lambda e, j, i: (i, 0)))
        args.append(gates)
    return pl.pallas_call(
        functools.partial(_up_kernel, gated=gated),
        grid=(ne, nj, m // tm),
        in_specs=in_specs,
        out_specs=pl.BlockSpec((tm, tn), lambda e, j, i: (i, e * nj + j)),
        out_shape=jax.ShapeDtypeStruct((m, ne * ff), bf16),
        scratch_shapes=[pltpu.VMEM((D, tn), bf16), pltpu.VMEM((D, tn), bf16)],
        compiler_params=_cparams(3),
        name="swiglu_up",
    )(*args)


def _down_kernel(a_ref, w_ref, h_ref, g_ref, o_ref, acc_ref, *, nk):
    k = pl.program_id(2)

    @pl.when(k == 0)
    def _():
        acc_ref[...] = jnp.zeros_like(acc_ref)

    acc_ref[...] += jnp.dot(a_ref[...], w_ref[...].astype(bf16), preferred_element_type=f32)

    @pl.when(k == nk - 1)
    def _():
        o_ref[...] = h_ref[...] + g_ref[...] * acc_ref[...]


def _down_residual(act, w2, h, mod4, row_of_batch, k_gate):
    bsz, l, kk = act.shape
    tl = min(512, l)
    tk = 1024
    nk = kk // tk
    return pl.pallas_call(
        functools.partial(_down_kernel, nk=nk),
        grid=(bsz, l // tl, nk),
        in_specs=[pl.BlockSpec((None, tl, tk), lambda b, i, k: (b, i, k)),
                  pl.BlockSpec((tk, D), lambda b, i, k: (k, 0)),
                  pl.BlockSpec((None, tl, D), lambda b, i, k: (b, i, 0)),
                  pl.BlockSpec((None, None, 1, D), lambda b, i, k: (row_of_batch(b), k_gate, 0, 0))],
        out_specs=pl.BlockSpec((None, tl, D), lambda b, i, k: (b, i, 0)),
        out_shape=jax.ShapeDtypeStruct((bsz, l, D), f32),
        scratch_shapes=[pltpu.VMEM((tl, D), f32)],
        compiler_params=_cparams(3),
        name="down_residual",
    )(act, w2, h, mod4)


MOE_TM = 512
MOE_TT = 512
NACT_LANE = 127


def _moe_count_kernel(gates_ref, pre_ref, cnt_ref, carry_ref):
    i = pl.program_id(0)

    @pl.when(i == 0)
    def _():
        carry_ref[...] = jnp.zeros_like(carry_ref)

    tt = gates_ref.shape[0]
    a = jnp.where(gates_ref[...] > 0.0, 1.0, 0.0)
    strict = jnp.where(_iota((tt, tt), 1) < _iota((tt, tt), 0), 1.0, 0.0).astype(bf16)
    carry = carry_ref[0:1, :]
    pre_ref[...] = jnp.dot(strict, a.astype(bf16), preferred_element_type=f32) + carry
    carry_ref[...] = jnp.broadcast_to(carry + jnp.sum(a, axis=0, keepdims=True), carry_ref.shape)
    cnt_ref[...] = carry_ref[...]


def _moe_place_kernel(gates_ref, pre_ref, cnt_ref, pos_ref, gv_ref, tile_ref, *, n_tiles):
    tt = gates_ref.shape[0]
    tm = float(MOE_TM)
    lane1 = _iota((1, 128), 1)
    cnt = cnt_ref[0:1, :]
    gsz = jnp.floor((cnt + (tm - 1.0)) * (1.0 / tm)) * tm
    upper = jnp.where(_iota((128, 128), 0) < _iota((128, 128), 1), 1.0, 0.0).astype(bf16)
    base = _mm_x3(jnp.broadcast_to(gsz, (8, 128)), upper)[0:1]
    gates = gates_ref[...]
    act = gates > 0.0
    pos = base + pre_ref[...]
    big = jnp.float32(1e9)
    p_lo = jnp.min(jnp.where(act, pos, big), axis=-1, keepdims=True)
    p_hi = jnp.max(jnp.where(act, pos, -1.0), axis=-1, keepdims=True)
    g_lo = jnp.sum(jnp.where(act & (pos == p_lo), gates, 0.0), axis=-1, keepdims=True)
    g_hi = jnp.where(p_hi != p_lo,
                     jnp.sum(jnp.where(act & (pos == p_hi), gates, 0.0), axis=-1, keepdims=True), 0.0)
    lane = _iota((tt, 128), 1)
    posf = jnp.where(lane == 0, p_lo, 0.0) + jnp.where(lane == 1, p_hi, 0.0)
    pos_ref[...] = posf.T[0:8].astype(jnp.int32)
    gv_ref[...] = jnp.where(lane == 0, g_lo, 0.0) + jnp.where(lane == 1, g_hi, 0.0)
    end = base + gsz
    start_j = lane1.astype(f32) * tm
    texp = jnp.zeros((1, 128), f32)
    for e in range(N_EXPERTS - 1):
        end_e = jnp.sum(jnp.where(lane1 == e, end, 0.0), axis=-1, keepdims=True)
        texp = texp + jnp.where(start_j >= end_e, 1.0, 0.0)
    total = jnp.sum(jnp.where(lane1 == N_EXPERTS - 1, end, 0.0), axis=-1, keepdims=True)
    table = jnp.where(lane1 == NACT_LANE, total * (1.0 / tm), texp)
    tile_ref[...] = jnp.broadcast_to(table, (8, 128)).astype(jnp.int32)


def _moe_scatter_kernel(pos_ref, v_ref, xs_in_ref, xs_ref, sem, *, dump_base):
    del xs_in_ref
    tt = v_ref.shape[0]

    def row_copy(r, dst_row):
        return pltpu.make_async_copy(v_ref.at[pl.ds(r, 1)], xs_ref.at[pl.ds(dst_row, 1)], sem)

    def body(r, carry):
        p0 = pos_ref[0, r]
        p1 = pos_ref[1, r]
        p1 = jnp.where(p1 == p0, dump_base + r, p1)
        row_copy(r, p0).start()
        row_copy(r, p1).start()
        return carry

    lax.fori_loop(0, tt, body, 0, unroll=8)
    for _ in range(2):
        pltpu.make_async_copy(v_ref, xs_ref.at[pl.ds(0, tt)], sem).wait()


def _moe_gup_kernel(tile_ref, x_ref, w1_ref, w3_ref, o_ref):
    t = pl.program_id(1)

    @pl.when(t < tile_ref[NACT_LANE])
    def _():
        x = x_ref[...].astype(bf16)
        a = jnp.dot(x, w1_ref[...].astype(bf16), preferred_element_type=f32)
        b = jnp.dot(x, w3_ref[...].astype(bf16), preferred_element_type=f32)
        o_ref[...] = (_silu(a) * b).astype(o_ref.dtype)

    @pl.when(t >= tile_ref[NACT_LANE])
    def _():
        o_ref[...] = jnp.zeros_like(o_ref)


def _moe_gdown_kernel(tile_ref, a_ref, w_ref, o_ref):
    t = pl.program_id(1)

    @pl.when(t < tile_ref[NACT_LANE])
    def _():
        o_ref[...] = jnp.dot(a_ref[...], w_ref[...].astype(bf16), preferred_element_type=f32)

    @pl.when(t >= tile_ref[NACT_LANE])
    def _():
        o_ref[...] = jnp.zeros_like(o_ref)


def _moe_combine_kernel(pos_ref, gv_ref, h_ref, gf_ref, ys_ref, o_ref, buf0, buf1, sem):
    tt = h_ref.shape[0]

    def row_copy(src_row, buf, r):
        return pltpu.make_async_copy(ys_ref.at[pl.ds(src_row, 1)], buf.at[pl.ds(r, 1)], sem)

    def body(r, carry):
        row_copy(pos_ref[0, r], buf0, r).start()
        row_copy(pos_ref[1, r], buf1, r).start()
        return carry

    lax.fori_loop(0, tt, body, 0, unroll=8)
    for buf in (buf0, buf1):
        pltpu.make_async_copy(ys_ref.at[pl.ds(0, tt)], buf, sem).wait()
    gv = gv_ref[...]
    lane = _iota(gv.shape, 1)
    g0 = jnp.sum(jnp.where(lane == 0, gv, 0.0), axis=-1, keepdims=True)
    g1 = jnp.sum(jnp.where(lane == 1, gv, 0.0), axis=-1, keepdims=True)
    o_ref[...] = h_ref[...] + gf_ref[...] * (g0 * buf0[...] + g1 * buf1[...])


def _moe_sparse(v32, gates, h, w1, w3, w2, lead, mod4, row_of_batch, k_gate):
    bsz, l, _ = h.shape
    n_tok = bsz * l
    tt = min(MOE_TT, n_tok)
    n_tt = n_tok // tt
    n_tiles = (2 * n_tok) // MOE_TM + N_EXPERTS
    rows = n_tiles * MOE_TM
    ff = w1.shape[3]
    vf = v32.reshape(n_tok, D)
    gf = gates.reshape(n_tok, 128)

    pre, cnt = pl.pallas_call(
        _moe_count_kernel,
        grid=(n_tt,),
        in_specs=[pl.BlockSpec((tt, 128), lambda i: (i, 0))],
        out_specs=[pl.BlockSpec((tt, 128), lambda i: (i, 0)), pl.BlockSpec((8, 128), lambda i: (0, 0))],
        out_shape=[jax.ShapeDtypeStruct((n_tok, 128), f32), jax.ShapeDtypeStruct((8, 128), f32)],
        scratch_shapes=[pltpu.VMEM((8, 128), f32)],
        compiler_params=_cparams(1),
        name="moe_count",
    )(gf)

    pos, gv, table = pl.pallas_call(
        functools.partial(_moe_place_kernel, n_tiles=n_tiles),
        grid=(n_tt,),
        in_specs=[pl.BlockSpec((tt, 128), lambda i: (i, 0)), pl.BlockSpec((tt, 128), lambda i: (i, 0)),
                  pl.BlockSpec((8, 128), lambda i: (0, 0))],
        out_specs=[pl.BlockSpec((8, tt), lambda i: (0, i)), pl.BlockSpec((tt, 128), lambda i: (i, 0)),
                   pl.BlockSpec((8, 128), lambda i: (0, 0))],
        out_shape=[jax.ShapeDtypeStruct((8, n_tok), jnp.int32), jax.ShapeDtypeStruct((n_tok, 128), f32),
                   jax.ShapeDtypeStruct((8, 128), jnp.int32)],
        compiler_params=_cparams(1),
        name="moe_place",
    )(gf, pre, cnt)
    tile_tab = table[0]

    smem_pos = pl.BlockSpec((8, tt), lambda i: (0, i), memory_space=pltpu.SMEM)
    xs = pl.pallas_call(
        functools.partial(_moe_scatter_kernel, dump_base=rows),
        grid=(n_tt,),
        in_specs=[smem_pos, pl.BlockSpec((tt, D), lambda i: (i, 0)), pl.BlockSpec(memory_space=pl.ANY)],
        out_specs=pl.BlockSpec(memory_space=pl.ANY),
        out_shape=jax.ShapeDtypeStruct((rows + tt, D), f32),
        scratch_shapes=[pltpu.SemaphoreType.DMA],
        input_output_aliases={2: 0},
        compiler_params=_cparams(1),
        name="moe_scatter",
    )(pos, vf, jnp.zeros((rows + tt, D), f32))

    assert n_tiles < NACT_LANE
    tn = 512
    nj = ff // tn
    last_tile = lambda tab: jnp.maximum(tab[NACT_LANE] - 1, 0)
    act = pl.pallas_call(
        _moe_gup_kernel,
        grid_spec=pltpu.PrefetchScalarGridSpec(
            num_scalar_prefetch=1,
            grid=(nj, n_tiles),
            in_specs=[pl.BlockSpec((MOE_TM, D), lambda j, t, tab: (jnp.minimum(t, last_tile(tab)), 0)),
                      pl.BlockSpec((None, None, D, tn),
                                   lambda j, t, tab: (lead, tab[jnp.minimum(t, last_tile(tab))], 0, j)),
                      pl.BlockSpec((None, None, D, tn),
                                   lambda j, t, tab: (lead, tab[jnp.minimum(t, last_tile(tab))], 0, j))],
            out_specs=pl.BlockSpec((MOE_TM, tn), lambda j, t, tab: (t, j))),
        out_shape=jax.ShapeDtypeStruct((rows, ff), bf16),
        compiler_params=_cparams(2),
        name="moe_up",
    )(tile_tab, xs, w1, w3)

    tnd = 1024
    ys = pl.pallas_call(
        _moe_gdown_kernel,
        grid_spec=pltpu.PrefetchScalarGridSpec(
            num_scalar_prefetch=1,
            grid=(D // tnd, n_tiles),
            in_specs=[pl.BlockSpec((MOE_TM, ff), lambda j, t, tab: (jnp.minimum(t, last_tile(tab)), 0)),
                      pl.BlockSpec((None, None, ff, tnd),
                                   lambda j, t, tab: (lead, tab[jnp.minimum(t, last_tile(tab))], 0, j))],
            out_specs=pl.BlockSpec((MOE_TM, tnd), lambda j, t, tab: (t, j))),
        out_shape=jax.ShapeDtypeStruct((rows, D), f32),
        compiler_params=_cparams(2),
        name="moe_down",
    )(tile_tab, act, w2)

    tpb = l // tt if l >= tt else 1
    out = pl.pallas_call(
        _moe_combine_kernel,
        grid=(n_tt,),
        in_specs=[smem_pos, pl.BlockSpec((tt, 128), lambda i: (i, 0)), pl.BlockSpec((tt, D), lambda i: (i, 0)),
                  pl.BlockSpec((None, None, 1, D), lambda i: (row_of_batch(i // tpb), k_gate, 0, 0)),
                  pl.BlockSpec(memory_space=pl.ANY)],
        out_specs=pl.BlockSpec((tt, D), lambda i: (i, 0)),
        out_shape=jax.ShapeDtypeStruct((n_tok, D), f32),
        scratch_shapes=[pltpu.VMEM((tt, D), f32), pltpu.VMEM((tt, D), f32), pltpu.SemaphoreType.DMA],
        compiler_params=_cparams(1),
        name="moe_combine",
    )(pos, gv, h.reshape(n_tok, D), mod4, ys)
    return out.reshape(bsz, l, D)


def _block_ones(n, bs):
    idx = jnp.arange(n) // bs
    return (idx[:, None] == idx[None, :]).astype(bf16)


def _pad_cols(w, n):
    return jnp.pad(w, ((0, 0), (0, n - w.shape[1])))


def _rwkv_params(li, mu, w0, w_up, a0, a_up, g_up, k_k, k_a, r_k, ln_w, ln_b):
    wup = jnp.zeros((2, 256, D), f32)
    aup = jnp.zeros((2, 256, D), f32)
    for d in range(2):
        wup = wup.at[d, d * W_LORA:(d + 1) * W_LORA].set(w_up[li, d])
        o = (RW_OFF_G - 2 * A_LORA) - RW_OFF_A + d * A_LORA
        aup = aup.at[d, o:o + A_LORA].set(a_up[li, d])
    return {
        "mu": jnp.pad(mu[li], (0, N_RWKV_PAD - N_RWKV)).reshape(1, N_RWKV_PAD),
        "w0": w0[li], "wup": wup.astype(bf16), "a0": a0[li], "aup": aup.astype(bf16),
        "gup": g_up[li].astype(bf16), "k_k": k_k[li].reshape(1, D), "k_a": k_a[li].reshape(1, D),
        "r_k": r_k[li].reshape(1, D), "ln_w": ln_w[li].reshape(1, D), "ln_b": ln_b[li].reshape(1, D),
        "ones_bd": _block_ones(RW_GW, HEAD),
    }


def _ssd_params(li, conv_w, conv_b, a_log, dt_bias, d_skip):
    a = -jnp.exp(a_log[li].astype(f32))
    hid = jnp.arange(D) // SSM_P
    hq = jnp.arange(SSM_HEADS * SSM_Q) // SSM_Q
    sel = jnp.arange(128)[:, None]
    e = jnp.stack([(sel == hid[None, :] + d * SSM_HEADS) for d in range(2)]).astype(bf16)
    eq = jnp.stack([(sel == hq[None, :] + d * SSM_HEADS) for d in range(2)]).astype(bf16)
    return {
        "cw": conv_w[li], "cb": conv_b[li].reshape(1, D_XBC),
        "dtb": jnp.pad(dt_bias[li].reshape(1, 2 * SSM_HEADS), ((0, 0), (0, 64))),
        "a": jnp.pad(a.reshape(1, 2 * SSM_HEADS), ((0, 0), (0, 64))),
        "dsk": jnp.stack([jnp.repeat(d_skip[li], SSM_P), jnp.zeros((D,), f32)]).reshape(2, 1, D),
        "e": e, "eq": eq,
    }


def kernel(x, c, ctx, c_ctx, ada_w, ada_b, norm_mix, norm_ffn, norm_final, w_in, lru_conv_w, lru_conv_b, lru_gate_w, lru_gate_b, lru_lambda, rwkv_mu, rwkv_w0, rwkv_w_up, rwkv_a0, rwkv_a_up, rwkv_g_up, rwkv_k_k, rwkv_k_a, rwkv_r_k, rwkv_ln_w, rwkv_ln_b, ssm_conv_w, ssm_conv_b, ssm_a_log, ssm_dt_bias, ssm_d, ssm_norm_w, w_out_lru, w_out_rwkv, w_out_ssm, w_o, ffn_w1, ffn_w3, ffn_w2, moe_router, moe_w1, moe_w3, moe_w2):
    bsz, l, _ = x.shape
    depth = ada_w.shape[0]
    off_lru = 3 * D
    off_rwkv = off_lru + 2 * D
    off_ssm = off_rwkv + N_RWKV
    cvec = jnp.zeros((8, D), f32).at[:bsz].set(c).at[bsz].set(c_ctx)
    lat_row = lambda b: b
    ctx_row = lambda b: bsz

    h_lat, h_ctx = x, ctx
    for li in range(depth):
        last = li == depth - 1
        odd = li % 2 == 1
        mod4 = _ada(cvec, ada_w, ada_b, li).reshape(8, 6, 1, D)

        w_ss = _pad_cols(w_in[li, :, off_ssm:], N_SSM_PAD)[None]
        lru_gw = [jnp.concatenate([lru_gate_w[li, d, 0], lru_gate_w[li, d, 1]], axis=-1).astype(bf16)
                  for d in range(2)]
        rp = _rwkv_params(li, rwkv_mu, rwkv_w0, rwkv_w_up, rwkv_a0, rwkv_a_up, rwkv_g_up, rwkv_k_k,
                          rwkv_k_a, rwkv_r_k, rwkv_ln_w, rwkv_ln_b)
        sp = _ssd_params(li, ssm_conv_w, ssm_conv_b, ssm_a_log, ssm_dt_bias, ssm_d)
        w_outs = [w_out_lru[li].astype(bf16), w_out_rwkv[li].astype(bf16), w_out_ssm[li].astype(bf16)]
        w_o_b = w_o[li].astype(bf16)

        def token_mix(u, states, need_out):
            lx = u.shape[1]
            um = u.reshape(bsz * lx, D)
            p_lru = _proj(um, w_in, li, tn=1024, col0=off_lru, n=2 * D).reshape(bsz, lx, 2 * D)
            p_rw = _proj(um, w_in, li, tn=512, col0=off_rwkv, n=N_RWKV_PAD).reshape(bsz, lx, N_RWKV_PAD)
            p_ss = _proj(um, w_ss, 0, tn=768).reshape(bsz, lx, N_SSM_PAD)
            lru_s, rw_s, ss_s = states
            cw, cb = lru_conv_w[li], lru_conv_b[li].reshape(1, D)
            hb, hl_b = _lru_pass(p_lru, cw, cb, lru_gw[1], lru_gate_b[li, 1], lru_lambda[li, 1].reshape(1, D),
                                 lru_s[1], None, reverse=True)
            ya, hl_f = _lru_pass(p_lru, cw, cb, lru_gw[0], lru_gate_b[li, 0], lru_lambda[li, 0].reshape(1, D),
                                 lru_s[0], hb, reverse=False)
            r, v, kk, g, bonus, lw, key, bvec = _rwkv_feat(p_rw, rp)
            y2, rw_fin = _rwkv_scan(r, v, kk, lw, key, bvec, rw_s)
            ys2, ss_fin = _ssd(p_ss, sp, ss_s)
            new_states = ((hl_f, hl_b), rw_fin, ss_fin)
            if not need_out:
                return None, new_states
            yb = _rwkv_out(y2, bonus, g, rp["ln_w"], rp["ln_b"], rp["ones_bd"])
            yc = _ssd_out(ys2, p_ss, ssm_norm_w[li].reshape(1, D))
            sig = _proj(um, w_in, li, tn=1024, col0=0, n=off_lru, act="sigmoid", out_dtype=bf16)
            m = _merge([ya.reshape(-1, D), yb.reshape(-1, D), yc.reshape(-1, D)], sig, w_outs)
            return m.reshape(bsz, lx, D), new_states

        zero_states = ((jnp.zeros((bsz, 1, D), f32), jnp.zeros((bsz, 1, D), f32)),
                       jnp.zeros((2, bsz, RW_NG, RW_GW, RW_GW), f32),
                       jnp.zeros((2, bsz, SSM_G, SSM_N, 8 * SSM_P), f32))

        u_ctx = _norm(h_ctx, norm_mix[li], mod4, ctx_row, 1, 0)
        u_lat = _norm(h_lat, norm_mix[li], mod4, lat_row, 1, 0, transposed=odd)
        m_ctx, ctx_states = token_mix(u_ctx, zero_states, not last)
        m_lat, _ = token_mix(u_lat, ctx_states, True)
        h_lat = _wo_residual(m_lat, w_o_b, h_lat, mod4, lat_row, 2, transposed=odd)
        if not last:
            h_ctx = _wo_residual(m_ctx, w_o_b, h_ctx, mod4, ctx_row, 2, transposed=False)

        j = li // 2
        streams = [(h_lat, lat_row)] + ([] if last else [(h_ctx, ctx_row)])
        outs = []
        for h, row_fn in streams:
            lx = h.shape[1]
            if not odd:
                v = _norm(h, norm_ffn[li], mod4, row_fn, 4, 3)
                act = _swiglu_up(v.reshape(bsz * lx, D), ffn_w1[:, None], ffn_w3[:, None], j)
                w2 = ffn_w2[j].astype(bf16)
            elif row_fn is lat_row:
                router_pad = _pad_cols(moe_router[j], 128)
                v32, gates = _norm_router(h, norm_ffn[li], mod4, row_fn, 4, 3, router_pad, out_dtype=f32)
                outs.append(_moe_sparse(v32, gates, h, moe_w1, moe_w3, moe_w2, j, mod4, row_fn, 5))
                continue
            else:
                router_pad = _pad_cols(moe_router[j], 128)
                v, gates = _norm_router(h, norm_ffn[li], mod4, row_fn, 4, 3, router_pad)
                act = _swiglu_up(v.reshape(bsz * lx, D), moe_w1, moe_w3, j, gates.reshape(bsz * lx, 128))
                w2 = moe_w2[j].astype(bf16).reshape(N_EXPERTS * D_FF_EXPERT, D)
            outs.append(_down_residual(act.reshape(bsz, lx, -1), w2, h, mod4, row_fn, 5))
        h_lat = outs[0]
        if not last:
            h_ctx = outs[1]
    return _final_norm(h_lat, norm_final)
```

```python
import functools
import math

import jax
import jax.numpy as jnp
from jax import lax
from jax.experimental import pallas as pl
from jax.experimental.pallas import tpu as pltpu

f32 = jnp.float32
bf16 = jnp.bfloat16

D = 2048
GRID_W = 64
EPS = 1e-6
CONV_W = 4
CONV_LEFT = 2
HALO = 8

LRU_BLOCKS = 16
LRU_BS = D // LRU_BLOCKS
LRU_C = 8.0

HEADS = 32
HEAD = 64
W_LORA = 96
A_LORA = 96
G_LORA = 256
GN_EPS = 64e-5
N_RWKV = 3 * D + 2 * W_LORA + 2 * A_LORA + G_LORA
N_RWKV_PAD = 7168
RW_OFF_W = 3 * D
RW_OFF_A = 3 * D + 128
RW_OFF_G = 3 * D + 2 * W_LORA + 2 * A_LORA
RW_CHUNK = 64
RW_GW = 256
RW_NG = D // RW_GW

SSM_HEADS = 32
SSM_P = 64
SSM_N = 128
SSM_G = 4
SSM_Q = 128
D_XBC = D + 2 * SSM_G * SSM_N
N_SSM = D + D_XBC + 2 * SSM_HEADS
N_SSM_PAD = 5376
SS_OFF_DT = D + D_XBC

D_FF = 3 * D
N_EXPERTS = 8
D_FF_EXPERT = D_FF // 2

VMEM_LIMIT = 56 * 1024 * 1024


def _cparams(n_axes, vmem=VMEM_LIMIT):
    return pltpu.CompilerParams(dimension_semantics=("arbitrary",) * n_axes, vmem_limit_bytes=vmem)


def _mm(a, b):
    return jnp.dot(a.astype(bf16), b.astype(bf16), preferred_element_type=f32)


def _mm_nt(a, b):
    return lax.dot_general(a.astype(bf16), b.astype(bf16), (((1,), (1,)), ((), ())),
                           preferred_element_type=f32)


def _split3(x):
    x0 = x.astype(bf16)
    r = x - x0.astype(f32)
    x1 = r.astype(bf16)
    r = r - x1.astype(f32)
    return x0, x1, r.astype(bf16)


def _mm_x3(x, e):
    x0, x1, x2 = _split3(x)
    return (jnp.dot(x0, e, preferred_element_type=f32) + jnp.dot(x1, e, preferred_element_type=f32)
            + jnp.dot(x2, e, preferred_element_type=f32))


def _mm_x2(x, e):
    x0 = x.astype(bf16)
    x1 = (x - x0.astype(f32)).astype(bf16)
    return jnp.dot(x0, e, preferred_element_type=f32) + jnp.dot(x1, e, preferred_element_type=f32)


def _mm_e3(e, x):
    x0, x1, x2 = _split3(x)
    return (jnp.dot(e, x0, preferred_element_type=f32) + jnp.dot(e, x1, preferred_element_type=f32)
            + jnp.dot(e, x2, preferred_element_type=f32))


def _softplus(x):
    return jnp.maximum(x, 0.0) + jnp.log1p(jnp.exp(-jnp.abs(x)))


def _sigmoid(x):
    return 0.5 * jnp.tanh(0.5 * x) + 0.5


def _silu(x):
    return x * _sigmoid(x)


def _iota(shape, dim):
    return lax.broadcasted_iota(jnp.int32, shape, dim)


def _ada_kernel(c_ref, w_ref, b_ref, o_ref):
    cv = c_ref[...]
    o_ref[...] = _mm(_silu(cv), w_ref[...]) + b_ref[...]


def _ada(cvec, w, b, li):
    depth, _, n = w.shape
    tn = 1536
    return pl.pallas_call(
        _ada_kernel,
        grid=(n // tn,),
        in_specs=[pl.BlockSpec((8, D), lambda j: (0, 0)),
                  pl.BlockSpec((None, D, tn), lambda j: (li, 0, j)),
                  pl.BlockSpec((None, 1, tn), lambda j: (li, 0, j))],
        out_specs=pl.BlockSpec((8, tn), lambda j: (0, j)),
        out_shape=jax.ShapeDtypeStruct((8, n), f32),
        compiler_params=_cparams(1),
        name="ada",
    )(cvec, w, b.reshape(depth, 1, n))


def _norm_kernel(x_ref, g_ref, sc_ref, sh_ref, o_ref):
    x = x_ref[...]
    ms = jnp.mean(x * x, axis=-1, keepdims=True)
    xn = x * lax.rsqrt(ms + EPS) * g_ref[...]
    o_ref[...] = (xn * (1.0 + sc_ref[...]) + sh_ref[...]).astype(o_ref.dtype)


def _norm(h, gain, mod4, row_of_batch, k_sc, k_sh, *, transposed=False, out_dtype=bf16):
    bsz, l, _ = h.shape
    if transposed:
        rows = l // GRID_W
        hin = h.reshape(bsz, rows, GRID_W * D)
        tl = rows
        nt = GRID_W
        in_spec = pl.BlockSpec((None, tl, D), lambda b, i: (b, 0, i))
    else:
        tl = min(512, l)
        nt = l // tl
        hin = h
        in_spec = pl.BlockSpec((None, tl, D), lambda b, i: (b, i, 0))
    return pl.pallas_call(
        _norm_kernel,
        grid=(bsz, nt),
        in_specs=[in_spec,
                  pl.BlockSpec((1, D), lambda b, i: (0, 0)),
                  pl.BlockSpec((None, None, 1, D), lambda b, i: (row_of_batch(b), k_sc, 0, 0)),
                  pl.BlockSpec((None, None, 1, D), lambda b, i: (row_of_batch(b), k_sh, 0, 0))],
        out_specs=pl.BlockSpec((None, tl, D), lambda b, i: (b, i, 0)),
        out_shape=jax.ShapeDtypeStruct((bsz, l, D), out_dtype),
        compiler_params=_cparams(2),
        name="norm",
    )(hin, gain.reshape(1, D), mod4, mod4)


def _plain_norm_kernel(x_ref, g_ref, o_ref):
    x = x_ref[...]
    ms = jnp.mean(x * x, axis=-1, keepdims=True)
    o_ref[...] = (x * lax.rsqrt(ms + EPS) * g_ref[...]).astype(o_ref.dtype)


def _final_norm(h, gain):
    bsz, l, _ = h.shape
    tl = min(512, l)
    return pl.pallas_call(
        _plain_norm_kernel,
        grid=(bsz, l // tl),
        in_specs=[pl.BlockSpec((None, tl, D), lambda b, i: (b, i, 0)),
                  pl.BlockSpec((1, D), lambda b, i: (0, 0))],
        out_specs=pl.BlockSpec((None, tl, D), lambda b, i: (b, i, 0)),
        out_shape=jax.ShapeDtypeStruct((bsz, l, D), f32),
        compiler_params=_cparams(2),
        name="final_norm",
    )(h, gain.reshape(1, D))


def _proj_kernel(u_ref, w_ref, o_ref, wb_ref, *, act):
    @pl.when(pl.program_id(1) == 0)
    def _():
        wb_ref[...] = w_ref[...].astype(bf16)

    acc = jnp.dot(u_ref[...], wb_ref[...], preferred_element_type=f32)
    if act == "sigmoid":
        acc = _sigmoid(acc)
    o_ref[...] = acc.astype(o_ref.dtype)


def _proj(u, w, li, *, tn, col0=0, n=None, act=None, out_dtype=f32):
    m, k = u.shape
    n = w.shape[2] if n is None else n
    assert col0 % tn == 0 and n % tn == 0 and col0 + n <= w.shape[2]
    off = col0 // tn
    tm = min(1024, m)
    return pl.pallas_call(
        functools.partial(_proj_kernel, act=act),
        grid=(n // tn, m // tm),
        in_specs=[pl.BlockSpec((tm, k), lambda j, i: (i, 0)),
                  pl.BlockSpec((None, k, tn), lambda j, i: (li, 0, j + off))],
        out_specs=pl.BlockSpec((tm, tn), lambda j, i: (i, j)),
        out_shape=jax.ShapeDtypeStruct((m, n), out_dtype),
        scratch_shapes=[pltpu.VMEM((k, tn), bf16)],
        compiler_params=_cparams(2),
        name="proj",
    )(u, w)


def _lru_kernel(*refs, reverse, final, tl, nt):
    if final:
        (x_ref, prev_ref, next_ref, gate_ref, hb_ref, cw_ref, cb_ref, gw_ref, gb_ref, lam_ref, h0_ref,
         out_ref, hlast_ref, xe_ref, a_ref, bx_ref, hs_ref, h_ref) = refs
    else:
        (x_ref, prev_ref, next_ref, cw_ref, cb_ref, gw_ref, gb_ref, lam_ref, h0_ref,
         out_ref, hlast_ref, xe_ref, a_ref, bx_ref, hs_ref, h_ref) = refs
    i = pl.program_id(1)
    t = (nt - 1 - i) if reverse else i

    @pl.when(i == 0)
    def _():
        h_ref[...] = h0_ref[...]

    zero = jnp.zeros((HALO, D), f32)
    xe_ref[0:HALO, :] = jnp.where(t > 0, prev_ref[...], zero)
    xe_ref[HALO:HALO + tl, :] = x_ref[...]
    xe_ref[HALO + tl:2 * HALO + tl, :] = jnp.where(t < nt - 1, next_ref[...], zero)

    for n in range(LRU_BLOCKS):
        cs = slice(n * LRU_BS, (n + 1) * LRU_BS)
        xc = cb_ref[:, cs]
        for tap in range(CONV_W):
            r0 = HALO - CONV_LEFT + tap
            xc = xc + xe_ref[r0:r0 + tl, cs] * cw_ref[tap:tap + 1, cs]
        g = _mm(xc, gw_ref[n])
        rec = _sigmoid(g[:, :LRU_BS] + gb_ref[0:1, cs])
        inp = _sigmoid(g[:, LRU_BS:] + gb_ref[1:2, cs])
        log_a = -LRU_C * rec * _softplus(-lam_ref[:, cs])
        a_ref[:, cs] = jnp.exp(log_a)
        th = jnp.tanh(log_a)
        bx_ref[:, cs] = jnp.sqrt(-2.0 * th / (1.0 - th)) * inp * xc

    def body(s, h):
        tt = (tl - 1 - s) if reverse else s
        h = a_ref[pl.ds(tt, 1), :] * h + bx_ref[pl.ds(tt, 1), :]
        hs_ref[pl.ds(tt, 1), :] = h
        return h

    h = lax.fori_loop(0, tl, body, h_ref[...], unroll=8)
    h_ref[...] = h

    @pl.when(i == nt - 1)
    def _():
        hlast_ref[...] = h

    if final:
        for n in range(LRU_BLOCKS):
            cs = slice(n * LRU_BS, (n + 1) * LRU_BS)
            y = (hs_ref[:, cs] + hb_ref[:, cs]) * jax.nn.gelu(gate_ref[:, cs])
            out_ref[:, cs] = y.astype(out_ref.dtype)
    else:
        out_ref[...] = hs_ref[...]


def _lru_pass(p, cw, cb, gw, gb, lam, h0, hb, *, reverse):
    bsz, l, _ = p.shape
    tl = min(256, l)
    nt = l // tl
    final = hb is not None
    nh = l // HALO
    tpb = tl // HALO

    def tmap(i):
        return (nt - 1 - i) if reverse else i

    main = pl.BlockSpec((None, tl, D), lambda b, i: (b, tmap(i), 0))
    in_specs = [main,
                pl.BlockSpec((None, HALO, D), lambda b, i: (b, jnp.maximum(tmap(i) * tpb - 1, 0), 0)),
                pl.BlockSpec((None, HALO, D), lambda b, i: (b, jnp.minimum((tmap(i) + 1) * tpb, nh - 1), 0))]
    args = [p, p, p]
    if final:
        in_specs += [pl.BlockSpec((None, tl, D), lambda b, i: (b, tmap(i), 1)), main]
        args += [p, hb]
    in_specs += [pl.BlockSpec((CONV_W, D), lambda b, i: (0, 0)),
                 pl.BlockSpec((1, D), lambda b, i: (0, 0)),
                 pl.BlockSpec((LRU_BLOCKS, LRU_BS, 2 * LRU_BS), lambda b, i: (0, 0, 0)),
                 pl.BlockSpec((2, D), lambda b, i: (0, 0)),
                 pl.BlockSpec((1, D), lambda b, i: (0, 0)),
                 pl.BlockSpec((None, 1, D), lambda b, i: (b, 0, 0))]
    args += [cw, cb, gw, gb, lam, h0]
    out, hlast = pl.pallas_call(
        functools.partial(_lru_kernel, reverse=reverse, final=final, tl=tl, nt=nt),
        grid=(bsz, nt),
        in_specs=in_specs,
        out_specs=[main, pl.BlockSpec((None, 1, D), lambda b, i: (b, 0, 0))],
        out_shape=[jax.ShapeDtypeStruct((bsz, l, D), bf16 if final else f32),
                   jax.ShapeDtypeStruct((bsz, 1, D), f32)],
        scratch_shapes=[pltpu.VMEM((tl + 2 * HALO, D), f32), pltpu.VMEM((tl, D), f32),
                        pltpu.VMEM((tl, D), f32), pltpu.VMEM((tl, D), f32), pltpu.VMEM((1, D), f32)],
        compiler_params=_cparams(2),
        name="lru_fwd" if final else "lru_bwd",
    )(*args)
    return out, hlast


def _rwkv_feat_kernel(p_ref, prev_ref, next_ref, mu_ref, w0_ref, wup_ref, a0_ref, aup_ref, gup_ref,
                      kk_ref, ka_ref, rk_ref, ones_ref,
                      r_out, v_out, kkv_out, g_out, bonus_out, lw_out, key_out, b_out,
                      pe_ref, *, tl, nt):
    i = pl.program_id(1)
    zero = jnp.zeros((HALO, N_RWKV_PAD), f32)
    pe_ref[0:HALO, :] = jnp.where(i > 0, prev_ref[...], zero)
    pe_ref[HALO:HALO + tl, :] = p_ref[...]
    pe_ref[HALO + tl:2 * HALO + tl, :] = jnp.where(i < nt - 1, next_ref[...], zero)

    def shifted(c0, width):
        cs = slice(c0, c0 + width)
        p = pe_ref[HALO:HALO + tl, cs]
        nb = 0.5 * (pe_ref[HALO - 1:HALO - 1 + tl, cs] + pe_ref[HALO + 1:HALO + 1 + tl, cs])
        return p + mu_ref[:, cs] * (nb - p)

    win_w = jnp.tanh(shifted(RW_OFF_W, 256)).astype(bf16)
    win_a = shifted(RW_OFF_A, 256).astype(bf16)
    win_g = _sigmoid(shifted(RW_OFF_G, 256)).astype(bf16)
    ones_bd = ones_ref[...]

    for g in range(RW_NG):
        cs = slice(g * RW_GW, (g + 1) * RW_GW)
        r = shifted(g * RW_GW, RW_GW)
        k = shifted(D + g * RW_GW, RW_GW)
        v = shifted(2 * D + g * RW_GW, RW_GW)
        kf = k * kk_ref[:, cs]
        kk = kf * lax.rsqrt(_mm_x2(kf * kf, ones_bd) + 1e-12)
        ksum = None
        for d in range(2):
            wv = -_softplus(-(w0_ref[d:d + 1, cs] + jnp.dot(win_w, wup_ref[d, :, cs],
                                                              preferred_element_type=f32))) - 0.5
            lw_out[d, :, cs] = -jnp.exp(wv)
            a = _sigmoid(a0_ref[d:d + 1, cs] + jnp.dot(win_a, aup_ref[d, :, cs],
                                                       preferred_element_type=f32))
            key = k * (1.0 + (a - 1.0) * ka_ref[:, cs])
            key_out[d, :, cs] = key
            b_out[d, :, cs] = kk * a
            ksum = key if ksum is None else ksum + key
        r_out[:, cs] = r
        v_out[:, cs] = v
        kkv_out[:, cs] = kk
        bonus_out[:, cs] = _mm_x2(r * ksum * rk_ref[:, cs], ones_bd) * v
        g_out[:, cs] = jnp.dot(win_g, gup_ref[:, cs], preferred_element_type=f32)


def _rwkv_feat(p, prm):
    bsz, l, _ = p.shape
    tl = min(128, l)
    nt = l // tl
    nh = l // HALO
    tpb = tl // HALO
    full2 = lambda shape: pl.BlockSpec(shape, lambda b, i: (0,) * len(shape))
    tok = pl.BlockSpec((None, tl, D), lambda b, i: (b, i, 0))
    tok2 = pl.BlockSpec((2, None, tl, D), lambda b, i: (0, b, i, 0))
    sd = jax.ShapeDtypeStruct((bsz, l, D), f32)
    sd2 = jax.ShapeDtypeStruct((2, bsz, l, D), f32)
    return pl.pallas_call(
        functools.partial(_rwkv_feat_kernel, tl=tl, nt=nt),
        grid=(bsz, nt),
        in_specs=[pl.BlockSpec((None, tl, N_RWKV_PAD), lambda b, i: (b, i, 0)),
                  pl.BlockSpec((None, HALO, N_RWKV_PAD), lambda b, i: (b, jnp.maximum(i * tpb - 1, 0), 0)),
                  pl.BlockSpec((None, HALO, N_RWKV_PAD),
                               lambda b, i: (b, jnp.minimum((i + 1) * tpb, nh - 1), 0)),
                  full2((1, N_RWKV_PAD)), full2((2, D)), full2((2, 256, D)), full2((2, D)),
                  full2((2, 256, D)), full2((256, D)), full2((1, D)), full2((1, D)), full2((1, D)),
                  full2((RW_GW, RW_GW))],
        out_specs=[tok, tok, tok, tok, tok, tok2, tok2, tok2],
        out_shape=[sd, sd, sd, sd, sd, sd2, sd2, sd2],
        scratch_shapes=[pltpu.VMEM((tl + 2 * HALO, N_RWKV_PAD), f32)],
        compiler_params=_cparams(2),
        name="rwkv_feat",
    )(p, p, p, prm["mu"], prm["w0"], prm["wup"], prm["a0"], prm["aup"], prm["gup"],
      prm["k_k"], prm["k_a"], prm["r_k"], prm["ones_bd"])


def _bd_expand(y, lane_head):
    yb = y.astype(f32)
    return jnp.concatenate([jnp.where(lane_head == h, yb, 0.0).astype(bf16) for h in range(4)], axis=0)


def _rwkv_scan_kernel(r_ref, v_ref, kk_ref, lw_ref, key_ref, b_ref, s0_ref, y_ref, sfin_ref,
                      s_ref, *, nc):
    c = RW_CHUNK
    d = pl.program_id(0)
    i = pl.program_id(2)
    fwd = d == 0

    @pl.when(i == 0)
    def _():
        s_ref[...] = s0_ref[...]

    row = _iota((c, c), 0)
    col = _iota((c, c), 1)
    sgn = jnp.where(fwd, 1, -1)
    tri = jnp.where((col - row) * sgn <= 0, 1.0, 0.0).astype(bf16)
    t4 = _iota((c, 4 * c), 0)
    j4 = _iota((c, 4 * c), 1) % c
    mask_s = (j4 - t4) * sgn < 0
    mask_i = (j4 - t4) * sgn <= 0
    eye4 = jnp.where(j4 == t4, 1.0, 0.0)
    lane_head = _iota((c, RW_GW), 1) // HEAD
    bd_mask = (_iota((RW_GW, RW_GW), 0) // HEAD) == (_iota((RW_GW, RW_GW), 1) // HEAD)

    def mmbd(x, y):
        return jnp.dot(x.astype(bf16), _bd_expand(y, lane_head), preferred_element_type=f32)

    groups = range(RW_NG)
    sl = [slice(g * RW_GW, (g + 1) * RW_GW) for g in groups]

    def state_free(rows):
        lw = [lw_ref[rows, sl[g]] for g in groups]
        cl = [_mm_e3(tri, lw[g]) for g in groups]
        tot = [jnp.sum(lw[g], axis=0, keepdims=True) for g in groups]
        v = [v_ref[rows, sl[g]] for g in groups]
        ar, a_b, a_k, bk = [], [], [], []
        for g in groups:
            g_inv = jnp.exp(-cl[g])
            g_end = jnp.exp(tot[g] - cl[g])
            bv = b_ref[rows, sl[g]]
            key = key_ref[rows, sl[g]]
            at = -kk_ref[rows, sl[g]] * jnp.exp(cl[g] - lw[g])
            rt = r_ref[rows, sl[g]] * jnp.exp(cl[g])
            ar.append(jnp.concatenate([at, rt], axis=0).astype(bf16))
            a_b.append(lax.dot_general(ar[g], _bd_expand(bv * g_inv, lane_head), (((1,), (1,)), ((), ())),
                                       preferred_element_type=f32))
            a_k.append(lax.dot_general(ar[g], _bd_expand(key * g_inv, lane_head), (((1,), (1,)), ((), ())),
                                       preferred_element_type=f32))
            bk.append(jnp.concatenate([bv * g_end, key * g_end], axis=0).astype(bf16))
        n_ab = [jnp.where(mask_s, a_b[g][:c], 0.0) for g in groups]
        a_rb = [jnp.where(mask_i, a_b[g][c:], 0.0) for g in groups]
        a_kk = [jnp.concatenate([jnp.where(mask_s, a_k[g][:c], 0.0), jnp.where(mask_i, a_k[g][c:], 0.0)],
                                axis=0) for g in groups]
        x = [eye4 + n_ab[g] for g in groups]
        m = [mmbd(n_ab[g], n_ab[g]) for g in groups]
        lvl = 2
        while lvl < c:
            if lvl * 2 < c:
                xm = [mmbd(jnp.concatenate([x[g], m[g]], axis=0), m[g]) for g in groups]
                x = [x[g] + xm[g][:c] for g in groups]
                m = [xm[g][c:] for g in groups]
            else:
                x = [x[g] + mmbd(x[g], m[g]) for g in groups]
            lvl *= 2
        av = [mmbd(a_kk[g], v[g]) for g in groups]
        return dict(ar=ar, x=x, av=av, a_rb=a_rb, v=v, bk=bk, tot=tot)

    def state_step(rows, p, s):
        sa = [_mm_nt(p["ar"][g], s[g]) for g in groups]
        u = [mmbd(p["x"][g], sa[g][:c] + p["av"][g][:c]) for g in groups]
        y = [sa[g][c:] + p["av"][g][c:] + mmbd(p["a_rb"][g], u[g]) for g in groups]
        upd = [_mm(jnp.concatenate([u[g], p["v"][g]], axis=0).T, p["bk"][g]) for g in groups]
        for g in groups:
            y_ref[rows, sl[g]] = y[g]
        return [s[g] * jnp.exp(p["tot"][g]) + jnp.where(bd_mask, upd[g], 0.0) for g in groups]

    first = pl.multiple_of(jnp.where(fwd, 0, c), c)
    chunk_rows = [pl.ds(first, c), pl.ds(pl.multiple_of(c - first, c), c)]
    parts = [state_free(rows) for rows in chunk_rows]
    s = [s_ref[g] for g in groups]
    for rows, p in zip(chunk_rows, parts):
        s = state_step(rows, p, s)
    for g in groups:
        s_ref[g] = s[g]

    @pl.when(i == nc - 1)
    def _():
        sfin_ref[...] = s_ref[...]


def _rwkv_scan(r, v, kk, lw, key, bvec, s0):
    bsz, l, _ = r.shape
    c = 2 * RW_CHUNK
    nc = l // c

    def cidx(d, i):
        return jnp.where(d == 0, i, nc - 1 - i)

    tok = pl.BlockSpec((None, c, D), lambda d, b, i: (b, cidx(d, i), 0))
    tok2 = pl.BlockSpec((None, None, c, D), lambda d, b, i: (d, b, cidx(d, i), 0))
    st = pl.BlockSpec((None, None, RW_NG, RW_GW, RW_GW), lambda d, b, i: (d, b, 0, 0, 0))
    return pl.pallas_call(
        functools.partial(_rwkv_scan_kernel, nc=nc),
        grid=(2, bsz, nc),
        in_specs=[tok, tok, tok, tok2, tok2, tok2, st],
        out_specs=[tok2, st],
        out_shape=[jax.ShapeDtypeStruct((2, bsz, l, D), f32),
                   jax.ShapeDtypeStruct((2, bsz, RW_NG, RW_GW, RW_GW), f32)],
        scratch_shapes=[pltpu.VMEM((RW_NG, RW_GW, RW_GW), f32)],
        compiler_params=_cparams(3),
        name="rwkv_scan",
    )(r, v, kk, lw, key, bvec, s0)


def _rwkv_out_kernel(y_ref, bonus_ref, g_ref, lnw_ref, lnb_ref, ones_ref, o_ref):
    ones_bd = ones_ref[...]
    for g in range(RW_NG):
        cs = slice(g * RW_GW, (g + 1) * RW_GW)
        y = y_ref[0, :, cs] + y_ref[1, :, cs]
        mean = _mm_x2(y, ones_bd) * (1.0 / HEAD)
        yc = y - mean
        var = _mm_x2(yc * yc, ones_bd) * (1.0 / HEAD)
        yn = yc * lax.rsqrt(var + GN_EPS) * lnw_ref[:, cs] + lnb_ref[:, cs]
        o_ref[:, cs] = ((yn + bonus_ref[:, cs]) * g_ref[:, cs]).astype(o_ref.dtype)


def _rwkv_out(y2, bonus, g, ln_w, ln_b, ones_bd):
    _, bsz, l, _ = y2.shape
    tl = min(256, l)
    tok = pl.BlockSpec((None, tl, D), lambda b, i: (b, i, 0))
    vec = pl.BlockSpec((1, D), lambda b, i: (0, 0))
    return pl.pallas_call(
        _rwkv_out_kernel,
        grid=(bsz, l // tl),
        in_specs=[pl.BlockSpec((2, None, tl, D), lambda b, i: (0, b, i, 0)), tok, tok, vec, vec,
                  pl.BlockSpec((RW_GW, RW_GW), lambda b, i: (0, 0))],
        out_specs=tok,
        out_shape=jax.ShapeDtypeStruct((bsz, l, D), bf16),
        compiler_params=_cparams(2),
        name="rwkv_out",
    )(y2, bonus, g, ln_w, ln_b, ones_bd)


def _ssd_prep_kernel(p_ref, prev_ref, next_ref, cw_ref, cb_ref, o_ref, pe_ref, *, tl, nt):
    i = pl.program_id(1)
    w = p_ref.shape[1]
    zero = jnp.zeros((HALO, w), f32)
    pe_ref[0:HALO, :] = jnp.where(i > 0, prev_ref[...], zero)
    pe_ref[HALO:HALO + tl, :] = p_ref[...]
    pe_ref[HALO + tl:2 * HALO + tl, :] = jnp.where(i < nt - 1, next_ref[...], zero)
    for n in range(w // 128):
        cs = slice(n * 128, (n + 1) * 128)
        xc = cb_ref[:, cs]
        for tap in range(CONV_W):
            r0 = HALO - CONV_LEFT + tap
            xc = xc + pe_ref[r0:r0 + tl, cs] * cw_ref[tap:tap + 1, cs]
        o_ref[:, cs] = _silu(xc)


def _ssd_prep(p, cw, cb):
    bsz, l, _ = p.shape
    tl = min(256, l)
    nt = l // tl
    nh = l // HALO
    tpb = tl // HALO
    wc = 1024
    c0 = D // wc
    return pl.pallas_call(
        functools.partial(_ssd_prep_kernel, tl=tl, nt=nt),
        grid=(bsz, nt, D_XBC // wc),
        in_specs=[pl.BlockSpec((None, tl, wc), lambda b, i, c: (b, i, c + c0)),
                  pl.BlockSpec((None, HALO, wc), lambda b, i, c: (b, jnp.maximum(i * tpb - 1, 0), c + c0)),
                  pl.BlockSpec((None, HALO, wc),
                               lambda b, i, c: (b, jnp.minimum((i + 1) * tpb, nh - 1), c + c0)),
                  pl.BlockSpec((CONV_W, wc), lambda b, i, c: (0, c)),
                  pl.BlockSpec((1, wc), lambda b, i, c: (0, c))],
        out_specs=pl.BlockSpec((None, tl, wc), lambda b, i, c: (b, i, c)),
        out_shape=jax.ShapeDtypeStruct((bsz, l, D_XBC), f32),
        scratch_shapes=[pltpu.VMEM((tl + 2 * HALO, wc), f32)],
        compiler_params=_cparams(3),
        name="ssd_prep",
    )(p, p, p, cw, cb)


def _ssd_kernel(xbc_ref, dtw_ref, dtb_ref, a_ref, dsk_ref, e_ref,
                h0_ref, y_ref, hfin_ref, cumt_ref, dtt_ref, h_ref, *, nc):
    q = SSM_Q
    d = pl.program_id(0)
    i = pl.program_id(2)
    fwd = d == 0

    @pl.when(i == 0)
    def _():
        h_ref[...] = h0_ref[...]

    row = _iota((q, q), 0)
    col = _iota((q, q), 1)
    sgn = jnp.where(fwd, 1, -1)
    low = (col - row) * sgn <= 0
    tri = jnp.where(low, 1.0, 0.0).astype(bf16)
    tri_t = jnp.where((row - col) * sgn <= 0, 1.0, 0.0).astype(bf16)

    dt = _softplus(dtw_ref[...] + dtb_ref[...])
    dta = dt * a_ref[...]
    cum = _mm_e3(tri, dta)
    tot = jnp.sum(dta, axis=0, keepdims=True)
    cumt_ref[...] = _mm_x3(dta.T, tri_t)
    dtt_ref[...] = dt.T
    e_d = e_ref[...]
    dec_e = _mm_x2(jnp.exp(cum), e_d)
    toend_e = _mm_x2(jnp.exp(tot - cum) * dt, e_d)
    tot_e = _mm_x2(jnp.broadcast_to(jnp.exp(tot), (8, 128)), e_d)[0:1]
    cum_d = pltpu.roll(cum, jnp.where(fwd, 0, 128 - SSM_HEADS), 1)
    lane = _iota((q, 128), 1)

    for g in range(SSM_G):
        bg = xbc_ref[:, D + g * SSM_N:D + (g + 1) * SSM_N]
        cg = xbc_ref[:, D + SSM_G * SSM_N + g * SSM_N:D + SSM_G * SSM_N + (g + 1) * SSM_N]
        cb = _mm_nt(cg, bg)
        gs = slice(g * 512, (g + 1) * 512)
        hg = h_ref[g]
        y_off = _mm(cg, hg) * dec_e[:, gs]
        for pr in range(4):
            ls = []
            for hh in range(2):
                h = g * 8 + pr * 2 + hh
                rowv = cumt_ref[pl.ds(d * 32 + h, 1), :]
                dtr = dtt_ref[pl.ds(d * 32 + h, 1), :]
                seg = cum_d[:, h:h + 1] - rowv
                ls.append(jnp.where(low, jnp.exp(jnp.where(low, seg, 0.0)), 0.0) * cb * dtr)
            lp = jnp.concatenate(ls, axis=1)
            ps = slice(g * 512 + pr * 128, g * 512 + (pr + 1) * 128)
            xp = xbc_ref[:, ps]
            bd2 = jnp.concatenate([jnp.where(lane < SSM_P, xp, 0.0), jnp.where(lane >= SSM_P, xp, 0.0)],
                                  axis=0)
            yd = _mm(lp, bd2)
            y_ref[:, ps] = yd + y_off[:, pr * 128:(pr + 1) * 128] + dsk_ref[:, ps] * xp
        xs = xbc_ref[:, gs] * toend_e[:, gs]
        states = _mm(bg.T, xs)
        h_ref[g] = hg * tot_e[:, gs] + states

    @pl.when(i == nc - 1)
    def _():
        hfin_ref[...] = h_ref[...]


def _ssd(p, prm, h0):
    bsz, l, _ = p.shape
    q = SSM_Q
    nc = l // q
    xbc = _ssd_prep(p, prm["cw"], prm["cb"])

    def cidx(d, i):
        return jnp.where(d == 0, i, nc - 1 - i)

    full = lambda shape: pl.BlockSpec(shape, lambda d, b, i: (0,) * len(shape))
    st = pl.BlockSpec((None, None, SSM_G, SSM_N, 8 * SSM_P), lambda d, b, i: (d, b, 0, 0, 0))
    return pl.pallas_call(
        functools.partial(_ssd_kernel, nc=nc),
        grid=(2, bsz, nc),
        in_specs=[pl.BlockSpec((None, q, D_XBC), lambda d, b, i: (b, cidx(d, i), 0)),
                  pl.BlockSpec((None, q, 128), lambda d, b, i: (b, cidx(d, i), SS_OFF_DT // 128)),
                  full((1, 128)), full((1, 128)),
                  pl.BlockSpec((None, 1, D), lambda d, b, i: (d, 0, 0)),
                  pl.BlockSpec((None, 128, D), lambda d, b, i: (d, 0, 0)),
                  st],
        out_specs=[pl.BlockSpec((None, None, q, D), lambda d, b, i: (d, b, cidx(d, i), 0)), st],
        out_shape=[jax.ShapeDtypeStruct((2, bsz, l, D), f32),
                   jax.ShapeDtypeStruct((2, bsz, SSM_G, SSM_N, 8 * SSM_P), f32)],
        scratch_shapes=[pltpu.VMEM((128, q), f32), pltpu.VMEM((128, q), f32),
                        pltpu.VMEM((SSM_G, SSM_N, 8 * SSM_P), f32)],
        compiler_params=_cparams(3),
        name="ssd",
    )(xbc, p, prm["dtb"], prm["a"], prm["dsk"], prm["e"], h0)


def _ssd_out_kernel(y_ref, z_ref, nw_ref, o_ref):
    gw = D // SSM_G
    for g in range(SSM_G):
        cs = slice(g * gw, (g + 1) * gw)
        yg = (y_ref[0, :, cs] + y_ref[1, :, cs]) * _silu(z_ref[:, cs])
        ms = jnp.mean(yg * yg, axis=-1, keepdims=True)
        o_ref[:, cs] = (yg * lax.rsqrt(ms + EPS) * nw_ref[:, cs]).astype(o_ref.dtype)


def _ssd_out(y2, p, norm_w):
    _, bsz, l, _ = y2.shape
    tl = min(256, l)
    tok = pl.BlockSpec((None, tl, D), lambda b, i: (b, i, 0))
    return pl.pallas_call(
        _ssd_out_kernel,
        grid=(bsz, l // tl),
        in_specs=[pl.BlockSpec((2, None, tl, D), lambda b, i: (0, b, i, 0)), tok,
                  pl.BlockSpec((1, D), lambda b, i: (0, 0))],
        out_specs=tok,
        out_shape=jax.ShapeDtypeStruct((bsz, l, D), bf16),
        compiler_params=_cparams(2),
        name="ssd_out",
    )(y2, p, norm_w)


def _merge_kernel(ya_ref, yb_ref, yc_ref, sa_ref, sb_ref, sc_ref, wa_ref, wb_ref, wc_ref, o_ref):
    acc = None
    for y_ref, s_ref, w_ref in ((ya_ref, sa_ref, wa_ref), (yb_ref, sb_ref, wb_ref), (yc_ref, sc_ref, wc_ref)):
        t = s_ref[...].astype(f32) * jnp.dot(y_ref[...], w_ref[...], preferred_element_type=f32)
        acc = t if acc is None else acc + t
    o_ref[...] = acc.astype(o_ref.dtype)


def _merge(ys, sig, ws):
    m = ys[0].shape[0]
    tm = min(512, m)
    tn = 512
    nj = D // tn
    ysp = pl.BlockSpec((tm, D), lambda i, j: (i, 0))
    wsp = pl.BlockSpec((D, tn), lambda i, j: (0, j))
    ssp = [pl.BlockSpec((tm, tn), functools.partial(lambda i, j, k: (i, k * nj + j), k=k)) for k in range(3)]
    return pl.pallas_call(
        _merge_kernel,
        grid=(m // tm, nj),
        in_specs=[ysp, ysp, ysp] + ssp + [wsp, wsp, wsp],
        out_specs=pl.BlockSpec((tm, tn), lambda i, j: (i, j)),
        out_shape=jax.ShapeDtypeStruct((m, D), bf16),
        compiler_params=_cparams(2),
        name="merge",
    )(ys[0], ys[1], ys[2], sig, sig, sig, ws[0], ws[1], ws[2])


def _wo_kernel(m_ref, w_ref, h_ref, g_ref, o_ref):
    o_ref[...] = h_ref[...] + g_ref[...] * jnp.dot(m_ref[...], w_ref[...], preferred_element_type=f32)


def _wo_residual(m, w_o, h, mod4, row_of_batch, k_gate, *, transposed):
    bsz, l, _ = h.shape
    if transposed:
        rows = l // GRID_W
        hv = h.reshape(bsz, rows, GRID_W * D)
        tl, nt = rows, GRID_W
        hspec = pl.BlockSpec((None, tl, D), lambda b, i: (b, 0, i))
        oshape = jax.ShapeDtypeStruct((bsz, rows, GRID_W * D), f32)
    else:
        tl = min(512, l)
        nt = l // tl
        hv = h
        hspec = pl.BlockSpec((None, tl, D), lambda b, i: (b, i, 0))
        oshape = jax.ShapeDtypeStruct((bsz, l, D), f32)
    out = pl.pallas_call(
        _wo_kernel,
        grid=(bsz, nt),
        in_specs=[pl.BlockSpec((None, tl, D), lambda b, i: (b, i, 0)),
                  pl.BlockSpec((D, D), lambda b, i: (0, 0)),
                  hspec,
                  pl.BlockSpec((None, None, 1, D), lambda b, i: (row_of_batch(b), k_gate, 0, 0))],
        out_specs=hspec,
        out_shape=oshape,
        compiler_params=_cparams(2),
        name="wo_residual",
    )(m, w_o, hv, mod4)
    return out.reshape(bsz, l, D)


def _router_kernel(x_ref, g_ref, sc_ref, sh_ref, r_ref, u_ref, gates_ref):
    x = x_ref[...]
    ms = jnp.mean(x * x, axis=-1, keepdims=True)
    u = (x * lax.rsqrt(ms + EPS) * g_ref[...]) * (1.0 + sc_ref[...]) + sh_ref[...]
    u_ref[...] = u.astype(u_ref.dtype)
    u0, u1, u2 = _split3(u)
    r0, r1, r2 = _split3(r_ref[...])
    dot = lambda a, b: jnp.dot(a, b, preferred_element_type=f32)
    logits = (dot(u0, r0) + (dot(u0, r1) + dot(u1, r0))
              + (dot(u1, r1) + dot(u0, r2) + dot(u2, r0)))
    lane = _iota(logits.shape, 1)
    neg = jnp.float32(-jnp.inf)
    lg = jnp.where(lane < N_EXPERTS, logits, neg)
    m1 = jnp.max(lg, axis=-1, keepdims=True)
    i1 = jnp.min(jnp.where(lg == m1, lane, 128), axis=-1, keepdims=True)
    lg2 = jnp.where(lane == i1, neg, lg)
    m2 = jnp.max(lg2, axis=-1, keepdims=True)
    i2 = jnp.min(jnp.where(lg2 == m2, lane, 128), axis=-1, keepdims=True)
    e2 = jnp.exp(m2 - m1)
    den = 1.0 + e2
    gates_ref[...] = jnp.where(lane == i1, 1.0 / den, 0.0) + jnp.where(lane == i2, e2 / den, 0.0)


def _norm_router(h, gain, mod4, row_of_batch, k_sc, k_sh, router_pad, out_dtype=bf16):
    bsz, l, _ = h.shape
    tl = min(512, l)
    tok = pl.BlockSpec((None, tl, D), lambda b, i: (b, i, 0))
    return pl.pallas_call(
        _router_kernel,
        grid=(bsz, l // tl),
        in_specs=[tok,
                  pl.BlockSpec((1, D), lambda b, i: (0, 0)),
                  pl.BlockSpec((None, None, 1, D), lambda b, i: (row_of_batch(b), k_sc, 0, 0)),
                  pl.BlockSpec((None, None, 1, D), lambda b, i: (row_of_batch(b), k_sh, 0, 0)),
                  pl.BlockSpec((D, 128), lambda b, i: (0, 0))],
        out_specs=[tok, pl.BlockSpec((None, tl, 128), lambda b, i: (b, i, 0))],
        out_shape=[jax.ShapeDtypeStruct((bsz, l, D), out_dtype), jax.ShapeDtypeStruct((bsz, l, 128), f32)],
        compiler_params=_cparams(2),
        name="norm_router",
    )(h, gain.reshape(1, D), mod4, mod4, router_pad)


def _up_kernel(*refs, gated):
    if gated:
        u_ref, w1_ref, w3_ref, gt_ref, o_ref, w1b_ref, w3b_ref = refs
    else:
        u_ref, w1_ref, w3_ref, o_ref, w1b_ref, w3b_ref = refs

    @pl.when(pl.program_id(2) == 0)
    def _():
        w1b_ref[...] = w1_ref[...].astype(bf16)
        w3b_ref[...] = w3_ref[...].astype(bf16)

    u = u_ref[...]
    a = jnp.dot(u, w1b_ref[...], preferred_element_type=f32)
    b = jnp.dot(u, w3b_ref[...], preferred_element_type=f32)
    act = _silu(a) * b
    if gated:
        e = pl.program_id(0)
        gt = gt_ref[...]
        lane = _iota(gt.shape, 1)
        act = act * jnp.sum(jnp.where(lane == e, gt, 0.0), axis=-1, keepdims=True)
    o_ref[...] = act.astype(o_ref.dtype)


def _swiglu_up(u, w1, w3, lead, gates=None):
    m = u.shape[0]
    _, ne, _, ff = w1.shape
    tm = min(1024, m)
    tn = 512
    nj = ff // tn
    gated = gates is not None
    wspec = pl.BlockSpec((None, None, D, tn), lambda e, j, i: (lead, e, 0, j))
    in_specs = [pl.BlockSpec((tm, D), lambda e, j, i: (i, 0)), wspec, wspec]
    args = [u, w1, w3]
    if gated:
        in_specs.append(pl.BlockSpec((tm, 128), lambda e, j, i: (i, 0)))
        args.append(gates)
    return pl.pallas_call(
        functools.partial(_up_kernel, gated=gated),
        grid=(ne, nj, m // tm),
        in_specs=in_specs,
        out_specs=pl.BlockSpec((tm, tn), lambda e, j, i: (i, e * nj + j)),
        out_shape=jax.ShapeDtypeStruct((m, ne * ff), bf16),
        scratch_shapes=[pltpu.VMEM((D, tn), bf16), pltpu.VMEM((D, tn), bf16)],
        compiler_params=_cparams(3),
        name="swiglu_up",
    )(*args)


def _down_kernel(a_ref, w_ref, h_ref, g_ref, o_ref, acc_ref, *, nk):
    k = pl.program_id(2)

    @pl.when(k == 0)
    def _():
        acc_ref[...] = jnp.zeros_like(acc_ref)

    acc_ref[...] += jnp.dot(a_ref[...], w_ref[...].astype(bf16), preferred_element_type=f32)

    @pl.when(k == nk - 1)
    def _():
        o_ref[...] = h_ref[...] + g_ref[...] * acc_ref[...]


def _down_residual(act, w2, h, mod4, row_of_batch, k_gate):
    bsz, l, kk = act.shape
    tl = min(512, l)
    tk = 1024
    nk = kk // tk
    return pl.pallas_call(
        functools.partial(_down_kernel, nk=nk),
        grid=(bsz, l // tl, nk),
        in_specs=[pl.BlockSpec((None, tl, tk), lambda b, i, k: (b, i, k)),
                  pl.BlockSpec((tk, D), lambda b, i, k: (k, 0)),
                  pl.BlockSpec((None, tl, D), lambda b, i, k: (b, i, 0)),
                  pl.BlockSpec((None, None, 1, D), lambda b, i, k: (row_of_batch(b), k_gate, 0, 0))],
        out_specs=pl.BlockSpec((None, tl, D), lambda b, i, k: (b, i, 0)),
        out_shape=jax.ShapeDtypeStruct((bsz, l, D), f32),
        scratch_shapes=[pltpu.VMEM((tl, D), f32)],
        compiler_params=_cparams(3),
        name="down_residual",
    )(act, w2, h, mod4)


MOE_TM = 512
MOE_TT = 512
NACT_LANE = 127


def _moe_count_kernel(gates_ref, pre_ref, cnt_ref, carry_ref):
    i = pl.program_id(0)

    @pl.when(i == 0)
    def _():
        carry_ref[...] = jnp.zeros_like(carry_ref)

    tt = gates_ref.shape[0]
    a = jnp.where(gates_ref[...] > 0.0, 1.0, 0.0)
    strict = jnp.where(_iota((tt, tt), 1) < _iota((tt, tt), 0), 1.0, 0.0).astype(bf16)
    carry = carry_ref[0:1, :]
    pre_ref[...] = jnp.dot(strict, a.astype(bf16), preferred_element_type=f32) + carry
    carry_ref[...] = jnp.broadcast_to(carry + jnp.sum(a, axis=0, keepdims=True), carry_ref.shape)
    cnt_ref[...] = carry_ref[...]


def _moe_place_kernel(gates_ref, pre_ref, cnt_ref, pos_ref, gv_ref, tile_ref, *, n_tiles):
    tt = gates_ref.shape[0]
    tm = float(MOE_TM)
    lane1 = _iota((1, 128), 1)
    cnt = cnt_ref[0:1, :]
    gsz = jnp.floor((cnt + (tm - 1.0)) * (1.0 / tm)) * tm
    upper = jnp.where(_iota((128, 128), 0) < _iota((128, 128), 1), 1.0, 0.0).astype(bf16)
    base = _mm_x3(jnp.broadcast_to(gsz, (8, 128)), upper)[0:1]
    gates = gates_ref[...]
    act = gates > 0.0
    pos = base + pre_ref[...]
    big = jnp.float32(1e9)
    p_lo = jnp.min(jnp.where(act, pos, big), axis=-1, keepdims=True)
    p_hi = jnp.max(jnp.where(act, pos, -1.0), axis=-1, keepdims=True)
    g_lo = jnp.sum(jnp.where(act & (pos == p_lo), gates, 0.0), axis=-1, keepdims=True)
    g_hi = jnp.where(p_hi != p_lo,
                     jnp.sum(jnp.where(act & (pos == p_hi), gates, 0.0), axis=-1, keepdims=True), 0.0)
    lane = _iota((tt, 128), 1)
    posf = jnp.where(lane == 0, p_lo, 0.0) + jnp.where(lane == 1, p_hi, 0.0)
    pos_ref[...] = posf.T[0:8].astype(jnp.int32)
    gv_ref[...] = jnp.where(lane == 0, g_lo, 0.0) + jnp.where(lane == 1, g_hi, 0.0)
    end = base + gsz
    start_j = lane1.astype(f32) * tm
    texp = jnp.zeros((1, 128), f32)
    for e in range(N_EXPERTS - 1):
        end_e = jnp.sum(jnp.where(lane1 == e, end, 0.0), axis=-1, keepdims=True)
        texp = texp + jnp.where(start_j >= end_e, 1.0, 0.0)
    total = jnp.sum(jnp.where(lane1 == N_EXPERTS - 1, end, 0.0), axis=-1, keepdims=True)
    table = jnp.where(lane1 == NACT_LANE, total * (1.0 / tm), texp)
    tile_ref[...] = jnp.broadcast_to(table, (8, 128)).astype(jnp.int32)


def _moe_scatter_kernel(pos_ref, v_ref, xs_in_ref, xs_ref, sem, *, dump_base):
    del xs_in_ref
    tt = v_ref.shape[0]

    def row_copy(r, dst_row):
        return pltpu.make_async_copy(v_ref.at[pl.ds(r, 1)], xs_ref.at[pl.ds(dst_row, 1)], sem)

    def body(r, carry):
        p0 = pos_ref[0, r]
        p1 = pos_ref[1, r]
        p1 = jnp.where(p1 == p0, dump_base + r, p1)
        row_copy(r, p0).start()
        row_copy(r, p1).start()
        return carry

    lax.fori_loop(0, tt, body, 0, unroll=8)
    for _ in range(2):
        pltpu.make_async_copy(v_ref, xs_ref.at[pl.ds(0, tt)], sem).wait()


def _new_expert_tile(tile_ref, t):
    return (t == 0) | (tile_ref[t] != tile_ref[jnp.maximum(t - 1, 0)])


def _moe_gup_kernel(tile_ref, x_ref, w1_ref, w3_ref, o_ref, w1b_ref, w3b_ref):
    t = pl.program_id(1)
    active = t < tile_ref[NACT_LANE]

    @pl.when(active & _new_expert_tile(tile_ref, t))
    def _():
        w1b_ref[...] = w1_ref[...].astype(bf16)
        w3b_ref[...] = w3_ref[...].astype(bf16)

    @pl.when(active)
    def _():
        x = x_ref[...].astype(bf16)
        a = jnp.dot(x, w1b_ref[...], preferred_element_type=f32)
        b = jnp.dot(x, w3b_ref[...], preferred_element_type=f32)
        o_ref[...] = (_silu(a) * b).astype(o_ref.dtype)

    @pl.when(jnp.logical_not(active))
    def _():
        o_ref[...] = jnp.zeros_like(o_ref)


def _moe_gdown_kernel(tile_ref, a_ref, w_ref, o_ref, wb_ref):
    t = pl.program_id(1)
    active = t < tile_ref[NACT_LANE]

    @pl.when(active & _new_expert_tile(tile_ref, t))
    def _():
        wb_ref[...] = w_ref[...].astype(bf16)

    @pl.when(active)
    def _():
        o_ref[...] = jnp.dot(a_ref[...], wb_ref[...], preferred_element_type=f32)

    @pl.when(jnp.logical_not(active))
    def _():
        o_ref[...] = jnp.zeros_like(o_ref)


def _moe_combine_kernel(pos_ref, gv_ref, h_ref, gf_ref, ys_ref, o_ref, buf0, buf1, sem):
    tt = h_ref.shape[0]

    def row_copy(src_row, buf, r):
        return pltpu.make_async_copy(ys_ref.at[pl.ds(src_row, 1)], buf.at[pl.ds(r, 1)], sem)

    def body(r, carry):
        row_copy(pos_ref[0, r], buf0, r).start()
        row_copy(pos_ref[1, r], buf1, r).start()
        return carry

    lax.fori_loop(0, tt, body, 0, unroll=8)
    for buf in (buf0, buf1):
        pltpu.make_async_copy(ys_ref.at[pl.ds(0, tt)], buf, sem).wait()
    gv = gv_ref[...]
    lane = _iota(gv.shape, 1)
    g0 = jnp.sum(jnp.where(lane == 0, gv, 0.0), axis=-1, keepdims=True)
    g1 = jnp.sum(jnp.where(lane == 1, gv, 0.0), axis=-1, keepdims=True)
    o_ref[...] = h_ref[...] + gf_ref[...] * (g0 * buf0[...] + g1 * buf1[...])


def _moe_sparse(v32, gates, h, w1, w3, w2, lead, mod4, row_of_batch, k_gate):
    bsz, l, _ = h.shape
    n_tok = bsz * l
    tt = min(MOE_TT, n_tok)
    n_tt = n_tok // tt
    n_tiles = (2 * n_tok) // MOE_TM + N_EXPERTS
    rows = n_tiles * MOE_TM
    ff = w1.shape[3]
    vf = v32.reshape(n_tok, D)
    gf = gates.reshape(n_tok, 128)

    pre, cnt = pl.pallas_call(
        _moe_count_kernel,
        grid=(n_tt,),
        in_specs=[pl.BlockSpec((tt, 128), lambda i: (i, 0))],
        out_specs=[pl.BlockSpec((tt, 128), lambda i: (i, 0)), pl.BlockSpec((8, 128), lambda i: (0, 0))],
        out_shape=[jax.ShapeDtypeStruct((n_tok, 128), f32), jax.ShapeDtypeStruct((8, 128), f32)],
        scratch_shapes=[pltpu.VMEM((8, 128), f32)],
        compiler_params=_cparams(1),
        name="moe_count",
    )(gf)

    pos, gv, table = pl.pallas_call(
        functools.partial(_moe_place_kernel, n_tiles=n_tiles),
        grid=(n_tt,),
        in_specs=[pl.BlockSpec((tt, 128), lambda i: (i, 0)), pl.BlockSpec((tt, 128), lambda i: (i, 0)),
                  pl.BlockSpec((8, 128), lambda i: (0, 0))],
        out_specs=[pl.BlockSpec((8, tt), lambda i: (0, i)), pl.BlockSpec((tt, 128), lambda i: (i, 0)),
                   pl.BlockSpec((8, 128), lambda i: (0, 0))],
        out_shape=[jax.ShapeDtypeStruct((8, n_tok), jnp.int32), jax.ShapeDtypeStruct((n_tok, 128), f32),
                   jax.ShapeDtypeStruct((8, 128), jnp.int32)],
        compiler_params=_cparams(1),
        name="moe_place",
    )(gf, pre, cnt)
    tile_tab = table[0]

    smem_pos = pl.BlockSpec((8, tt), lambda i: (0, i), memory_space=pltpu.SMEM)
    xs = pl.pallas_call(
        functools.partial(_moe_scatter_kernel, dump_base=rows),
        grid=(n_tt,),
        in_specs=[smem_pos, pl.BlockSpec((tt, D), lambda i: (i, 0)), pl.BlockSpec(memory_space=pl.ANY)],
        out_specs=pl.BlockSpec(memory_space=pl.ANY),
        out_shape=jax.ShapeDtypeStruct((rows + tt, D), f32),
        scratch_shapes=[pltpu.SemaphoreType.DMA],
        input_output_aliases={2: 0},
        compiler_params=_cparams(1),
        name="moe_scatter",
    )(pos, vf, jnp.zeros((rows + tt, D), f32))

    assert n_tiles < NACT_LANE
    tn = 512
    nj = ff // tn
    last_tile = lambda tab: jnp.maximum(tab[NACT_LANE] - 1, 0)
    act = pl.pallas_call(
        _moe_gup_kernel,
        grid_spec=pltpu.PrefetchScalarGridSpec(
            num_scalar_prefetch=1,
            grid=(nj, n_tiles),
            in_specs=[pl.BlockSpec((MOE_TM, D), lambda j, t, tab: (jnp.minimum(t, last_tile(tab)), 0)),
                      pl.BlockSpec((None, None, D, tn),
                                   lambda j, t, tab: (lead, tab[jnp.minimum(t, last_tile(tab))], 0, j)),
                      pl.BlockSpec((None, None, D, tn),
                                   lambda j, t, tab: (lead, tab[jnp.minimum(t, last_tile(tab))], 0, j))],
            out_specs=pl.BlockSpec((MOE_TM, tn), lambda j, t, tab: (t, j)),
            scratch_shapes=[pltpu.VMEM((D, tn), bf16), pltpu.VMEM((D, tn), bf16)]),
        out_shape=jax.ShapeDtypeStruct((rows, ff), bf16),
        compiler_params=_cparams(2),
        name="moe_up",
    )(tile_tab, xs, w1, w3)

    tnd = 1024
    ys = pl.pallas_call(
        _moe_gdown_kernel,
        grid_spec=pltpu.PrefetchScalarGridSpec(
            num_scalar_prefetch=1,
            grid=(D // tnd, n_tiles),
            in_specs=[pl.BlockSpec((MOE_TM, ff), lambda j, t, tab: (jnp.minimum(t, last_tile(tab)), 0)),
                      pl.BlockSpec((None, None, ff, tnd),
                                   lambda j, t, tab: (lead, tab[jnp.minimum(t, last_tile(tab))], 0, j))],
            out_specs=pl.BlockSpec((MOE_TM, tnd), lambda j, t, tab: (t, j)),
            scratch_shapes=[pltpu.VMEM((ff, tnd), bf16)]),
        out_shape=jax.ShapeDtypeStruct((rows, D), f32),
        compiler_params=_cparams(2),
        name="moe_down",
    )(tile_tab, act, w2)

    tpb = l // tt if l >= tt else 1
    out = pl.pallas_call(
        _moe_combine_kernel,
        grid=(n_tt,),
        in_specs=[smem_pos, pl.BlockSpec((tt, 128), lambda i: (i, 0)), pl.BlockSpec((tt, D), lambda i: (i, 0)),
                  pl.BlockSpec((None, None, 1, D), lambda i: (row_of_batch(i // tpb), k_gate, 0, 0)),
                  pl.BlockSpec(memory_space=pl.ANY)],
        out_specs=pl.BlockSpec((tt, D), lambda i: (i, 0)),
        out_shape=jax.ShapeDtypeStruct((n_tok, D), f32),
        scratch_shapes=[pltpu.VMEM((tt, D), f32), pltpu.VMEM((tt, D), f32), pltpu.SemaphoreType.DMA],
        compiler_params=_cparams(1),
        name="moe_combine",
    )(pos, gv, h.reshape(n_tok, D), mod4, ys)
    return out.reshape(bsz, l, D)


def _block_ones(n, bs):
    idx = jnp.arange(n) // bs
    return (idx[:, None] == idx[None, :]).astype(bf16)


def _pad_cols(w, n):
    return jnp.pad(w, ((0, 0), (0, n - w.shape[1])))


def _rwkv_params(li, mu, w0, w_up, a0, a_up, g_up, k_k, k_a, r_k, ln_w, ln_b):
    wup = jnp.zeros((2, 256, D), f32)
    aup = jnp.zeros((2, 256, D), f32)
    for d in range(2):
        wup = wup.at[d, d * W_LORA:(d + 1) * W_LORA].set(w_up[li, d])
        o = (RW_OFF_G - 2 * A_LORA) - RW_OFF_A + d * A_LORA
        aup = aup.at[d, o:o + A_LORA].set(a_up[li, d])
    return {
        "mu": jnp.pad(mu[li], (0, N_RWKV_PAD - N_RWKV)).reshape(1, N_RWKV_PAD),
        "w0": w0[li], "wup": wup.astype(bf16), "a0": a0[li], "aup": aup.astype(bf16),
        "gup": g_up[li].astype(bf16), "k_k": k_k[li].reshape(1, D), "k_a": k_a[li].reshape(1, D),
        "r_k": r_k[li].reshape(1, D), "ln_w": ln_w[li].reshape(1, D), "ln_b": ln_b[li].reshape(1, D),
        "ones_bd": _block_ones(RW_GW, HEAD),
    }


def _ssd_params(li, conv_w, conv_b, a_log, dt_bias, d_skip):
    a = -jnp.exp(a_log[li].astype(f32))
    hid = jnp.arange(D) // SSM_P
    sel = jnp.arange(128)[:, None]
    e = jnp.stack([(sel == hid[None, :] + d * SSM_HEADS) for d in range(2)]).astype(bf16)
    return {
        "cw": conv_w[li], "cb": conv_b[li].reshape(1, D_XBC),
        "dtb": jnp.pad(dt_bias[li].reshape(1, 2 * SSM_HEADS), ((0, 0), (0, 64))),
        "a": jnp.pad(a.reshape(1, 2 * SSM_HEADS), ((0, 0), (0, 64))),
        "dsk": jnp.stack([jnp.repeat(d_skip[li], SSM_P), jnp.zeros((D,), f32)]).reshape(2, 1, D),
        "e": e,
    }


def kernel(x, c, ctx, c_ctx, ada_w, ada_b, norm_mix, norm_ffn, norm_final, w_in, lru_conv_w, lru_conv_b, lru_gate_w, lru_gate_b, lru_lambda, rwkv_mu, rwkv_w0, rwkv_w_up, rwkv_a0, rwkv_a_up, rwkv_g_up, rwkv_k_k, rwkv_k_a, rwkv_r_k, rwkv_ln_w, rwkv_ln_b, ssm_conv_w, ssm_conv_b, ssm_a_log, ssm_dt_bias, ssm_d, ssm_norm_w, w_out_lru, w_out_rwkv, w_out_ssm, w_o, ffn_w1, ffn_w3, ffn_w2, moe_router, moe_w1, moe_w3, moe_w2):
    bsz, l, _ = x.shape
    depth = ada_w.shape[0]
    off_lru = 3 * D
    off_rwkv = off_lru + 2 * D
    off_ssm = off_rwkv + N_RWKV
    cvec = jnp.zeros((8, D), f32).at[:bsz].set(c).at[bsz].set(c_ctx)
    lat_row = lambda b: b
    ctx_row = lambda b: bsz

    h_lat, h_ctx = x, ctx
    for li in range(depth):
        last = li == depth - 1
        odd = li % 2 == 1
        mod4 = _ada(cvec, ada_w, ada_b, li).reshape(8, 6, 1, D)

        w_ss = _pad_cols(w_in[li, :, off_ssm:], N_SSM_PAD)[None]
        lru_gw = [jnp.concatenate([lru_gate_w[li, d, 0], lru_gate_w[li, d, 1]], axis=-1).astype(bf16)
                  for d in range(2)]
        rp = _rwkv_params(li, rwkv_mu, rwkv_w0, rwkv_w_up, rwkv_a0, rwkv_a_up, rwkv_g_up, rwkv_k_k,
                          rwkv_k_a, rwkv_r_k, rwkv_ln_w, rwkv_ln_b)
        sp = _ssd_params(li, ssm_conv_w, ssm_conv_b, ssm_a_log, ssm_dt_bias, ssm_d)
        w_outs = [w_out_lru[li].astype(bf16), w_out_rwkv[li].astype(bf16), w_out_ssm[li].astype(bf16)]
        w_o_b = w_o[li].astype(bf16)

        def token_mix(u, states, need_out):
            lx = u.shape[1]
            um = u.reshape(bsz * lx, D)
            p_lru = _proj(um, w_in, li, tn=1024, col0=off_lru, n=2 * D).reshape(bsz, lx, 2 * D)
            p_rw = _proj(um, w_in, li, tn=1024, col0=off_rwkv, n=N_RWKV_PAD).reshape(bsz, lx, N_RWKV_PAD)
            p_ss = _proj(um, w_ss, 0, tn=768).reshape(bsz, lx, N_SSM_PAD)
            lru_s, rw_s, ss_s = states
            cw, cb = lru_conv_w[li], lru_conv_b[li].reshape(1, D)
            hb, hl_b = _lru_pass(p_lru, cw, cb, lru_gw[1], lru_gate_b[li, 1], lru_lambda[li, 1].reshape(1, D),
                                 lru_s[1], None, reverse=True)
            ya, hl_f = _lru_pass(p_lru, cw, cb, lru_gw[0], lru_gate_b[li, 0], lru_lambda[li, 0].reshape(1, D),
                                 lru_s[0], hb, reverse=False)
            r, v, kk, g, bonus, lw, key, bvec = _rwkv_feat(p_rw, rp)
            y2, rw_fin = _rwkv_scan(r, v, kk, lw, key, bvec, rw_s)
            ys2, ss_fin = _ssd(p_ss, sp, ss_s)
            new_states = ((hl_f, hl_b), rw_fin, ss_fin)
            if not need_out:
                return None, new_states
            yb = _rwkv_out(y2, bonus, g, rp["ln_w"], rp["ln_b"], rp["ones_bd"])
            yc = _ssd_out(ys2, p_ss, ssm_norm_w[li].reshape(1, D))
            sig = _proj(um, w_in, li, tn=1024, col0=0, n=off_lru, act="sigmoid", out_dtype=bf16)
            m = _merge([ya.reshape(-1, D), yb.reshape(-1, D), yc.reshape(-1, D)], sig, w_outs)
            return m.reshape(bsz, lx, D), new_states

        zero_states = ((jnp.zeros((bsz, 1, D), f32), jnp.zeros((bsz, 1, D), f32)),
                       jnp.zeros((2, bsz, RW_NG, RW_GW, RW_GW), f32),
                       jnp.zeros((2, bsz, SSM_G, SSM_N, 8 * SSM_P), f32))

        u_ctx = _norm(h_ctx, norm_mix[li], mod4, ctx_row, 1, 0)
        u_lat = _norm(h_lat, norm_mix[li], mod4, lat_row, 1, 0, transposed=odd)
        m_ctx, ctx_states = token_mix(u_ctx, zero_states, not last)
        m_lat, _ = token_mix(u_lat, ctx_states, True)
        h_lat = _wo_residual(m_lat, w_o_b, h_lat, mod4, lat_row, 2, transposed=odd)
        if not last:
            h_ctx = _wo_residual(m_ctx, w_o_b, h_ctx, mod4, ctx_row, 2, transposed=False)

        j = li // 2
        streams = [(h_lat, lat_row)] + ([] if last else [(h_ctx, ctx_row)])
        outs = []
        for h, row_fn in streams:
            lx = h.shape[1]
            if not odd:
                v = _norm(h, norm_ffn[li], mod4, row_fn, 4, 3)
                act = _swiglu_up(v.reshape(bsz * lx, D), ffn_w1[:, None], ffn_w3[:, None], j)
                w2 = ffn_w2[j].astype(bf16)
            elif row_fn is lat_row:
                router_pad = _pad_cols(moe_router[j], 128)
                v32, gates = _norm_router(h, norm_ffn[li], mod4, row_fn, 4, 3, router_pad, out_dtype=f32)
                outs.append(_moe_sparse(v32, gates, h, moe_w1, moe_w3, moe_w2, j, mod4, row_fn, 5))
                continue
            else:
                router_pad = _pad_cols(moe_router[j], 128)
                v, gates = _norm_router(h, norm_ffn[li], mod4, row_fn, 4, 3, router_pad)
                act = _swiglu_up(v.reshape(bsz * lx, D), moe_w1, moe_w3, j, gates.reshape(bsz * lx, 128))
                w2 = moe_w2[j].astype(bf16).reshape(N_EXPERTS * D_FF_EXPERT, D)
            outs.append(_down_residual(act.reshape(bsz, lx, -1), w2, h, mod4, row_fn, 5))
        h_lat = outs[0]
        if not last:
            h_ctx = outs[1]
    return _final_norm(h_lat, norm_final)
```

```python
import functools
import math

import jax
import jax.numpy as jnp
from jax import lax
from jax.experimental import pallas as pl
from jax.experimental.pallas import tpu as pltpu

f32 = jnp.float32
bf16 = jnp.bfloat16

D = 2048
GRID_W = 64
EPS = 1e-6
CONV_W = 4
CONV_LEFT = 2
HALO = 8

LRU_BLOCKS = 16
LRU_BS = D // LRU_BLOCKS
LRU_C = 8.0

HEADS = 32
HEAD = 64
W_LORA = 96
A_LORA = 96
G_LORA = 256
GN_EPS = 64e-5
N_RWKV = 3 * D + 2 * W_LORA + 2 * A_LORA + G_LORA
N_RWKV_PAD = 7168
RW_OFF_W = 3 * D
RW_OFF_A = 3 * D + 128
RW_OFF_G = 3 * D + 2 * W_LORA + 2 * A_LORA
RW_CHUNK = 64
RW_GW = 256
RW_NG = D // RW_GW

SSM_HEADS = 32
SSM_P = 64
SSM_N = 128
SSM_G = 4
SSM_Q = 128
D_XBC = D + 2 * SSM_G * SSM_N
N_SSM = D + D_XBC + 2 * SSM_HEADS
N_SSM_PAD = 5376
SS_OFF_DT = D + D_XBC

D_FF = 3 * D
N_EXPERTS = 8
D_FF_EXPERT = D_FF // 2

VMEM_LIMIT = 56 * 1024 * 1024


def _cparams(n_axes, vmem=VMEM_LIMIT):
    return pltpu.CompilerParams(dimension_semantics=("arbitrary",) * n_axes, vmem_limit_bytes=vmem)


def _mm(a, b):
    return jnp.dot(a.astype(bf16), b.astype(bf16), preferred_element_type=f32)


def _mm_nt(a, b):
    return lax.dot_general(a.astype(bf16), b.astype(bf16), (((1,), (1,)), ((), ())),
                           preferred_element_type=f32)


def _split3(x):
    x0 = x.astype(bf16)
    r = x - x0.astype(f32)
    x1 = r.astype(bf16)
    r = r - x1.astype(f32)
    return x0, x1, r.astype(bf16)


def _mm_x3(x, e):
    x0, x1, x2 = _split3(x)
    return (jnp.dot(x0, e, preferred_element_type=f32) + jnp.dot(x1, e, preferred_element_type=f32)
            + jnp.dot(x2, e, preferred_element_type=f32))


def _mm_x2(x, e):
    x0 = x.astype(bf16)
    x1 = (x - x0.astype(f32)).astype(bf16)
    return jnp.dot(x0, e, preferred_element_type=f32) + jnp.dot(x1, e, preferred_element_type=f32)


def _mm_e3(e, x):
    x0, x1, x2 = _split3(x)
    return (jnp.dot(e, x0, preferred_element_type=f32) + jnp.dot(e, x1, preferred_element_type=f32)
            + jnp.dot(e, x2, preferred_element_type=f32))


def _softplus(x):
    return jnp.maximum(x, 0.0) + jnp.log1p(jnp.exp(-jnp.abs(x)))


def _sigmoid(x):
    return 0.5 * jnp.tanh(0.5 * x) + 0.5


def _silu(x):
    return x * _sigmoid(x)


def _iota(shape, dim):
    return lax.broadcasted_iota(jnp.int32, shape, dim)


def _ada_kernel(c_ref, w_ref, b_ref, o_ref):
    cv = c_ref[...]
    o_ref[...] = _mm(_silu(cv), w_ref[...]) + b_ref[...]


def _ada(cvec, w, b, li):
    depth, _, n = w.shape
    tn = 1536
    return pl.pallas_call(
        _ada_kernel,
        grid=(n // tn,),
        in_specs=[pl.BlockSpec((8, D), lambda j: (0, 0)),
                  pl.BlockSpec((None, D, tn), lambda j: (li, 0, j)),
                  pl.BlockSpec((None, 1, tn), lambda j: (li, 0, j))],
        out_specs=pl.BlockSpec((8, tn), lambda j: (0, j)),
        out_shape=jax.ShapeDtypeStruct((8, n), f32),
        compiler_params=_cparams(1),
        name="ada",
    )(cvec, w, b.reshape(depth, 1, n))


def _norm_kernel(x_ref, g_ref, sc_ref, sh_ref, o_ref):
    x = x_ref[...]
    ms = jnp.mean(x * x, axis=-1, keepdims=True)
    xn = x * lax.rsqrt(ms + EPS) * g_ref[...]
    o_ref[...] = (xn * (1.0 + sc_ref[...]) + sh_ref[...]).astype(o_ref.dtype)


def _norm(h, gain, mod4, row_of_batch, k_sc, k_sh, *, transposed=False, out_dtype=bf16):
    bsz, l, _ = h.shape
    if transposed:
        rows = l // GRID_W
        hin = h.reshape(bsz, rows, GRID_W * D)
        tl = rows
        nt = GRID_W
        in_spec = pl.BlockSpec((None, tl, D), lambda b, i: (b, 0, i))
    else:
        tl = min(512, l)
        nt = l // tl
        hin = h
        in_spec = pl.BlockSpec((None, tl, D), lambda b, i: (b, i, 0))
    return pl.pallas_call(
        _norm_kernel,
        grid=(bsz, nt),
        in_specs=[in_spec,
                  pl.BlockSpec((1, D), lambda b, i: (0, 0)),
                  pl.BlockSpec((None, None, 1, D), lambda b, i: (row_of_batch(b), k_sc, 0, 0)),
                  pl.BlockSpec((None, None, 1, D), lambda b, i: (row_of_batch(b), k_sh, 0, 0))],
        out_specs=pl.BlockSpec((None, tl, D), lambda b, i: (b, i, 0)),
        out_shape=jax.ShapeDtypeStruct((bsz, l, D), out_dtype),
        compiler_params=_cparams(2),
        name="norm",
    )(hin, gain.reshape(1, D), mod4, mod4)


def _plain_norm_kernel(x_ref, g_ref, o_ref):
    x = x_ref[...]
    ms = jnp.mean(x * x, axis=-1, keepdims=True)
    o_ref[...] = (x * lax.rsqrt(ms + EPS) * g_ref[...]).astype(o_ref.dtype)


def _final_norm(h, gain):
    bsz, l, _ = h.shape
    tl = min(512, l)
    return pl.pallas_call(
        _plain_norm_kernel,
        grid=(bsz, l // tl),
        in_specs=[pl.BlockSpec((None, tl, D), lambda b, i: (b, i, 0)),
                  pl.BlockSpec((1, D), lambda b, i: (0, 0))],
        out_specs=pl.BlockSpec((None, tl, D), lambda b, i: (b, i, 0)),
        out_shape=jax.ShapeDtypeStruct((bsz, l, D), f32),
        compiler_params=_cparams(2),
        name="final_norm",
    )(h, gain.reshape(1, D))


def _proj_kernel(u_ref, w_ref, o_ref, wb_ref, *, act):
    @pl.when(pl.program_id(1) == 0)
    def _():
        wb_ref[...] = w_ref[...].astype(bf16)

    acc = jnp.dot(u_ref[...], wb_ref[...], preferred_element_type=f32)
    if act == "sigmoid":
        acc = _sigmoid(acc)
    o_ref[...] = acc.astype(o_ref.dtype)


def _proj(u, w, li, *, tn, tm=2048, col0=0, n=None, act=None, out_dtype=f32):
    m, k = u.shape
    n = w.shape[2] if n is None else n
    assert col0 % tn == 0 and n % tn == 0 and col0 + n <= w.shape[2]
    off = col0 // tn
    tm = min(tm, m)
    return pl.pallas_call(
        functools.partial(_proj_kernel, act=act),
        grid=(n // tn, m // tm),
        in_specs=[pl.BlockSpec((tm, k), lambda j, i: (i, 0)),
                  pl.BlockSpec((None, k, tn), lambda j, i: (li, 0, j + off))],
        out_specs=pl.BlockSpec((tm, tn), lambda j, i: (i, j)),
        out_shape=jax.ShapeDtypeStruct((m, n), out_dtype),
        scratch_shapes=[pltpu.VMEM((k, tn), bf16)],
        compiler_params=_cparams(2),
        name="proj",
    )(u, w)


def _lru_kernel(*refs, reverse, final, tl, nt):
    if final:
        (x_ref, prev_ref, next_ref, gate_ref, hb_ref, cw_ref, cb_ref, gw_ref, gb_ref, lam_ref, h0_ref,
         out_ref, hlast_ref, xe_ref, a_ref, bx_ref, hs_ref, h_ref) = refs
    else:
        (x_ref, prev_ref, next_ref, cw_ref, cb_ref, gw_ref, gb_ref, lam_ref, h0_ref,
         out_ref, hlast_ref, xe_ref, a_ref, bx_ref, hs_ref, h_ref) = refs
    i = pl.program_id(1)
    t = (nt - 1 - i) if reverse else i

    @pl.when(i == 0)
    def _():
        h_ref[...] = h0_ref[...]

    zero = jnp.zeros((HALO, D), f32)
    xe_ref[0:HALO, :] = jnp.where(t > 0, prev_ref[...], zero)
    xe_ref[HALO:HALO + tl, :] = x_ref[...]
    xe_ref[HALO + tl:2 * HALO + tl, :] = jnp.where(t < nt - 1, next_ref[...], zero)

    for n in range(LRU_BLOCKS):
        cs = slice(n * LRU_BS, (n + 1) * LRU_BS)
        xc = cb_ref[:, cs]
        for tap in range(CONV_W):
            r0 = HALO - CONV_LEFT + tap
            xc = xc + xe_ref[r0:r0 + tl, cs] * cw_ref[tap:tap + 1, cs]
        g = _mm(xc, gw_ref[n])
        rec = _sigmoid(g[:, :LRU_BS] + gb_ref[0:1, cs])
        inp = _sigmoid(g[:, LRU_BS:] + gb_ref[1:2, cs])
        log_a = -LRU_C * rec * _softplus(-lam_ref[:, cs])
        a_ref[:, cs] = jnp.exp(log_a)
        th = jnp.tanh(log_a)
        bx_ref[:, cs] = jnp.sqrt(-2.0 * th / (1.0 - th)) * inp * xc

    def body(s, h):
        tt = (tl - 1 - s) if reverse else s
        h = a_ref[pl.ds(tt, 1), :] * h + bx_ref[pl.ds(tt, 1), :]
        hs_ref[pl.ds(tt, 1), :] = h
        return h

    h = lax.fori_loop(0, tl, body, h_ref[...], unroll=8)
    h_ref[...] = h

    @pl.when(i == nt - 1)
    def _():
        hlast_ref[...] = h

    if final:
        for n in range(LRU_BLOCKS):
            cs = slice(n * LRU_BS, (n + 1) * LRU_BS)
            y = (hs_ref[:, cs] + hb_ref[:, cs]) * jax.nn.gelu(gate_ref[:, cs])
            out_ref[:, cs] = y.astype(out_ref.dtype)
    else:
        out_ref[...] = hs_ref[...]


def _lru_pass(p, cw, cb, gw, gb, lam, h0, hb, *, reverse):
    bsz, l, _ = p.shape
    tl = min(256, l)
    nt = l // tl
    final = hb is not None
    nh = l // HALO
    tpb = tl // HALO

    def tmap(i):
        return (nt - 1 - i) if reverse else i

    main = pl.BlockSpec((None, tl, D), lambda b, i: (b, tmap(i), 0))
    in_specs = [main,
                pl.BlockSpec((None, HALO, D), lambda b, i: (b, jnp.maximum(tmap(i) * tpb - 1, 0), 0)),
                pl.BlockSpec((None, HALO, D), lambda b, i: (b, jnp.minimum((tmap(i) + 1) * tpb, nh - 1), 0))]
    args = [p, p, p]
    if final:
        in_specs += [pl.BlockSpec((None, tl, D), lambda b, i: (b, tmap(i), 1)), main]
        args += [p, hb]
    in_specs += [pl.BlockSpec((CONV_W, D), lambda b, i: (0, 0)),
                 pl.BlockSpec((1, D), lambda b, i: (0, 0)),
                 pl.BlockSpec((LRU_BLOCKS, LRU_BS, 2 * LRU_BS), lambda b, i: (0, 0, 0)),
                 pl.BlockSpec((2, D), lambda b, i: (0, 0)),
                 pl.BlockSpec((1, D), lambda b, i: (0, 0)),
                 pl.BlockSpec((None, 1, D), lambda b, i: (b, 0, 0))]
    args += [cw, cb, gw, gb, lam, h0]
    out, hlast = pl.pallas_call(
        functools.partial(_lru_kernel, reverse=reverse, final=final, tl=tl, nt=nt),
        grid=(bsz, nt),
        in_specs=in_specs,
        out_specs=[main, pl.BlockSpec((None, 1, D), lambda b, i: (b, 0, 0))],
        out_shape=[jax.ShapeDtypeStruct((bsz, l, D), bf16 if final else f32),
                   jax.ShapeDtypeStruct((bsz, 1, D), f32)],
        scratch_shapes=[pltpu.VMEM((tl + 2 * HALO, D), f32), pltpu.VMEM((tl, D), f32),
                        pltpu.VMEM((tl, D), f32), pltpu.VMEM((tl, D), f32), pltpu.VMEM((1, D), f32)],
        compiler_params=_cparams(2),
        name="lru_fwd" if final else "lru_bwd",
    )(*args)
    return out, hlast


def _rwkv_feat_kernel(p_ref, prev_ref, next_ref, mu_ref, w0_ref, wup_ref, a0_ref, aup_ref, gup_ref,
                      kk_ref, ka_ref, rk_ref, ones_ref,
                      r_out, v_out, kkv_out, g_out, bonus_out, lw_out, key_out, b_out,
                      pe_ref, *, tl, nt):
    i = pl.program_id(1)
    zero = jnp.zeros((HALO, N_RWKV_PAD), f32)
    pe_ref[0:HALO, :] = jnp.where(i > 0, prev_ref[...], zero)
    pe_ref[HALO:HALO + tl, :] = p_ref[...]
    pe_ref[HALO + tl:2 * HALO + tl, :] = jnp.where(i < nt - 1, next_ref[...], zero)

    def shifted(c0, width):
        cs = slice(c0, c0 + width)
        p = pe_ref[HALO:HALO + tl, cs]
        nb = 0.5 * (pe_ref[HALO - 1:HALO - 1 + tl, cs] + pe_ref[HALO + 1:HALO + 1 + tl, cs])
        return p + mu_ref[:, cs] * (nb - p)

    win_w = jnp.tanh(shifted(RW_OFF_W, 256)).astype(bf16)
    win_a = shifted(RW_OFF_A, 256).astype(bf16)
    win_g = _sigmoid(shifted(RW_OFF_G, 256)).astype(bf16)
    ones_bd = ones_ref[...]

    for g in range(RW_NG):
        cs = slice(g * RW_GW, (g + 1) * RW_GW)
        r = shifted(g * RW_GW, RW_GW)
        k = shifted(D + g * RW_GW, RW_GW)
        v = shifted(2 * D + g * RW_GW, RW_GW)
        kf = k * kk_ref[:, cs]
        kk = kf * lax.rsqrt(_mm_x2(kf * kf, ones_bd) + 1e-12)
        ksum = None
        for d in range(2):
            wv = -_softplus(-(w0_ref[d:d + 1, cs] + jnp.dot(win_w, wup_ref[d, :, cs],
                                                              preferred_element_type=f32))) - 0.5
            lw_out[d, :, cs] = -jnp.exp(wv)
            a = _sigmoid(a0_ref[d:d + 1, cs] + jnp.dot(win_a, aup_ref[d, :, cs],
                                                       preferred_element_type=f32))
            key = k * (1.0 + (a - 1.0) * ka_ref[:, cs])
            key_out[d, :, cs] = key
            b_out[d, :, cs] = kk * a
            ksum = key if ksum is None else ksum + key
        r_out[:, cs] = r
        v_out[:, cs] = v
        kkv_out[:, cs] = kk
        bonus_out[:, cs] = _mm_x2(r * ksum * rk_ref[:, cs], ones_bd) * v
        g_out[:, cs] = jnp.dot(win_g, gup_ref[:, cs], preferred_element_type=f32)


def _rwkv_feat(p, prm):
    bsz, l, _ = p.shape
    tl = min(128, l)
    nt = l // tl
    nh = l // HALO
    tpb = tl // HALO
    full2 = lambda shape: pl.BlockSpec(shape, lambda b, i: (0,) * len(shape))
    tok = pl.BlockSpec((None, tl, D), lambda b, i: (b, i, 0))
    tok2 = pl.BlockSpec((2, None, tl, D), lambda b, i: (0, b, i, 0))
    sd = jax.ShapeDtypeStruct((bsz, l, D), f32)
    sd2 = jax.ShapeDtypeStruct((2, bsz, l, D), f32)
    return pl.pallas_call(
        functools.partial(_rwkv_feat_kernel, tl=tl, nt=nt),
        grid=(bsz, nt),
        in_specs=[pl.BlockSpec((None, tl, N_RWKV_PAD), lambda b, i: (b, i, 0)),
                  pl.BlockSpec((None, HALO, N_RWKV_PAD), lambda b, i: (b, jnp.maximum(i * tpb - 1, 0), 0)),
                  pl.BlockSpec((None, HALO, N_RWKV_PAD),
                               lambda b, i: (b, jnp.minimum((i + 1) * tpb, nh - 1), 0)),
                  full2((1, N_RWKV_PAD)), full2((2, D)), full2((2, 256, D)), full2((2, D)),
                  full2((2, 256, D)), full2((256, D)), full2((1, D)), full2((1, D)), full2((1, D)),
                  full2((RW_GW, RW_GW))],
        out_specs=[tok, tok, tok, tok, tok, tok2, tok2, tok2],
        out_shape=[sd, sd, sd, sd, sd, sd2, sd2, sd2],
        scratch_shapes=[pltpu.VMEM((tl + 2 * HALO, N_RWKV_PAD), f32)],
        compiler_params=_cparams(2),
        name="rwkv_feat",
    )(p, p, p, prm["mu"], prm["w0"], prm["wup"], prm["a0"], prm["aup"], prm["gup"],
      prm["k_k"], prm["k_a"], prm["r_k"], prm["ones_bd"])


def _bd_expand(y, lane_head):
    yb = y.astype(f32)
    return jnp.concatenate([jnp.where(lane_head == h, yb, 0.0).astype(bf16) for h in range(4)], axis=0)


def _rwkv_scan_kernel(r_ref, v_ref, kk_ref, lw_ref, key_ref, b_ref, s0_ref, y_ref, sfin_ref,
                      s_ref, *, nc):
    c = RW_CHUNK
    d = pl.program_id(0)
    i = pl.program_id(2)
    fwd = d == 0

    @pl.when(i == 0)
    def _():
        s_ref[...] = s0_ref[...]

    row = _iota((c, c), 0)
    col = _iota((c, c), 1)
    sgn = jnp.where(fwd, 1, -1)
    tri = jnp.where((col - row) * sgn <= 0, 1.0, 0.0).astype(bf16)
    t4 = _iota((c, 4 * c), 0)
    j4 = _iota((c, 4 * c), 1) % c
    mask_s = (j4 - t4) * sgn < 0
    mask_i = (j4 - t4) * sgn <= 0
    eye4 = jnp.where(j4 == t4, 1.0, 0.0)
    lane_head = _iota((c, RW_GW), 1) // HEAD
    bd_mask = (_iota((RW_GW, RW_GW), 0) // HEAD) == (_iota((RW_GW, RW_GW), 1) // HEAD)

    def mmbd(x, y):
        return jnp.dot(x.astype(bf16), _bd_expand(y, lane_head), preferred_element_type=f32)

    groups = range(RW_NG)
    sl = [slice(g * RW_GW, (g + 1) * RW_GW) for g in groups]

    def state_free(rows):
        lw = [lw_ref[rows, sl[g]] for g in groups]
        cl = [_mm_e3(tri, lw[g]) for g in groups]
        tot = [jnp.sum(lw[g], axis=0, keepdims=True) for g in groups]
        v = [v_ref[rows, sl[g]] for g in groups]
        ar, a_b, a_k, bk = [], [], [], []
        for g in groups:
            g_inv = jnp.exp(-cl[g])
            g_end = jnp.exp(tot[g] - cl[g])
            bv = b_ref[rows, sl[g]]
            key = key_ref[rows, sl[g]]
            at = -kk_ref[rows, sl[g]] * jnp.exp(cl[g] - lw[g])
            rt = r_ref[rows, sl[g]] * jnp.exp(cl[g])
            ar.append(jnp.concatenate([at, rt], axis=0).astype(bf16))
            a_b.append(lax.dot_general(ar[g], _bd_expand(bv * g_inv, lane_head), (((1,), (1,)), ((), ())),
                                       preferred_element_type=f32))
            a_k.append(lax.dot_general(ar[g], _bd_expand(key * g_inv, lane_head), (((1,), (1,)), ((), ())),
                                       preferred_element_type=f32))
            bk.append(jnp.concatenate([bv * g_end, key * g_end], axis=0).astype(bf16))
        n_ab = [jnp.where(mask_s, a_b[g][:c], 0.0) for g in groups]
        a_rb = [jnp.where(mask_i, a_b[g][c:], 0.0) for g in groups]
        a_kk = [jnp.concatenate([jnp.where(mask_s, a_k[g][:c], 0.0), jnp.where(mask_i, a_k[g][c:], 0.0)],
                                axis=0) for g in groups]
        x = [eye4 + n_ab[g] for g in groups]
        m = [mmbd(n_ab[g], n_ab[g]) for g in groups]
        lvl = 2
        while lvl < c:
            if lvl * 2 < c:
                xm = [mmbd(jnp.concatenate([x[g], m[g]], axis=0), m[g]) for g in groups]
                x = [x[g] + xm[g][:c] for g in groups]
                m = [xm[g][c:] for g in groups]
            else:
                x = [x[g] + mmbd(x[g], m[g]) for g in groups]
            lvl *= 2
        av = [mmbd(a_kk[g], v[g]) for g in groups]
        return dict(ar=ar, x=x, av=av, a_rb=a_rb, v=v, bk=bk, tot=tot)

    def state_step(rows, p, s):
        sa = [_mm_nt(p["ar"][g], s[g]) for g in groups]
        u = [mmbd(p["x"][g], sa[g][:c] + p["av"][g][:c]) for g in groups]
        y = [sa[g][c:] + p["av"][g][c:] + mmbd(p["a_rb"][g], u[g]) for g in groups]
        upd = [_mm(jnp.concatenate([u[g], p["v"][g]], axis=0).T, p["bk"][g]) for g in groups]
        for g in groups:
            y_ref[rows, sl[g]] = y[g]
        return [s[g] * jnp.exp(p["tot"][g]) + jnp.where(bd_mask, upd[g], 0.0) for g in groups]

    first = pl.multiple_of(jnp.where(fwd, 0, c), c)
    chunk_rows = [pl.ds(first, c), pl.ds(pl.multiple_of(c - first, c), c)]
    parts = [state_free(rows) for rows in chunk_rows]
    s = [s_ref[g] for g in groups]
    for rows, p in zip(chunk_rows, parts):
        s = state_step(rows, p, s)
    for g in groups:
        s_ref[g] = s[g]

    @pl.when(i == nc - 1)
    def _():
        sfin_ref[...] = s_ref[...]


def _rwkv_scan(r, v, kk, lw, key, bvec, s0):
    bsz, l, _ = r.shape
    c = 2 * RW_CHUNK
    nc = l // c

    def cidx(d, i):
        return jnp.where(d == 0, i, nc - 1 - i)

    tok = pl.BlockSpec((None, c, D), lambda d, b, i: (b, cidx(d, i), 0))
    tok2 = pl.BlockSpec((None, None, c, D), lambda d, b, i: (d, b, cidx(d, i), 0))
    st = pl.BlockSpec((None, None, RW_NG, RW_GW, RW_GW), lambda d, b, i: (d, b, 0, 0, 0))
    return pl.pallas_call(
        functools.partial(_rwkv_scan_kernel, nc=nc),
        grid=(2, bsz, nc),
        in_specs=[tok, tok, tok, tok2, tok2, tok2, st],
        out_specs=[tok2, st],
        out_shape=[jax.ShapeDtypeStruct((2, bsz, l, D), f32),
                   jax.ShapeDtypeStruct((2, bsz, RW_NG, RW_GW, RW_GW), f32)],
        scratch_shapes=[pltpu.VMEM((RW_NG, RW_GW, RW_GW), f32)],
        compiler_params=_cparams(3),
        name="rwkv_scan",
    )(r, v, kk, lw, key, bvec, s0)


def _rwkv_out_kernel(y_ref, bonus_ref, g_ref, lnw_ref, lnb_ref, ones_ref, o_ref):
    ones_bd = ones_ref[...]
    for g in range(RW_NG):
        cs = slice(g * RW_GW, (g + 1) * RW_GW)
        y = y_ref[0, :, cs] + y_ref[1, :, cs]
        mean = _mm_x2(y, ones_bd) * (1.0 / HEAD)
        yc = y - mean
        var = _mm_x2(yc * yc, ones_bd) * (1.0 / HEAD)
        yn = yc * lax.rsqrt(var + GN_EPS) * lnw_ref[:, cs] + lnb_ref[:, cs]
        o_ref[:, cs] = ((yn + bonus_ref[:, cs]) * g_ref[:, cs]).astype(o_ref.dtype)


def _rwkv_out(y2, bonus, g, ln_w, ln_b, ones_bd):
    _, bsz, l, _ = y2.shape
    tl = min(256, l)
    tok = pl.BlockSpec((None, tl, D), lambda b, i: (b, i, 0))
    vec = pl.BlockSpec((1, D), lambda b, i: (0, 0))
    return pl.pallas_call(
        _rwkv_out_kernel,
        grid=(bsz, l // tl),
        in_specs=[pl.BlockSpec((2, None, tl, D), lambda b, i: (0, b, i, 0)), tok, tok, vec, vec,
                  pl.BlockSpec((RW_GW, RW_GW), lambda b, i: (0, 0))],
        out_specs=tok,
        out_shape=jax.ShapeDtypeStruct((bsz, l, D), bf16),
        compiler_params=_cparams(2),
        name="rwkv_out",
    )(y2, bonus, g, ln_w, ln_b, ones_bd)


def _ssd_prep_kernel(p_ref, prev_ref, next_ref, cw_ref, cb_ref, o_ref, pe_ref, *, tl, nt):
    i = pl.program_id(1)
    w = p_ref.shape[1]
    zero = jnp.zeros((HALO, w), f32)
    pe_ref[0:HALO, :] = jnp.where(i > 0, prev_ref[...], zero)
    pe_ref[HALO:HALO + tl, :] = p_ref[...]
    pe_ref[HALO + tl:2 * HALO + tl, :] = jnp.where(i < nt - 1, next_ref[...], zero)
    for n in range(w // 128):
        cs = slice(n * 128, (n + 1) * 128)
        xc = cb_ref[:, cs]
        for tap in range(CONV_W):
            r0 = HALO - CONV_LEFT + tap
            xc = xc + pe_ref[r0:r0 + tl, cs] * cw_ref[tap:tap + 1, cs]
        o_ref[:, cs] = _silu(xc)


def _ssd_prep(p, cw, cb):
    bsz, l, _ = p.shape
    tl = min(256, l)
    nt = l // tl
    nh = l // HALO
    tpb = tl // HALO
    wc = 1024
    c0 = D // wc
    return pl.pallas_call(
        functools.partial(_ssd_prep_kernel, tl=tl, nt=nt),
        grid=(bsz, nt, D_XBC // wc),
        in_specs=[pl.BlockSpec((None, tl, wc), lambda b, i, c: (b, i, c + c0)),
                  pl.BlockSpec((None, HALO, wc), lambda b, i, c: (b, jnp.maximum(i * tpb - 1, 0), c + c0)),
                  pl.BlockSpec((None, HALO, wc),
                               lambda b, i, c: (b, jnp.minimum((i + 1) * tpb, nh - 1), c + c0)),
                  pl.BlockSpec((CONV_W, wc), lambda b, i, c: (0, c)),
                  pl.BlockSpec((1, wc), lambda b, i, c: (0, c))],
        out_specs=pl.BlockSpec((None, tl, wc), lambda b, i, c: (b, i, c)),
        out_shape=jax.ShapeDtypeStruct((bsz, l, D_XBC), f32),
        scratch_shapes=[pltpu.VMEM((tl + 2 * HALO, wc), f32)],
        compiler_params=_cparams(3),
        name="ssd_prep",
    )(p, p, p, cw, cb)


def _ssd_kernel(xbc_ref, dtw_ref, dtb_ref, a_ref, dsk_ref, e_ref,
                h0_ref, y_ref, hfin_ref, cumt_ref, dtt_ref, h_ref, *, nc):
    q = SSM_Q
    d = pl.program_id(0)
    i = pl.program_id(2)
    fwd = d == 0

    @pl.when(i == 0)
    def _():
        h_ref[...] = h0_ref[...]

    row = _iota((q, q), 0)
    col = _iota((q, q), 1)
    sgn = jnp.where(fwd, 1, -1)
    low = (col - row) * sgn <= 0
    tri = jnp.where(low, 1.0, 0.0).astype(bf16)
    tri_t = jnp.where((row - col) * sgn <= 0, 1.0, 0.0).astype(bf16)

    dt = _softplus(dtw_ref[...] + dtb_ref[...])
    dta = dt * a_ref[...]
    cum = _mm_e3(tri, dta)
    tot = jnp.sum(dta, axis=0, keepdims=True)
    cumt_ref[...] = _mm_x3(dta.T, tri_t)
    dtt_ref[...] = dt.T
    e_d = e_ref[...]
    dec_e = _mm_x2(jnp.exp(cum), e_d)
    toend_e = _mm_x2(jnp.exp(tot - cum) * dt, e_d)
    tot_e = _mm_x2(jnp.broadcast_to(jnp.exp(tot), (8, 128)), e_d)[0:1]
    cum_d = pltpu.roll(cum, jnp.where(fwd, 0, 128 - SSM_HEADS), 1)
    lane = _iota((q, 128), 1)

    for g in range(SSM_G):
        bg = xbc_ref[:, D + g * SSM_N:D + (g + 1) * SSM_N]
        cg = xbc_ref[:, D + SSM_G * SSM_N + g * SSM_N:D + SSM_G * SSM_N + (g + 1) * SSM_N]
        cb = _mm_nt(cg, bg)
        gs = slice(g * 512, (g + 1) * 512)
        hg = h_ref[g]
        y_off = _mm(cg, hg) * dec_e[:, gs]
        for pr in range(4):
            ls = []
            for hh in range(2):
                h = g * 8 + pr * 2 + hh
                rowv = cumt_ref[pl.ds(d * 32 + h, 1), :]
                dtr = dtt_ref[pl.ds(d * 32 + h, 1), :]
                seg = cum_d[:, h:h + 1] - rowv
                ls.append(jnp.where(low, jnp.exp(jnp.where(low, seg, 0.0)), 0.0) * cb * dtr)
            lp = jnp.concatenate(ls, axis=1)
            ps = slice(g * 512 + pr * 128, g * 512 + (pr + 1) * 128)
            xp = xbc_ref[:, ps]
            bd2 = jnp.concatenate([jnp.where(lane < SSM_P, xp, 0.0), jnp.where(lane >= SSM_P, xp, 0.0)],
                                  axis=0)
            yd = _mm(lp, bd2)
            y_ref[:, ps] = yd + y_off[:, pr * 128:(pr + 1) * 128] + dsk_ref[:, ps] * xp
        xs = xbc_ref[:, gs] * toend_e[:, gs]
        states = _mm(bg.T, xs)
        h_ref[g] = hg * tot_e[:, gs] + states

    @pl.when(i == nc - 1)
    def _():
        hfin_ref[...] = h_ref[...]


def _ssd(p, prm, h0):
    bsz, l, _ = p.shape
    q = SSM_Q
    nc = l // q
    xbc = _ssd_prep(p, prm["cw"], prm["cb"])

    def cidx(d, i):
        return jnp.where(d == 0, i, nc - 1 - i)

    full = lambda shape: pl.BlockSpec(shape, lambda d, b, i: (0,) * len(shape))
    st = pl.BlockSpec((None, None, SSM_G, SSM_N, 8 * SSM_P), lambda d, b, i: (d, b, 0, 0, 0))
    return pl.pallas_call(
        functools.partial(_ssd_kernel, nc=nc),
        grid=(2, bsz, nc),
        in_specs=[pl.BlockSpec((None, q, D_XBC), lambda d, b, i: (b, cidx(d, i), 0)),
                  pl.BlockSpec((None, q, 128), lambda d, b, i: (b, cidx(d, i), SS_OFF_DT // 128)),
                  full((1, 128)), full((1, 128)),
                  pl.BlockSpec((None, 1, D), lambda d, b, i: (d, 0, 0)),
                  pl.BlockSpec((None, 128, D), lambda d, b, i: (d, 0, 0)),
                  st],
        out_specs=[pl.BlockSpec((None, None, q, D), lambda d, b, i: (d, b, cidx(d, i), 0)), st],
        out_shape=[jax.ShapeDtypeStruct((2, bsz, l, D), f32),
                   jax.ShapeDtypeStruct((2, bsz, SSM_G, SSM_N, 8 * SSM_P), f32)],
        scratch_shapes=[pltpu.VMEM((128, q), f32), pltpu.VMEM((128, q), f32),
                        pltpu.VMEM((SSM_G, SSM_N, 8 * SSM_P), f32)],
        compiler_params=_cparams(3),
        name="ssd",
    )(xbc, p, prm["dtb"], prm["a"], prm["dsk"], prm["e"], h0)


def _ssd_out_kernel(y_ref, z_ref, nw_ref, o_ref):
    gw = D // SSM_G
    for g in range(SSM_G):
        cs = slice(g * gw, (g + 1) * gw)
        yg = (y_ref[0, :, cs] + y_ref[1, :, cs]) * _silu(z_ref[:, cs])
        ms = jnp.mean(yg * yg, axis=-1, keepdims=True)
        o_ref[:, cs] = (yg * lax.rsqrt(ms + EPS) * nw_ref[:, cs]).astype(o_ref.dtype)


def _ssd_out(y2, p, norm_w):
    _, bsz, l, _ = y2.shape
    tl = min(256, l)
    tok = pl.BlockSpec((None, tl, D), lambda b, i: (b, i, 0))
    return pl.pallas_call(
        _ssd_out_kernel,
        grid=(bsz, l // tl),
        in_specs=[pl.BlockSpec((2, None, tl, D), lambda b, i: (0, b, i, 0)), tok,
                  pl.BlockSpec((1, D), lambda b, i: (0, 0))],
        out_specs=tok,
        out_shape=jax.ShapeDtypeStruct((bsz, l, D), bf16),
        compiler_params=_cparams(2),
        name="ssd_out",
    )(y2, p, norm_w)


def _merge_kernel(ya_ref, yb_ref, yc_ref, sa_ref, sb_ref, sc_ref, wa_ref, wb_ref, wc_ref, o_ref):
    acc = None
    for y_ref, s_ref, w_ref in ((ya_ref, sa_ref, wa_ref), (yb_ref, sb_ref, wb_ref), (yc_ref, sc_ref, wc_ref)):
        t = s_ref[...].astype(f32) * jnp.dot(y_ref[...], w_ref[...], preferred_element_type=f32)
        acc = t if acc is None else acc + t
    o_ref[...] = acc.astype(o_ref.dtype)


def _merge(ys, sig, ws):
    m = ys[0].shape[0]
    tm = min(1024, m)
    tn = 512
    nj = D // tn
    ysp = pl.BlockSpec((tm, D), lambda i, j: (i, 0))
    wsp = pl.BlockSpec((D, tn), lambda i, j: (0, j))
    ssp = [pl.BlockSpec((tm, tn), functools.partial(lambda i, j, k: (i, k * nj + j), k=k)) for k in range(3)]
    return pl.pallas_call(
        _merge_kernel,
        grid=(m // tm, nj),
        in_specs=[ysp, ysp, ysp] + ssp + [wsp, wsp, wsp],
        out_specs=pl.BlockSpec((tm, tn), lambda i, j: (i, j)),
        out_shape=jax.ShapeDtypeStruct((m, D), bf16),
        compiler_params=_cparams(2),
        name="merge",
    )(ys[0], ys[1], ys[2], sig, sig, sig, ws[0], ws[1], ws[2])


def _wo_kernel(m_ref, w_ref, h_ref, g_ref, o_ref):
    o_ref[...] = h_ref[...] + g_ref[...] * jnp.dot(m_ref[...], w_ref[...], preferred_element_type=f32)


def _wo_residual(m, w_o, h, mod4, row_of_batch, k_gate, *, transposed):
    bsz, l, _ = h.shape
    if transposed:
        rows = l // GRID_W
        hv = h.reshape(bsz, rows, GRID_W * D)
        tl, nt = rows, GRID_W
        hspec = pl.BlockSpec((None, tl, D), lambda b, i: (b, 0, i))
        oshape = jax.ShapeDtypeStruct((bsz, rows, GRID_W * D), f32)
    else:
        tl = min(512, l)
        nt = l // tl
        hv = h
        hspec = pl.BlockSpec((None, tl, D), lambda b, i: (b, i, 0))
        oshape = jax.ShapeDtypeStruct((bsz, l, D), f32)
    out = pl.pallas_call(
        _wo_kernel,
        grid=(bsz, nt),
        in_specs=[pl.BlockSpec((None, tl, D), lambda b, i: (b, i, 0)),
                  pl.BlockSpec((D, D), lambda b, i: (0, 0)),
                  hspec,
                  pl.BlockSpec((None, None, 1, D), lambda b, i: (row_of_batch(b), k_gate, 0, 0))],
        out_specs=hspec,
        out_shape=oshape,
        compiler_params=_cparams(2),
        name="wo_residual",
    )(m, w_o, hv, mod4)
    return out.reshape(bsz, l, D)


def _router_kernel(x_ref, g_ref, sc_ref, sh_ref, r_ref, u_ref, gates_ref):
    x = x_ref[...]
    ms = jnp.mean(x * x, axis=-1, keepdims=True)
    u = (x * lax.rsqrt(ms + EPS) * g_ref[...]) * (1.0 + sc_ref[...]) + sh_ref[...]
    u_ref[...] = u.astype(u_ref.dtype)
    u0, u1, u2 = _split3(u)
    r0, r1, r2 = _split3(r_ref[...])
    dot = lambda a, b: jnp.dot(a, b, preferred_element_type=f32)
    logits = (dot(u0, r0) + (dot(u0, r1) + dot(u1, r0))
              + (dot(u1, r1) + dot(u0, r2) + dot(u2, r0)))
    lane = _iota(logits.shape, 1)
    neg = jnp.float32(-jnp.inf)
    lg = jnp.where(lane < N_EXPERTS, logits, neg)
    m1 = jnp.max(lg, axis=-1, keepdims=True)
    i1 = jnp.min(jnp.where(lg == m1, lane, 128), axis=-1, keepdims=True)
    lg2 = jnp.where(lane == i1, neg, lg)
    m2 = jnp.max(lg2, axis=-1, keepdims=True)
    i2 = jnp.min(jnp.where(lg2 == m2, lane, 128), axis=-1, keepdims=True)
    e2 = jnp.exp(m2 - m1)
    den = 1.0 + e2
    gates_ref[...] = jnp.where(lane == i1, 1.0 / den, 0.0) + jnp.where(lane == i2, e2 / den, 0.0)


def _norm_router(h, gain, mod4, row_of_batch, k_sc, k_sh, router_pad, out_dtype=bf16):
    bsz, l, _ = h.shape
    tl = min(512, l)
    tok = pl.BlockSpec((None, tl, D), lambda b, i: (b, i, 0))
    return pl.pallas_call(
        _router_kernel,
        grid=(bsz, l // tl),
        in_specs=[tok,
                  pl.BlockSpec((1, D), lambda b, i: (0, 0)),
                  pl.BlockSpec((None, None, 1, D), lambda b, i: (row_of_batch(b), k_sc, 0, 0)),
                  pl.BlockSpec((None, None, 1, D), lambda b, i: (row_of_batch(b), k_sh, 0, 0)),
                  pl.BlockSpec((D, 128), lambda b, i: (0, 0))],
        out_specs=[tok, pl.BlockSpec((None, tl, 128), lambda b, i: (b, i, 0))],
        out_shape=[jax.ShapeDtypeStruct((bsz, l, D), out_dtype), jax.ShapeDtypeStruct((bsz, l, 128), f32)],
        compiler_params=_cparams(2),
        name="norm_router",
    )(h, gain.reshape(1, D), mod4, mod4, router_pad)


def _up_kernel(*refs, gated):
    if gated:
        u_ref, w1_ref, w3_ref, gt_ref, o_ref, w1b_ref, w3b_ref = refs
    else:
        u_ref, w1_ref, w3_ref, o_ref, w1b_ref, w3b_ref = refs

    @pl.when(pl.program_id(2) == 0)
    def _():
        w1b_ref[...] = w1_ref[...].astype(bf16)
        w3b_ref[...] = w3_ref[...].astype(bf16)

    u = u_ref[...]
    a = jnp.dot(u, w1b_ref[...], preferred_element_type=f32)
    b = jnp.dot(u, w3b_ref[...], preferred_element_type=f32)
    act = _silu(a) * b
    if gated:
        e = pl.program_id(0)
        gt = gt_ref[...]
        lane = _iota(gt.shape, 1)
        act = act * jnp.sum(jnp.where(lane == e, gt, 0.0), axis=-1, keepdims=True)
    o_ref[...] = act.astype(o_ref.dtype)


def _swiglu_up(u, w1, w3, lead, gates=None):
    m = u.shape[0]
    _, ne, _, ff = w1.shape
    tm = min(2048, m)
    tn = 512
    nj = ff // tn
    gated = gates is not None
    wspec = pl.BlockSpec((None, None, D, tn), lambda e, j, i: (lead, e, 0, j))
    in_specs = [pl.BlockSpec((tm, D), lambda e, j, i: (i, 0)), wspec, wspec]
    args = [u, w1, w3]
    if gated:
        in_specs.append(pl.BlockSpec((tm, 128), lambda e, j, i: (i, 0)))
        args.append(gates)
    return pl.pallas_call(
        functools.partial(_up_kernel, gated=gated),
        grid=(ne, nj, m // tm),
        in_specs=in_specs,
        out_specs=pl.BlockSpec((tm, tn), lambda e, j, i: (i, e * nj + j)),
        out_shape=jax.ShapeDtypeStruct((m, ne * ff), bf16),
        scratch_shapes=[pltpu.VMEM((D, tn), bf16), pltpu.VMEM((D, tn), bf16)],
        compiler_params=_cparams(3),
        name="swiglu_up",
    )(*args)


def _down_kernel(a_ref, w_ref, h_ref, g_ref, o_ref, acc_ref, *, nk):
    k = pl.program_id(3)

    @pl.when(k == 0)
    def _():
        acc_ref[...] = jnp.zeros_like(acc_ref)

    acc_ref[...] += jnp.dot(a_ref[...], w_ref[...].astype(bf16), preferred_element_type=f32)

    @pl.when(k == nk - 1)
    def _():
        o_ref[...] = h_ref[...] + g_ref[...] * acc_ref[...]


def _down_residual(act, w2, h, mod4, row_of_batch, k_gate):
    bsz, l, kk = act.shape
    tl = min(1024, l)
    tn = 1024
    tk = 2048
    nk = kk // tk
    return pl.pallas_call(
        functools.partial(_down_kernel, nk=nk),
        grid=(bsz, l // tl, D // tn, nk),
        in_specs=[pl.BlockSpec((None, tl, tk), lambda b, i, j, k: (b, i, k)),
                  pl.BlockSpec((tk, tn), lambda b, i, j, k: (k, j)),
                  pl.BlockSpec((None, tl, tn), lambda b, i, j, k: (b, i, j)),
                  pl.BlockSpec((None, None, 1, tn), lambda b, i, j, k: (row_of_batch(b), k_gate, 0, j))],
        out_specs=pl.BlockSpec((None, tl, tn), lambda b, i, j, k: (b, i, j)),
        out_shape=jax.ShapeDtypeStruct((bsz, l, D), f32),
        scratch_shapes=[pltpu.VMEM((tl, tn), f32)],
        compiler_params=_cparams(4),
        name="down_residual",
    )(act, w2, h, mod4)


MOE_TM = 1024
MOE_TT = 512
NACT_LANE = 127


def _moe_count_kernel(gates_ref, pre_ref, cnt_ref, carry_ref):
    i = pl.program_id(0)

    @pl.when(i == 0)
    def _():
        carry_ref[...] = jnp.zeros_like(carry_ref)

    tt = gates_ref.shape[0]
    a = jnp.where(gates_ref[...] > 0.0, 1.0, 0.0)
    strict = jnp.where(_iota((tt, tt), 1) < _iota((tt, tt), 0), 1.0, 0.0).astype(bf16)
    carry = carry_ref[0:1, :]
    pre_ref[...] = jnp.dot(strict, a.astype(bf16), preferred_element_type=f32) + carry
    carry_ref[...] = jnp.broadcast_to(carry + jnp.sum(a, axis=0, keepdims=True), carry_ref.shape)
    cnt_ref[...] = carry_ref[...]


def _moe_place_kernel(gates_ref, pre_ref, cnt_ref, pos_ref, gv_ref, tile_ref, *, n_tiles):
    tt = gates_ref.shape[0]
    tm = float(MOE_TM)
    lane1 = _iota((1, 128), 1)
    cnt = cnt_ref[0:1, :]
    gsz = jnp.floor((cnt + (tm - 1.0)) * (1.0 / tm)) * tm
    upper = jnp.where(_iota((128, 128), 0) < _iota((128, 128), 1), 1.0, 0.0).astype(bf16)
    base = _mm_x3(jnp.broadcast_to(gsz, (8, 128)), upper)[0:1]
    gates = gates_ref[...]
    act = gates > 0.0
    pos = base + pre_ref[...]
    big = jnp.float32(1e9)
    p_lo = jnp.min(jnp.where(act, pos, big), axis=-1, keepdims=True)
    p_hi = jnp.max(jnp.where(act, pos, -1.0), axis=-1, keepdims=True)
    g_lo = jnp.sum(jnp.where(act & (pos == p_lo), gates, 0.0), axis=-1, keepdims=True)
    g_hi = jnp.where(p_hi != p_lo,
                     jnp.sum(jnp.where(act & (pos == p_hi), gates, 0.0), axis=-1, keepdims=True), 0.0)
    lane = _iota((tt, 128), 1)
    posf = jnp.where(lane == 0, p_lo, 0.0) + jnp.where(lane == 1, p_hi, 0.0)
    pos_ref[...] = posf.T[0:8].astype(jnp.int32)
    gv_ref[...] = jnp.where(lane == 0, g_lo, 0.0) + jnp.where(lane == 1, g_hi, 0.0)
    end = base + gsz
    start_j = lane1.astype(f32) * tm
    texp = jnp.zeros((1, 128), f32)
    for e in range(N_EXPERTS - 1):
        end_e = jnp.sum(jnp.where(lane1 == e, end, 0.0), axis=-1, keepdims=True)
        texp = texp + jnp.where(start_j >= end_e, 1.0, 0.0)
    total = jnp.sum(jnp.where(lane1 == N_EXPERTS - 1, end, 0.0), axis=-1, keepdims=True)
    table = jnp.where(lane1 == NACT_LANE, total * (1.0 / tm), texp)
    tile_ref[...] = jnp.broadcast_to(table, (8, 128)).astype(jnp.int32)


def _moe_scatter_kernel(pos_ref, v_ref, xs_in_ref, xs_ref, sem, *, dump_base):
    del xs_in_ref
    tt = v_ref.shape[0]

    def row_copy(r, dst_row):
        return pltpu.make_async_copy(v_ref.at[pl.ds(r, 1)], xs_ref.at[pl.ds(dst_row, 1)], sem)

    def body(r, carry):
        p0 = pos_ref[0, r]
        p1 = pos_ref[1, r]
        p1 = jnp.where(p1 == p0, dump_base + r, p1)
        row_copy(r, p0).start(priority=0)
        row_copy(r, p1).start(priority=1)
        return carry

    lax.fori_loop(0, tt, body, 0, unroll=8)
    for _ in range(2):
        pltpu.make_async_copy(v_ref, xs_ref.at[pl.ds(0, tt)], sem).wait()


def _new_expert_tile(tile_ref, t):
    return (t == 0) | (tile_ref[t] != tile_ref[jnp.maximum(t - 1, 0)])


def _moe_gup_kernel(tile_ref, x_ref, w1_ref, w3_ref, o_ref, w1b_ref, w3b_ref):
    t = pl.program_id(1)
    active = t < tile_ref[NACT_LANE]

    @pl.when(active & _new_expert_tile(tile_ref, t))
    def _():
        w1b_ref[...] = w1_ref[...].astype(bf16)
        w3b_ref[...] = w3_ref[...].astype(bf16)

    @pl.when(active)
    def _():
        x = x_ref[...].astype(bf16)
        a = jnp.dot(x, w1b_ref[...], preferred_element_type=f32)
        b = jnp.dot(x, w3b_ref[...], preferred_element_type=f32)
        o_ref[...] = (_silu(a) * b).astype(o_ref.dtype)

    @pl.when(jnp.logical_not(active))
    def _():
        o_ref[...] = jnp.zeros_like(o_ref)


def _moe_gdown_kernel(tile_ref, a_ref, w_ref, o_ref, wb_ref):
    t = pl.program_id(1)
    active = t < tile_ref[NACT_LANE]

    @pl.when(active & _new_expert_tile(tile_ref, t))
    def _():
        wb_ref[...] = w_ref[...].astype(bf16)

    @pl.when(active)
    def _():
        o_ref[...] = jnp.dot(a_ref[...], wb_ref[...], preferred_element_type=f32)

    @pl.when(jnp.logical_not(active))
    def _():
        o_ref[...] = jnp.zeros_like(o_ref)


def _moe_combine_kernel(pos_ref, gv_ref, h_ref, gf_ref, ys_ref, o_ref, buf0, buf1, sem):
    tt = h_ref.shape[0]

    def row_copy(src_row, buf, r):
        return pltpu.make_async_copy(ys_ref.at[pl.ds(src_row, 1)], buf.at[pl.ds(r, 1)], sem)

    def body(r, carry):
        row_copy(pos_ref[0, r], buf0, r).start(priority=0)
        row_copy(pos_ref[1, r], buf1, r).start(priority=1)
        return carry

    lax.fori_loop(0, tt, body, 0, unroll=8)
    for buf in (buf0, buf1):
        pltpu.make_async_copy(ys_ref.at[pl.ds(0, tt)], buf, sem).wait()
    gv = gv_ref[...]
    lane = _iota(gv.shape, 1)
    g0 = jnp.sum(jnp.where(lane == 0, gv, 0.0), axis=-1, keepdims=True)
    g1 = jnp.sum(jnp.where(lane == 1, gv, 0.0), axis=-1, keepdims=True)
    o_ref[...] = h_ref[...] + gf_ref[...] * (g0 * buf0[...] + g1 * buf1[...])


def _moe_sparse(v32, gates, h, w1, w3, w2, lead, mod4, row_of_batch, k_gate):
    bsz, l, _ = h.shape
    n_tok = bsz * l
    tt = min(MOE_TT, n_tok)
    n_tt = n_tok // tt
    n_tiles = (2 * n_tok) // MOE_TM + N_EXPERTS
    rows = n_tiles * MOE_TM
    ff = w1.shape[3]
    vf = v32.reshape(n_tok, D)
    gf = gates.reshape(n_tok, 128)

    pre, cnt = pl.pallas_call(
        _moe_count_kernel,
        grid=(n_tt,),
        in_specs=[pl.BlockSpec((tt, 128), lambda i: (i, 0))],
        out_specs=[pl.BlockSpec((tt, 128), lambda i: (i, 0)), pl.BlockSpec((8, 128), lambda i: (0, 0))],
        out_shape=[jax.ShapeDtypeStruct((n_tok, 128), f32), jax.ShapeDtypeStruct((8, 128), f32)],
        scratch_shapes=[pltpu.VMEM((8, 128), f32)],
        compiler_params=_cparams(1),
        name="moe_count",
    )(gf)

    pos, gv, table = pl.pallas_call(
        functools.partial(_moe_place_kernel, n_tiles=n_tiles),
        grid=(n_tt,),
        in_specs=[pl.BlockSpec((tt, 128), lambda i: (i, 0)), pl.BlockSpec((tt, 128), lambda i: (i, 0)),
                  pl.BlockSpec((8, 128), lambda i: (0, 0))],
        out_specs=[pl.BlockSpec((8, tt), lambda i: (0, i)), pl.BlockSpec((tt, 128), lambda i: (i, 0)),
                   pl.BlockSpec((8, 128), lambda i: (0, 0))],
        out_shape=[jax.ShapeDtypeStruct((8, n_tok), jnp.int32), jax.ShapeDtypeStruct((n_tok, 128), f32),
                   jax.ShapeDtypeStruct((8, 128), jnp.int32)],
        compiler_params=_cparams(1),
        name="moe_place",
    )(gf, pre, cnt)
    tile_tab = table[0]

    smem_pos = pl.BlockSpec((8, tt), lambda i: (0, i), memory_space=pltpu.SMEM)
    xs = pl.pallas_call(
        functools.partial(_moe_scatter_kernel, dump_base=rows),
        grid=(n_tt,),
        in_specs=[smem_pos, pl.BlockSpec((tt, D), lambda i: (i, 0)), pl.BlockSpec(memory_space=pl.ANY)],
        out_specs=pl.BlockSpec(memory_space=pl.ANY),
        out_shape=jax.ShapeDtypeStruct((rows + tt, D), f32),
        scratch_shapes=[pltpu.SemaphoreType.DMA],
        input_output_aliases={2: 0},
        compiler_params=_cparams(1),
        name="moe_scatter",
    )(pos, vf, jnp.zeros((rows + tt, D), f32))

    assert n_tiles < NACT_LANE
    tn = 512
    nj = ff // tn
    last_tile = lambda tab: jnp.maximum(tab[NACT_LANE] - 1, 0)
    act = pl.pallas_call(
        _moe_gup_kernel,
        grid_spec=pltpu.PrefetchScalarGridSpec(
            num_scalar_prefetch=1,
            grid=(nj, n_tiles),
            in_specs=[pl.BlockSpec((MOE_TM, D), lambda j, t, tab: (jnp.minimum(t, last_tile(tab)), 0)),
                      pl.BlockSpec((None, None, D, tn),
                                   lambda j, t, tab: (lead, tab[jnp.minimum(t, last_tile(tab))], 0, j)),
                      pl.BlockSpec((None, None, D, tn),
                                   lambda j, t, tab: (lead, tab[jnp.minimum(t, last_tile(tab))], 0, j))],
            out_specs=pl.BlockSpec((MOE_TM, tn), lambda j, t, tab: (t, j)),
            scratch_shapes=[pltpu.VMEM((D, tn), bf16), pltpu.VMEM((D, tn), bf16)]),
        out_shape=jax.ShapeDtypeStruct((rows, ff), bf16),
        compiler_params=_cparams(2),
        name="moe_up",
    )(tile_tab, xs, w1, w3)

    tnd = 512
    ys = pl.pallas_call(
        _moe_gdown_kernel,
        grid_spec=pltpu.PrefetchScalarGridSpec(
            num_scalar_prefetch=1,
            grid=(D // tnd, n_tiles),
            in_specs=[pl.BlockSpec((MOE_TM, ff), lambda j, t, tab: (jnp.minimum(t, last_tile(tab)), 0)),
                      pl.BlockSpec((None, None, ff, tnd),
                                   lambda j, t, tab: (lead, tab[jnp.minimum(t, last_tile(tab))], 0, j))],
            out_specs=pl.BlockSpec((MOE_TM, tnd), lambda j, t, tab: (t, j)),
            scratch_shapes=[pltpu.VMEM((ff, tnd), bf16)]),
        out_shape=jax.ShapeDtypeStruct((rows, D), f32),
        compiler_params=_cparams(2),
        name="moe_down",
    )(tile_tab, act, w2)

    tpb = l // tt if l >= tt else 1
    out = pl.pallas_call(
        _moe_combine_kernel,
        grid=(n_tt,),
        in_specs=[smem_pos, pl.BlockSpec((tt, 128), lambda i: (i, 0)), pl.BlockSpec((tt, D), lambda i: (i, 0)),
                  pl.BlockSpec((None, None, 1, D), lambda i: (row_of_batch(i // tpb), k_gate, 0, 0)),
                  pl.BlockSpec(memory_space=pl.ANY)],
        out_specs=pl.BlockSpec((tt, D), lambda i: (i, 0)),
        out_shape=jax.ShapeDtypeStruct((n_tok, D), f32),
        scratch_shapes=[pltpu.VMEM((tt, D), f32), pltpu.VMEM((tt, D), f32), pltpu.SemaphoreType.DMA],
        compiler_params=_cparams(1),
        name="moe_combine",
    )(pos, gv, h.reshape(n_tok, D), mod4, ys)
    return out.reshape(bsz, l, D)


def _block_ones(n, bs):
    idx = jnp.arange(n) // bs
    return (idx[:, None] == idx[None, :]).astype(bf16)


def _pad_cols(w, n):
    return jnp.pad(w, ((0, 0), (0, n - w.shape[1])))


def _rwkv_params(li, mu, w0, w_up, a0, a_up, g_up, k_k, k_a, r_k, ln_w, ln_b):
    wup = jnp.zeros((2, 256, D), f32)
    aup = jnp.zeros((2, 256, D), f32)
    for d in range(2):
        wup = wup.at[d, d * W_LORA:(d + 1) * W_LORA].set(w_up[li, d])
        o = (RW_OFF_G - 2 * A_LORA) - RW_OFF_A + d * A_LORA
        aup = aup.at[d, o:o + A_LORA].set(a_up[li, d])
    return {
        "mu": jnp.pad(mu[li], (0, N_RWKV_PAD - N_RWKV)).reshape(1, N_RWKV_PAD),
        "w0": w0[li], "wup": wup.astype(bf16), "a0": a0[li], "aup": aup.astype(bf16),
        "gup": g_up[li].astype(bf16), "k_k": k_k[li].reshape(1, D), "k_a": k_a[li].reshape(1, D),
        "r_k": r_k[li].reshape(1, D), "ln_w": ln_w[li].reshape(1, D), "ln_b": ln_b[li].reshape(1, D),
        "ones_bd": _block_ones(RW_GW, HEAD),
    }


def _ssd_params(li, conv_w, conv_b, a_log, dt_bias, d_skip):
    a = -jnp.exp(a_log[li].astype(f32))
    hid = jnp.arange(D) // SSM_P
    sel = jnp.arange(128)[:, None]
    e = jnp.stack([(sel == hid[None, :] + d * SSM_HEADS) for d in range(2)]).astype(bf16)
    return {
        "cw": conv_w[li], "cb": conv_b[li].reshape(1, D_XBC),
        "dtb": jnp.pad(dt_bias[li].reshape(1, 2 * SSM_HEADS), ((0, 0), (0, 64))),
        "a": jnp.pad(a.reshape(1, 2 * SSM_HEADS), ((0, 0), (0, 64))),
        "dsk": jnp.stack([jnp.repeat(d_skip[li], SSM_P), jnp.zeros((D,), f32)]).reshape(2, 1, D),
        "e": e,
    }


def kernel(x, c, ctx, c_ctx, ada_w, ada_b, norm_mix, norm_ffn, norm_final, w_in, lru_conv_w, lru_conv_b, lru_gate_w, lru_gate_b, lru_lambda, rwkv_mu, rwkv_w0, rwkv_w_up, rwkv_a0, rwkv_a_up, rwkv_g_up, rwkv_k_k, rwkv_k_a, rwkv_r_k, rwkv_ln_w, rwkv_ln_b, ssm_conv_w, ssm_conv_b, ssm_a_log, ssm_dt_bias, ssm_d, ssm_norm_w, w_out_lru, w_out_rwkv, w_out_ssm, w_o, ffn_w1, ffn_w3, ffn_w2, moe_router, moe_w1, moe_w3, moe_w2):
    bsz, l, _ = x.shape
    depth = ada_w.shape[0]
    off_lru = 3 * D
    off_rwkv = off_lru + 2 * D
    off_ssm = off_rwkv + N_RWKV
    cvec = jnp.zeros((8, D), f32).at[:bsz].set(c).at[bsz].set(c_ctx)
    lat_row = lambda b: b
    ctx_row = lambda b: bsz

    h_lat, h_ctx = x, ctx
    for li in range(depth):
        last = li == depth - 1
        odd = li % 2 == 1
        mod4 = _ada(cvec, ada_w, ada_b, li).reshape(8, 6, 1, D)

        w_ss = _pad_cols(w_in[li, :, off_ssm:], N_SSM_PAD)[None]
        lru_gw = [jnp.concatenate([lru_gate_w[li, d, 0], lru_gate_w[li, d, 1]], axis=-1).astype(bf16)
                  for d in range(2)]
        rp = _rwkv_params(li, rwkv_mu, rwkv_w0, rwkv_w_up, rwkv_a0, rwkv_a_up, rwkv_g_up, rwkv_k_k,
                          rwkv_k_a, rwkv_r_k, rwkv_ln_w, rwkv_ln_b)
        sp = _ssd_params(li, ssm_conv_w, ssm_conv_b, ssm_a_log, ssm_dt_bias, ssm_d)
        w_outs = [w_out_lru[li].astype(bf16), w_out_rwkv[li].astype(bf16), w_out_ssm[li].astype(bf16)]
        w_o_b = w_o[li].astype(bf16)

        def token_mix(u, states, need_out):
            lx = u.shape[1]
            um = u.reshape(bsz * lx, D)
            p_lru = _proj(um, w_in, li, tn=512, col0=off_lru, n=2 * D).reshape(bsz, lx, 2 * D)
            p_rw = _proj(um, w_in, li, tn=512, col0=off_rwkv, n=N_RWKV_PAD).reshape(bsz, lx, N_RWKV_PAD)
            p_ss = _proj(um, w_ss, 0, tn=768, tm=1024).reshape(bsz, lx, N_SSM_PAD)
            lru_s, rw_s, ss_s = states
            cw, cb = lru_conv_w[li], lru_conv_b[li].reshape(1, D)
            hb, hl_b = _lru_pass(p_lru, cw, cb, lru_gw[1], lru_gate_b[li, 1], lru_lambda[li, 1].reshape(1, D),
                                 lru_s[1], None, reverse=True)
            ya, hl_f = _lru_pass(p_lru, cw, cb, lru_gw[0], lru_gate_b[li, 0], lru_lambda[li, 0].reshape(1, D),
                                 lru_s[0], hb, reverse=False)
            r, v, kk, g, bonus, lw, key, bvec = _rwkv_feat(p_rw, rp)
            y2, rw_fin = _rwkv_scan(r, v, kk, lw, key, bvec, rw_s)
            ys2, ss_fin = _ssd(p_ss, sp, ss_s)
            new_states = ((hl_f, hl_b), rw_fin, ss_fin)
            if not need_out:
                return None, new_states
            yb = _rwkv_out(y2, bonus, g, rp["ln_w"], rp["ln_b"], rp["ones_bd"])
            yc = _ssd_out(ys2, p_ss, ssm_norm_w[li].reshape(1, D))
            sig = _proj(um, w_in, li, tn=512, col0=0, n=off_lru, act="sigmoid", out_dtype=bf16)
            m = _merge([ya.reshape(-1, D), yb.reshape(-1, D), yc.reshape(-1, D)], sig, w_outs)
            return m.reshape(bsz, lx, D), new_states

        zero_states = ((jnp.zeros((bsz, 1, D), f32), jnp.zeros((bsz, 1, D), f32)),
                       jnp.zeros((2, bsz, RW_NG, RW_GW, RW_GW), f32),
                       jnp.zeros((2, bsz, SSM_G, SSM_N, 8 * SSM_P), f32))

        u_ctx = _norm(h_ctx, norm_mix[li], mod4, ctx_row, 1, 0)
        u_lat = _norm(h_lat, norm_mix[li], mod4, lat_row, 1, 0, transposed=odd)
        m_ctx, ctx_states = token_mix(u_ctx, zero_states, not last)
        m_lat, _ = token_mix(u_lat, ctx_states, True)
        h_lat = _wo_residual(m_lat, w_o_b, h_lat, mod4, lat_row, 2, transposed=odd)
        if not last:
            h_ctx = _wo_residual(m_ctx, w_o_b, h_ctx, mod4, ctx_row, 2, transposed=False)

        j = li // 2
        streams = [(h_lat, lat_row)] + ([] if last else [(h_ctx, ctx_row)])
        outs = []
        for h, row_fn in streams:
            lx = h.shape[1]
            if not odd:
                v = _norm(h, norm_ffn[li], mod4, row_fn, 4, 3)
                act = _swiglu_up(v.reshape(bsz * lx, D), ffn_w1[:, None], ffn_w3[:, None], j)
                w2 = ffn_w2[j].astype(bf16)
            elif row_fn is lat_row:
                router_pad = _pad_cols(moe_router[j], 128)
                v32, gates = _norm_router(h, norm_ffn[li], mod4, row_fn, 4, 3, router_pad, out_dtype=f32)
                outs.append(_moe_sparse(v32, gates, h, moe_w1, moe_w3, moe_w2, j, mod4, row_fn, 5))
                continue
            else:
                router_pad = _pad_cols(moe_router[j], 128)
                v, gates = _norm_router(h, norm_ffn[li], mod4, row_fn, 4, 3, router_pad)
                act = _swiglu_up(v.reshape(bsz * lx, D), moe_w1, moe_w3, j, gates.reshape(bsz * lx, 128))
                w2 = moe_w2[j].astype(bf16).reshape(N_EXPERTS * D_FF_EXPERT, D)
            outs.append(_down_residual(act.reshape(bsz, lx, -1), w2, h, mod4, row_fn, 5))
        h_lat = outs[0]
        if not last:
            h_ctx = outs[1]
    return _final_norm(h_lat, norm_final)
```

```python
import functools
import math

import jax
import jax.numpy as jnp
from jax import lax
from jax.experimental import pallas as pl
from jax.experimental.pallas import tpu as pltpu

f32 = jnp.float32
bf16 = jnp.bfloat16

D = 2048
GRID_W = 64
EPS = 1e-6
CONV_W = 4
CONV_LEFT = 2
HALO = 8

LRU_BLOCKS = 16
LRU_BS = D // LRU_BLOCKS
LRU_C = 8.0

HEADS = 32
HEAD = 64
W_LORA = 96
A_LORA = 96
G_LORA = 256
GN_EPS = 64e-5
N_RWKV = 3 * D + 2 * W_LORA + 2 * A_LORA + G_LORA
N_RWKV_PAD = 7168
RW_OFF_W = 3 * D
RW_OFF_A = 3 * D + 128
RW_OFF_G = 3 * D + 2 * W_LORA + 2 * A_LORA
RW_CHUNK = 64
RW_GW = 256
RW_NG = D // RW_GW

SSM_HEADS = 32
SSM_P = 64
SSM_N = 128
SSM_G = 4
SSM_Q = 128
D_XBC = D + 2 * SSM_G * SSM_N
N_SSM = D + D_XBC + 2 * SSM_HEADS
N_SSM_PAD = 5376
SS_OFF_DT = D + D_XBC

D_FF = 3 * D
N_EXPERTS = 8
D_FF_EXPERT = D_FF // 2

VMEM_LIMIT = 56 * 1024 * 1024


def _cparams(n_axes, vmem=VMEM_LIMIT):
    return pltpu.CompilerParams(dimension_semantics=("arbitrary",) * n_axes, vmem_limit_bytes=vmem)


def _mm(a, b):
    return jnp.dot(a.astype(bf16), b.astype(bf16), preferred_element_type=f32)


def _mm_nt(a, b):
    return lax.dot_general(a.astype(bf16), b.astype(bf16), (((1,), (1,)), ((), ())),
                           preferred_element_type=f32)


def _split3(x):
    x0 = x.astype(bf16)
    r = x - x0.astype(f32)
    x1 = r.astype(bf16)
    r = r - x1.astype(f32)
    return x0, x1, r.astype(bf16)


def _mm_x3(x, e):
    x0, x1, x2 = _split3(x)
    return (jnp.dot(x0, e, preferred_element_type=f32) + jnp.dot(x1, e, preferred_element_type=f32)
            + jnp.dot(x2, e, preferred_element_type=f32))


def _mm_x2(x, e):
    x0 = x.astype(bf16)
    x1 = (x - x0.astype(f32)).astype(bf16)
    return jnp.dot(x0, e, preferred_element_type=f32) + jnp.dot(x1, e, preferred_element_type=f32)


def _mm_e3(e, x):
    x0, x1, x2 = _split3(x)
    return (jnp.dot(e, x0, preferred_element_type=f32) + jnp.dot(e, x1, preferred_element_type=f32)
            + jnp.dot(e, x2, preferred_element_type=f32))


def _softplus(x):
    return jnp.maximum(x, 0.0) + jnp.log1p(jnp.exp(-jnp.abs(x)))


def _sigmoid(x):
    return 0.5 * jnp.tanh(0.5 * x) + 0.5


def _silu(x):
    return x * _sigmoid(x)


def _iota(shape, dim):
    return lax.broadcasted_iota(jnp.int32, shape, dim)


def _ada_kernel(c_ref, w_ref, b_ref, o_ref):
    cv = c_ref[...]
    o_ref[...] = _mm(_silu(cv), w_ref[...]) + b_ref[...]


def _ada(cvec, w, b, li):
    depth, _, n = w.shape
    tn = 1536
    return pl.pallas_call(
        _ada_kernel,
        grid=(n // tn,),
        in_specs=[pl.BlockSpec((8, D), lambda j: (0, 0)),
                  pl.BlockSpec((None, D, tn), lambda j: (li, 0, j)),
                  pl.BlockSpec((None, 1, tn), lambda j: (li, 0, j))],
        out_specs=pl.BlockSpec((8, tn), lambda j: (0, j)),
        out_shape=jax.ShapeDtypeStruct((8, n), f32),
        compiler_params=_cparams(1),
        name="ada",
    )(cvec, w, b.reshape(depth, 1, n))


def _norm_kernel(x_ref, g_ref, sc_ref, sh_ref, o_ref):
    x = x_ref[...]
    ms = jnp.mean(x * x, axis=-1, keepdims=True)
    xn = x * lax.rsqrt(ms + EPS) * g_ref[...]
    o_ref[...] = (xn * (1.0 + sc_ref[...]) + sh_ref[...]).astype(o_ref.dtype)


def _norm(h, gain, mod4, row_of_batch, k_sc, k_sh, *, transposed=False, out_dtype=bf16):
    bsz, l, _ = h.shape
    if transposed:
        rows = l // GRID_W
        hin = h.reshape(bsz, rows, GRID_W * D)
        tl = rows
        nt = GRID_W
        in_spec = pl.BlockSpec((None, tl, D), lambda b, i: (b, 0, i))
    else:
        tl = min(512, l)
        nt = l // tl
        hin = h
        in_spec = pl.BlockSpec((None, tl, D), lambda b, i: (b, i, 0))
    return pl.pallas_call(
        _norm_kernel,
        grid=(bsz, nt),
        in_specs=[in_spec,
                  pl.BlockSpec((1, D), lambda b, i: (0, 0)),
                  pl.BlockSpec((None, None, 1, D), lambda b, i: (row_of_batch(b), k_sc, 0, 0)),
                  pl.BlockSpec((None, None, 1, D), lambda b, i: (row_of_batch(b), k_sh, 0, 0))],
        out_specs=pl.BlockSpec((None, tl, D), lambda b, i: (b, i, 0)),
        out_shape=jax.ShapeDtypeStruct((bsz, l, D), out_dtype),
        compiler_params=_cparams(2),
        name="norm",
    )(hin, gain.reshape(1, D), mod4, mod4)


def _plain_norm_kernel(x_ref, g_ref, o_ref):
    x = x_ref[...]
    ms = jnp.mean(x * x, axis=-1, keepdims=True)
    o_ref[...] = (x * lax.rsqrt(ms + EPS) * g_ref[...]).astype(o_ref.dtype)


def _final_norm(h, gain):
    bsz, l, _ = h.shape
    tl = min(512, l)
    return pl.pallas_call(
        _plain_norm_kernel,
        grid=(bsz, l // tl),
        in_specs=[pl.BlockSpec((None, tl, D), lambda b, i: (b, i, 0)),
                  pl.BlockSpec((1, D), lambda b, i: (0, 0))],
        out_specs=pl.BlockSpec((None, tl, D), lambda b, i: (b, i, 0)),
        out_shape=jax.ShapeDtypeStruct((bsz, l, D), f32),
        compiler_params=_cparams(2),
        name="final_norm",
    )(h, gain.reshape(1, D))


def _proj_kernel(u_ref, w_ref, o_ref, wb_ref, *, act):
    @pl.when(pl.program_id(1) == 0)
    def _():
        wb_ref[...] = w_ref[...].astype(bf16)

    acc = jnp.dot(u_ref[...], wb_ref[...], preferred_element_type=f32)
    if act == "sigmoid":
        acc = _sigmoid(acc)
    o_ref[...] = acc.astype(o_ref.dtype)


def _proj(u, w, li, *, tn, col0=0, n=None, act=None, out_dtype=f32):
    m, k = u.shape
    n = w.shape[2] if n is None else n
    assert col0 % tn == 0 and n % tn == 0 and col0 + n <= w.shape[2]
    off = col0 // tn
    tm = min(1024, m)
    return pl.pallas_call(
        functools.partial(_proj_kernel, act=act),
        grid=(n // tn, m // tm),
        in_specs=[pl.BlockSpec((tm, k), lambda j, i: (i, 0)),
                  pl.BlockSpec((None, k, tn), lambda j, i: (li, 0, j + off))],
        out_specs=pl.BlockSpec((tm, tn), lambda j, i: (i, j)),
        out_shape=jax.ShapeDtypeStruct((m, n), out_dtype),
        scratch_shapes=[pltpu.VMEM((k, tn), bf16)],
        compiler_params=_cparams(2),
        name="proj",
    )(u, w)


def _lru_kernel(*refs, reverse, final, tl, nt):
    if final:
        (x_ref, prev_ref, next_ref, gate_ref, hb_ref, cw_ref, cb_ref, gw_ref, gb_ref, lam_ref, h0_ref,
         out_ref, hlast_ref, xe_ref, a_ref, bx_ref, hs_ref, h_ref) = refs
    else:
        (x_ref, prev_ref, next_ref, cw_ref, cb_ref, gw_ref, gb_ref, lam_ref, h0_ref,
         out_ref, hlast_ref, xe_ref, a_ref, bx_ref, hs_ref, h_ref) = refs
    i = pl.program_id(1)
    t = (nt - 1 - i) if reverse else i

    @pl.when(i == 0)
    def _():
        h_ref[...] = h0_ref[...]

    zero = jnp.zeros((HALO, D), f32)
    xe_ref[0:HALO, :] = jnp.where(t > 0, prev_ref[...], zero)
    xe_ref[HALO:HALO + tl, :] = x_ref[...]
    xe_ref[HALO + tl:2 * HALO + tl, :] = jnp.where(t < nt - 1, next_ref[...], zero)

    for n in range(LRU_BLOCKS):
        cs = slice(n * LRU_BS, (n + 1) * LRU_BS)
        xc = cb_ref[:, cs]
        for tap in range(CONV_W):
            r0 = HALO - CONV_LEFT + tap
            xc = xc + xe_ref[r0:r0 + tl, cs] * cw_ref[tap:tap + 1, cs]
        g = _mm(xc, gw_ref[n])
        rec = _sigmoid(g[:, :LRU_BS] + gb_ref[0:1, cs])
        inp = _sigmoid(g[:, LRU_BS:] + gb_ref[1:2, cs])
        log_a = -LRU_C * rec * _softplus(-lam_ref[:, cs])
        a_ref[:, cs] = jnp.exp(log_a)
        th = jnp.tanh(log_a)
        bx_ref[:, cs] = jnp.sqrt(-2.0 * th / (1.0 - th)) * inp * xc

    def body(s, h):
        tt = (tl - 1 - s) if reverse else s
        h = a_ref[pl.ds(tt, 1), :] * h + bx_ref[pl.ds(tt, 1), :]
        hs_ref[pl.ds(tt, 1), :] = h
        return h

    h = lax.fori_loop(0, tl, body, h_ref[...], unroll=8)
    h_ref[...] = h

    @pl.when(i == nt - 1)
    def _():
        hlast_ref[...] = h

    if final:
        for n in range(LRU_BLOCKS):
            cs = slice(n * LRU_BS, (n + 1) * LRU_BS)
            y = (hs_ref[:, cs] + hb_ref[:, cs]) * jax.nn.gelu(gate_ref[:, cs])
            out_ref[:, cs] = y.astype(out_ref.dtype)
    else:
        out_ref[...] = hs_ref[...]


def _lru_pass(p, cw, cb, gw, gb, lam, h0, hb, *, reverse):
    bsz, l, _ = p.shape
    tl = min(256, l)
    nt = l // tl
    final = hb is not None
    nh = l // HALO
    tpb = tl // HALO

    def tmap(i):
        return (nt - 1 - i) if reverse else i

    main = pl.BlockSpec((None, tl, D), lambda b, i: (b, tmap(i), 0))
    in_specs = [main,
                pl.BlockSpec((None, HALO, D), lambda b, i: (b, jnp.maximum(tmap(i) * tpb - 1, 0), 0)),
                pl.BlockSpec((None, HALO, D), lambda b, i: (b, jnp.minimum((tmap(i) + 1) * tpb, nh - 1), 0))]
    args = [p, p, p]
    if final:
        in_specs += [pl.BlockSpec((None, tl, D), lambda b, i: (b, tmap(i), 1)), main]
        args += [p, hb]
    in_specs += [pl.BlockSpec((CONV_W, D), lambda b, i: (0, 0)),
                 pl.BlockSpec((1, D), lambda b, i: (0, 0)),
                 pl.BlockSpec((LRU_BLOCKS, LRU_BS, 2 * LRU_BS), lambda b, i: (0, 0, 0)),
                 pl.BlockSpec((2, D), lambda b, i: (0, 0)),
                 pl.BlockSpec((1, D), lambda b, i: (0, 0)),
                 pl.BlockSpec((None, 1, D), lambda b, i: (b, 0, 0))]
    args += [cw, cb, gw, gb, lam, h0]
    out, hlast = pl.pallas_call(
        functools.partial(_lru_kernel, reverse=reverse, final=final, tl=tl, nt=nt),
        grid=(bsz, nt),
        in_specs=in_specs,
        out_specs=[main, pl.BlockSpec((None, 1, D), lambda b, i: (b, 0, 0))],
        out_shape=[jax.ShapeDtypeStruct((bsz, l, D), bf16 if final else f32),
                   jax.ShapeDtypeStruct((bsz, 1, D), f32)],
        scratch_shapes=[pltpu.VMEM((tl + 2 * HALO, D), f32), pltpu.VMEM((tl, D), f32),
                        pltpu.VMEM((tl, D), f32), pltpu.VMEM((tl, D), f32), pltpu.VMEM((1, D), f32)],
        compiler_params=_cparams(2),
        name="lru_fwd" if final else "lru_bwd",
    )(*args)
    return out, hlast


def _rwkv_feat_kernel(p_ref, prev_ref, next_ref, mu_ref, w0_ref, wup_ref, a0_ref, aup_ref, gup_ref,
                      kk_ref, ka_ref, rk_ref, ones_ref,
                      r_out, v_out, kkv_out, g_out, bonus_out, lw_out, key_out, b_out,
                      pe_ref, *, tl, nt):
    i = pl.program_id(1)
    zero = jnp.zeros((HALO, N_RWKV_PAD), f32)
    pe_ref[0:HALO, :] = jnp.where(i > 0, prev_ref[...], zero)
    pe_ref[HALO:HALO + tl, :] = p_ref[...]
    pe_ref[HALO + tl:2 * HALO + tl, :] = jnp.where(i < nt - 1, next_ref[...], zero)

    def shifted(c0, width):
        cs = slice(c0, c0 + width)
        p = pe_ref[HALO:HALO + tl, cs]
        nb = 0.5 * (pe_ref[HALO - 1:HALO - 1 + tl, cs] + pe_ref[HALO + 1:HALO + 1 + tl, cs])
        return p + mu_ref[:, cs] * (nb - p)

    win_w = jnp.tanh(shifted(RW_OFF_W, 256)).astype(bf16)
    win_a = shifted(RW_OFF_A, 256).astype(bf16)
    win_g = _sigmoid(shifted(RW_OFF_G, 256)).astype(bf16)
    ones_bd = ones_ref[...]

    for g in range(RW_NG):
        cs = slice(g * RW_GW, (g + 1) * RW_GW)
        r = shifted(g * RW_GW, RW_GW)
        k = shifted(D + g * RW_GW, RW_GW)
        v = shifted(2 * D + g * RW_GW, RW_GW)
        kf = k * kk_ref[:, cs]
        kk = kf * lax.rsqrt(_mm_x2(kf * kf, ones_bd) + 1e-12)
        ksum = None
        for d in range(2):
            wv = -_softplus(-(w0_ref[d:d + 1, cs] + jnp.dot(win_w, wup_ref[d, :, cs],
                                                              preferred_element_type=f32))) - 0.5
            lw_out[d, :, cs] = -jnp.exp(wv)
            a = _sigmoid(a0_ref[d:d + 1, cs] + jnp.dot(win_a, aup_ref[d, :, cs],
                                                       preferred_element_type=f32))
            key = k * (1.0 + (a - 1.0) * ka_ref[:, cs])
            key_out[d, :, cs] = key
            b_out[d, :, cs] = kk * a
            ksum = key if ksum is None else ksum + key
        r_out[:, cs] = r
        v_out[:, cs] = v
        kkv_out[:, cs] = kk
        bonus_out[:, cs] = _mm_x2(r * ksum * rk_ref[:, cs], ones_bd) * v
        g_out[:, cs] = jnp.dot(win_g, gup_ref[:, cs], preferred_element_type=f32)


def _rwkv_feat(p, prm):
    bsz, l, _ = p.shape
    tl = min(128, l)
    nt = l // tl
    nh = l // HALO
    tpb = tl // HALO
    full2 = lambda shape: pl.BlockSpec(shape, lambda b, i: (0,) * len(shape))
    tok = pl.BlockSpec((None, tl, D), lambda b, i: (b, i, 0))
    tok2 = pl.BlockSpec((2, None, tl, D), lambda b, i: (0, b, i, 0))
    sd = jax.ShapeDtypeStruct((bsz, l, D), f32)
    sd2 = jax.ShapeDtypeStruct((2, bsz, l, D), f32)
    return pl.pallas_call(
        functools.partial(_rwkv_feat_kernel, tl=tl, nt=nt),
        grid=(bsz, nt),
        in_specs=[pl.BlockSpec((None, tl, N_RWKV_PAD), lambda b, i: (b, i, 0)),
                  pl.BlockSpec((None, HALO, N_RWKV_PAD), lambda b, i: (b, jnp.maximum(i * tpb - 1, 0), 0)),
                  pl.BlockSpec((None, HALO, N_RWKV_PAD),
                               lambda b, i: (b, jnp.minimum((i + 1) * tpb, nh - 1), 0)),
                  full2((1, N_RWKV_PAD)), full2((2, D)), full2((2, 256, D)), full2((2, D)),
                  full2((2, 256, D)), full2((256, D)), full2((1, D)), full2((1, D)), full2((1, D)),
                  full2((RW_GW, RW_GW))],
        out_specs=[tok, tok, tok, tok, tok, tok2, tok2, tok2],
        out_shape=[sd, sd, sd, sd, sd, sd2, sd2, sd2],
        scratch_shapes=[pltpu.VMEM((tl + 2 * HALO, N_RWKV_PAD), f32)],
        compiler_params=_cparams(2),
        name="rwkv_feat",
    )(p, p, p, prm["mu"], prm["w0"], prm["wup"], prm["a0"], prm["aup"], prm["gup"],
      prm["k_k"], prm["k_a"], prm["r_k"], prm["ones_bd"])


def _bd_expand(y, lane_head):
    yb = y.astype(f32)
    return jnp.concatenate([jnp.where(lane_head == h, yb, 0.0).astype(bf16) for h in range(4)], axis=0)


def _rwkv_scan_kernel(r_ref, v_ref, kk_ref, lw_ref, key_ref, b_ref, s0_ref, y_ref, sfin_ref,
                      s_ref, *, nc):
    c = RW_CHUNK
    d = pl.program_id(0)
    i = pl.program_id(2)
    fwd = d == 0

    @pl.when(i == 0)
    def _():
        s_ref[...] = s0_ref[...]

    row = _iota((c, c), 0)
    col = _iota((c, c), 1)
    sgn = jnp.where(fwd, 1, -1)
    tri = jnp.where((col - row) * sgn <= 0, 1.0, 0.0).astype(bf16)
    t4 = _iota((c, 4 * c), 0)
    j4 = _iota((c, 4 * c), 1) % c
    mask_s = (j4 - t4) * sgn < 0
    mask_i = (j4 - t4) * sgn <= 0
    eye4 = jnp.where(j4 == t4, 1.0, 0.0)
    lane_head = _iota((c, RW_GW), 1) // HEAD
    bd_mask = (_iota((RW_GW, RW_GW), 0) // HEAD) == (_iota((RW_GW, RW_GW), 1) // HEAD)

    def mmbd(x, y):
        return jnp.dot(x.astype(bf16), _bd_expand(y, lane_head), preferred_element_type=f32)

    groups = range(RW_NG)
    sl = [slice(g * RW_GW, (g + 1) * RW_GW) for g in groups]

    def state_free(rows):
        lw = [lw_ref[rows, sl[g]] for g in groups]
        cl = [_mm_e3(tri, lw[g]) for g in groups]
        tot = [jnp.sum(lw[g], axis=0, keepdims=True) for g in groups]
        v = [v_ref[rows, sl[g]] for g in groups]
        ar, a_b, a_k, bk = [], [], [], []
        for g in groups:
            g_inv = jnp.exp(-cl[g])
            g_end = jnp.exp(tot[g] - cl[g])
            bv = b_ref[rows, sl[g]]
            key = key_ref[rows, sl[g]]
            at = -kk_ref[rows, sl[g]] * jnp.exp(cl[g] - lw[g])
            rt = r_ref[rows, sl[g]] * jnp.exp(cl[g])
            ar.append(jnp.concatenate([at, rt], axis=0).astype(bf16))
            a_b.append(lax.dot_general(ar[g], _bd_expand(bv * g_inv, lane_head), (((1,), (1,)), ((), ())),
                                       preferred_element_type=f32))
            a_k.append(lax.dot_general(ar[g], _bd_expand(key * g_inv, lane_head), (((1,), (1,)), ((), ())),
                                       preferred_element_type=f32))
            bk.append(jnp.concatenate([bv * g_end, key * g_end], axis=0).astype(bf16))
        n_ab = [jnp.where(mask_s, a_b[g][:c], 0.0) for g in groups]
        a_rb = [jnp.where(mask_i, a_b[g][c:], 0.0) for g in groups]
        a_kk = [jnp.concatenate([jnp.where(mask_s, a_k[g][:c], 0.0), jnp.where(mask_i, a_k[g][c:], 0.0)],
                                axis=0) for g in groups]
        x = [eye4 + n_ab[g] for g in groups]
        m = [mmbd(n_ab[g], n_ab[g]) for g in groups]
        lvl = 2
        while lvl < c:
            if lvl * 2 < c:
                xm = [mmbd(jnp.concatenate([x[g], m[g]], axis=0), m[g]) for g in groups]
                x = [x[g] + xm[g][:c] for g in groups]
                m = [xm[g][c:] for g in groups]
            else:
                x = [x[g] + mmbd(x[g], m[g]) for g in groups]
            lvl *= 2
        av = [mmbd(a_kk[g], v[g]) for g in groups]
        return dict(ar=ar, x=x, av=av, a_rb=a_rb, v=v, bk=bk, tot=tot)

    def state_step(rows, p, s):
        sa = [_mm_nt(p["ar"][g], s[g]) for g in groups]
        u = [mmbd(p["x"][g], sa[g][:c] + p["av"][g][:c]) for g in groups]
        y = [sa[g][c:] + p["av"][g][c:] + mmbd(p["a_rb"][g], u[g]) for g in groups]
        upd = [_mm(jnp.concatenate([u[g], p["v"][g]], axis=0).T, p["bk"][g]) for g in groups]
        for g in groups:
            y_ref[rows, sl[g]] = y[g]
        return [s[g] * jnp.exp(p["tot"][g]) + jnp.where(bd_mask, upd[g], 0.0) for g in groups]

    first = pl.multiple_of(jnp.where(fwd, 0, c), c)
    chunk_rows = [pl.ds(first, c), pl.ds(pl.multiple_of(c - first, c), c)]
    parts = [state_free(rows) for rows in chunk_rows]
    s = [s_ref[g] for g in groups]
    for rows, p in zip(chunk_rows, parts):
        s = state_step(rows, p, s)
    for g in groups:
        s_ref[g] = s[g]

    @pl.when(i == nc - 1)
    def _():
        sfin_ref[...] = s_ref[...]


def _rwkv_scan(r, v, kk, lw, key, bvec, s0):
    bsz, l, _ = r.shape
    c = 2 * RW_CHUNK
    nc = l // c

    def cidx(d, i):
        return jnp.where(d == 0, i, nc - 1 - i)

    tok = pl.BlockSpec((None, c, D), lambda d, b, i: (b, cidx(d, i), 0))
    tok2 = pl.BlockSpec((None, None, c, D), lambda d, b, i: (d, b, cidx(d, i), 0))
    st = pl.BlockSpec((None, None, RW_NG, RW_GW, RW_GW), lambda d, b, i: (d, b, 0, 0, 0))
    return pl.pallas_call(
        functools.partial(_rwkv_scan_kernel, nc=nc),
        grid=(2, bsz, nc),
        in_specs=[tok, tok, tok, tok2, tok2, tok2, st],
        out_specs=[tok2, st],
        out_shape=[jax.ShapeDtypeStruct((2, bsz, l, D), f32),
                   jax.ShapeDtypeStruct((2, bsz, RW_NG, RW_GW, RW_GW), f32)],
        scratch_shapes=[pltpu.VMEM((RW_NG, RW_GW, RW_GW), f32)],
        compiler_params=_cparams(3),
        name="rwkv_scan",
    )(r, v, kk, lw, key, bvec, s0)


def _rwkv_out_kernel(y_ref, bonus_ref, g_ref, lnw_ref, lnb_ref, ones_ref, o_ref):
    ones_bd = ones_ref[...]
    for g in range(RW_NG):
        cs = slice(g * RW_GW, (g + 1) * RW_GW)
        y = y_ref[0, :, cs] + y_ref[1, :, cs]
        mean = _mm_x2(y, ones_bd) * (1.0 / HEAD)
        yc = y - mean
        var = _mm_x2(yc * yc, ones_bd) * (1.0 / HEAD)
        yn = yc * lax.rsqrt(var + GN_EPS) * lnw_ref[:, cs] + lnb_ref[:, cs]
        o_ref[:, cs] = ((yn + bonus_ref[:, cs]) * g_ref[:, cs]).astype(o_ref.dtype)


def _rwkv_out(y2, bonus, g, ln_w, ln_b, ones_bd):
    _, bsz, l, _ = y2.shape
    tl = min(256, l)
    tok = pl.BlockSpec((None, tl, D), lambda b, i: (b, i, 0))
    vec = pl.BlockSpec((1, D), lambda b, i: (0, 0))
    return pl.pallas_call(
        _rwkv_out_kernel,
        grid=(bsz, l // tl),
        in_specs=[pl.BlockSpec((2, None, tl, D), lambda b, i: (0, b, i, 0)), tok, tok, vec, vec,
                  pl.BlockSpec((RW_GW, RW_GW), lambda b, i: (0, 0))],
        out_specs=tok,
        out_shape=jax.ShapeDtypeStruct((bsz, l, D), bf16),
        compiler_params=_cparams(2),
        name="rwkv_out",
    )(y2, bonus, g, ln_w, ln_b, ones_bd)


def _ssd_prep_kernel(p_ref, prev_ref, next_ref, cw_ref, cb_ref, o_ref, pe_ref, *, tl, nt):
    i = pl.program_id(1)
    w = p_ref.shape[1]
    zero = jnp.zeros((HALO, w), f32)
    pe_ref[0:HALO, :] = jnp.where(i > 0, prev_ref[...], zero)
    pe_ref[HALO:HALO + tl, :] = p_ref[...]
    pe_ref[HALO + tl:2 * HALO + tl, :] = jnp.where(i < nt - 1, next_ref[...], zero)
    for n in range(w // 128):
        cs = slice(n * 128, (n + 1) * 128)
        xc = cb_ref[:, cs]
        for tap in range(CONV_W):
            r0 = HALO - CONV_LEFT + tap
            xc = xc + pe_ref[r0:r0 + tl, cs] * cw_ref[tap:tap + 1, cs]
        o_ref[:, cs] = _silu(xc)


def _ssd_prep(p, cw, cb):
    bsz, l, _ = p.shape
    tl = min(256, l)
    nt = l // tl
    nh = l // HALO
    tpb = tl // HALO
    wc = 1024
    c0 = D // wc
    return pl.pallas_call(
        functools.partial(_ssd_prep_kernel, tl=tl, nt=nt),
        grid=(bsz, nt, D_XBC // wc),
        in_specs=[pl.BlockSpec((None, tl, wc), lambda b, i, c: (b, i, c + c0)),
                  pl.BlockSpec((None, HALO, wc), lambda b, i, c: (b, jnp.maximum(i * tpb - 1, 0), c + c0)),
                  pl.BlockSpec((None, HALO, wc),
                               lambda b, i, c: (b, jnp.minimum((i + 1) * tpb, nh - 1), c + c0)),
                  pl.BlockSpec((CONV_W, wc), lambda b, i, c: (0, c)),
                  pl.BlockSpec((1, wc), lambda b, i, c: (0, c))],
        out_specs=pl.BlockSpec((None, tl, wc), lambda b, i, c: (b, i, c)),
        out_shape=jax.ShapeDtypeStruct((bsz, l, D_XBC), f32),
        scratch_shapes=[pltpu.VMEM((tl + 2 * HALO, wc), f32)],
        compiler_params=_cparams(3),
        name="ssd_prep",
    )(p, p, p, cw, cb)


def _ssd_kernel(xbc_ref, dtw_ref, dtb_ref, a_ref, dsk_ref, e_ref,
                h0_ref, y_ref, hfin_ref, cumt_ref, dtt_ref, h_ref, *, nc):
    q = SSM_Q
    d = pl.program_id(0)
    i = pl.program_id(2)
    fwd = d == 0

    @pl.when(i == 0)
    def _():
        h_ref[...] = h0_ref[...]

    row = _iota((q, q), 0)
    col = _iota((q, q), 1)
    sgn = jnp.where(fwd, 1, -1)
    low = (col - row) * sgn <= 0
    tri = jnp.where(low, 1.0, 0.0).astype(bf16)
    tri_t = jnp.where((row - col) * sgn <= 0, 1.0, 0.0).astype(bf16)

    dt = _softplus(dtw_ref[...] + dtb_ref[...])
    dta = dt * a_ref[...]
    cum = _mm_e3(tri, dta)
    tot = jnp.sum(dta, axis=0, keepdims=True)
    cumt_ref[...] = _mm_x3(dta.T, tri_t)
    dtt_ref[...] = dt.T
    e_d = e_ref[...]
    dec_e = _mm_x2(jnp.exp(cum), e_d)
    toend_e = _mm_x2(jnp.exp(tot - cum) * dt, e_d)
    tot_e = _mm_x2(jnp.broadcast_to(jnp.exp(tot), (8, 128)), e_d)[0:1]
    cum_d = pltpu.roll(cum, jnp.where(fwd, 0, 128 - SSM_HEADS), 1)
    lane = _iota((q, 128), 1)

    for g in range(SSM_G):
        bg = xbc_ref[:, D + g * SSM_N:D + (g + 1) * SSM_N]
        cg = xbc_ref[:, D + SSM_G * SSM_N + g * SSM_N:D + SSM_G * SSM_N + (g + 1) * SSM_N]
        cb = _mm_nt(cg, bg)
        gs = slice(g * 512, (g + 1) * 512)
        hg = h_ref[g]
        y_off = _mm(cg, hg) * dec_e[:, gs]
        for pr in range(4):
            ls = []
            for hh in range(2):
                h = g * 8 + pr * 2 + hh
                rowv = cumt_ref[pl.ds(d * 32 + h, 1), :]
                dtr = dtt_ref[pl.ds(d * 32 + h, 1), :]
                seg = cum_d[:, h:h + 1] - rowv
                ls.append(jnp.where(low, jnp.exp(jnp.where(low, seg, 0.0)), 0.0) * cb * dtr)
            lp = jnp.concatenate(ls, axis=1)
            ps = slice(g * 512 + pr * 128, g * 512 + (pr + 1) * 128)
            xp = xbc_ref[:, ps]
            bd2 = jnp.concatenate([jnp.where(lane < SSM_P, xp, 0.0), jnp.where(lane >= SSM_P, xp, 0.0)],
                                  axis=0)
            yd = _mm(lp, bd2)
            y_ref[:, ps] = yd + y_off[:, pr * 128:(pr + 1) * 128] + dsk_ref[:, ps] * xp
        xs = xbc_ref[:, gs] * toend_e[:, gs]
        states = _mm(bg.T, xs)
        h_ref[g] = hg * tot_e[:, gs] + states

    @pl.when(i == nc - 1)
    def _():
        hfin_ref[...] = h_ref[...]


def _ssd(p, prm, h0):
    bsz, l, _ = p.shape
    q = SSM_Q
    nc = l // q
    xbc = _ssd_prep(p, prm["cw"], prm["cb"])

    def cidx(d, i):
        return jnp.where(d == 0, i, nc - 1 - i)

    full = lambda shape: pl.BlockSpec(shape, lambda d, b, i: (0,) * len(shape))
    st = pl.BlockSpec((None, None, SSM_G, SSM_N, 8 * SSM_P), lambda d, b, i: (d, b, 0, 0, 0))
    return pl.pallas_call(
        functools.partial(_ssd_kernel, nc=nc),
        grid=(2, bsz, nc),
        in_specs=[pl.BlockSpec((None, q, D_XBC), lambda d, b, i: (b, cidx(d, i), 0)),
                  pl.BlockSpec((None, q, 128), lambda d, b, i: (b, cidx(d, i), SS_OFF_DT // 128)),
                  full((1, 128)), full((1, 128)),
                  pl.BlockSpec((None, 1, D), lambda d, b, i: (d, 0, 0)),
                  pl.BlockSpec((None, 128, D), lambda d, b, i: (d, 0, 0)),
                  st],
        out_specs=[pl.BlockSpec((None, None, q, D), lambda d, b, i: (d, b, cidx(d, i), 0)), st],
        out_shape=[jax.ShapeDtypeStruct((2, bsz, l, D), f32),
                   jax.ShapeDtypeStruct((2, bsz, SSM_G, SSM_N, 8 * SSM_P), f32)],
        scratch_shapes=[pltpu.VMEM((128, q), f32), pltpu.VMEM((128, q), f32),
                        pltpu.VMEM((SSM_G, SSM_N, 8 * SSM_P), f32)],
        compiler_params=_cparams(3),
        name="ssd",
    )(xbc, p, prm["dtb"], prm["a"], prm["dsk"], prm["e"], h0)


def _ssd_out_kernel(y_ref, z_ref, nw_ref, o_ref):
    gw = D // SSM_G
    for g in range(SSM_G):
        cs = slice(g * gw, (g + 1) * gw)
        yg = (y_ref[0, :, cs] + y_ref[1, :, cs]) * _silu(z_ref[:, cs])
        ms = jnp.mean(yg * yg, axis=-1, keepdims=True)
        o_ref[:, cs] = (yg * lax.rsqrt(ms + EPS) * nw_ref[:, cs]).astype(o_ref.dtype)


def _ssd_out(y2, p, norm_w):
    _, bsz, l, _ = y2.shape
    tl = min(256, l)
    tok = pl.BlockSpec((None, tl, D), lambda b, i: (b, i, 0))
    return pl.pallas_call(
        _ssd_out_kernel,
        grid=(bsz, l // tl),
        in_specs=[pl.BlockSpec((2, None, tl, D), lambda b, i: (0, b, i, 0)), tok,
                  pl.BlockSpec((1, D), lambda b, i: (0, 0))],
        out_specs=tok,
        out_shape=jax.ShapeDtypeStruct((bsz, l, D), bf16),
        compiler_params=_cparams(2),
        name="ssd_out",
    )(y2, p, norm_w)


def _merge_kernel(ya_ref, yb_ref, yc_ref, sa_ref, sb_ref, sc_ref, wa_ref, wb_ref, wc_ref, o_ref):
    acc = None
    for y_ref, s_ref, w_ref in ((ya_ref, sa_ref, wa_ref), (yb_ref, sb_ref, wb_ref), (yc_ref, sc_ref, wc_ref)):
        t = s_ref[...].astype(f32) * jnp.dot(y_ref[...], w_ref[...], preferred_element_type=f32)
        acc = t if acc is None else acc + t
    o_ref[...] = acc.astype(o_ref.dtype)


def _merge(ys, sig, ws):
    m = ys[0].shape[0]
    tm = min(1024, m)
    tn = 512
    nj = D // tn
    ysp = pl.BlockSpec((tm, D), lambda i, j: (i, 0))
    wsp = pl.BlockSpec((D, tn), lambda i, j: (0, j))
    ssp = [pl.BlockSpec((tm, tn), functools.partial(lambda i, j, k: (i, k * nj + j), k=k)) for k in range(3)]
    return pl.pallas_call(
        _merge_kernel,
        grid=(m // tm, nj),
        in_specs=[ysp, ysp, ysp] + ssp + [wsp, wsp, wsp],
        out_specs=pl.BlockSpec((tm, tn), lambda i, j: (i, j)),
        out_shape=jax.ShapeDtypeStruct((m, D), bf16),
        compiler_params=_cparams(2),
        name="merge",
    )(ys[0], ys[1], ys[2], sig, sig, sig, ws[0], ws[1], ws[2])


def _wo_kernel(m_ref, w_ref, h_ref, g_ref, o_ref):
    o_ref[...] = h_ref[...] + g_ref[...] * jnp.dot(m_ref[...], w_ref[...], preferred_element_type=f32)


def _wo_residual(m, w_o, h, mod4, row_of_batch, k_gate, *, transposed):
    bsz, l, _ = h.shape
    if transposed:
        rows = l // GRID_W
        hv = h.reshape(bsz, rows, GRID_W * D)
        tl, nt = rows, GRID_W
        hspec = pl.BlockSpec((None, tl, D), lambda b, i: (b, 0, i))
        oshape = jax.ShapeDtypeStruct((bsz, rows, GRID_W * D), f32)
    else:
        tl = min(512, l)
        nt = l // tl
        hv = h
        hspec = pl.BlockSpec((None, tl, D), lambda b, i: (b, i, 0))
        oshape = jax.ShapeDtypeStruct((bsz, l, D), f32)
    out = pl.pallas_call(
        _wo_kernel,
        grid=(bsz, nt),
        in_specs=[pl.BlockSpec((None, tl, D), lambda b, i: (b, i, 0)),
                  pl.BlockSpec((D, D), lambda b, i: (0, 0)),
                  hspec,
                  pl.BlockSpec((None, None, 1, D), lambda b, i: (row_of_batch(b), k_gate, 0, 0))],
        out_specs=hspec,
        out_shape=oshape,
        compiler_params=_cparams(2),
        name="wo_residual",
    )(m, w_o, hv, mod4)
    return out.reshape(bsz, l, D)


def _router_kernel(x_ref, g_ref, sc_ref, sh_ref, r_ref, u_ref, gates_ref):
    x = x_ref[...]
    ms = jnp.mean(x * x, axis=-1, keepdims=True)
    u = (x * lax.rsqrt(ms + EPS) * g_ref[...]) * (1.0 + sc_ref[...]) + sh_ref[...]
    u_ref[...] = u.astype(u_ref.dtype)
    u0, u1, u2 = _split3(u)
    r0, r1, r2 = _split3(r_ref[...])
    dot = lambda a, b: jnp.dot(a, b, preferred_element_type=f32)
    logits = (dot(u0, r0) + (dot(u0, r1) + dot(u1, r0))
              + (dot(u1, r1) + dot(u0, r2) + dot(u2, r0)))
    lane = _iota(logits.shape, 1)
    neg = jnp.float32(-jnp.inf)
    lg = jnp.where(lane < N_EXPERTS, logits, neg)
    m1 = jnp.max(lg, axis=-1, keepdims=True)
    i1 = jnp.min(jnp.where(lg == m1, lane, 128), axis=-1, keepdims=True)
    lg2 = jnp.where(lane == i1, neg, lg)
    m2 = jnp.max(lg2, axis=-1, keepdims=True)
    i2 = jnp.min(jnp.where(lg2 == m2, lane, 128), axis=-1, keepdims=True)
    e2 = jnp.exp(m2 - m1)
    den = 1.0 + e2
    gates_ref[...] = jnp.where(lane == i1, 1.0 / den, 0.0) + jnp.where(lane == i2, e2 / den, 0.0)


def _norm_router(h, gain, mod4, row_of_batch, k_sc, k_sh, router_pad, out_dtype=bf16):
    bsz, l, _ = h.shape
    tl = min(512, l)
    tok = pl.BlockSpec((None, tl, D), lambda b, i: (b, i, 0))
    return pl.pallas_call(
        _router_kernel,
        grid=(bsz, l // tl),
        in_specs=[tok,
                  pl.BlockSpec((1, D), lambda b, i: (0, 0)),
                  pl.BlockSpec((None, None, 1, D), lambda b, i: (row_of_batch(b), k_sc, 0, 0)),
                  pl.BlockSpec((None, None, 1, D), lambda b, i: (row_of_batch(b), k_sh, 0, 0)),
                  pl.BlockSpec((D, 128), lambda b, i: (0, 0))],
        out_specs=[tok, pl.BlockSpec((None, tl, 128), lambda b, i: (b, i, 0))],
        out_shape=[jax.ShapeDtypeStruct((bsz, l, D), out_dtype), jax.ShapeDtypeStruct((bsz, l, 128), f32)],
        compiler_params=_cparams(2),
        name="norm_router",
    )(h, gain.reshape(1, D), mod4, mod4, router_pad)


def _up_kernel(*refs, gated):
    if gated:
        u_ref, w1_ref, w3_ref, gt_ref, o_ref, w1b_ref, w3b_ref = refs
    else:
        u_ref, w1_ref, w3_ref, o_ref, w1b_ref, w3b_ref = refs

    @pl.when(pl.program_id(2) == 0)
    def _():
        w1b_ref[...] = w1_ref[...].astype(bf16)
        w3b_ref[...] = w3_ref[...].astype(bf16)

    u = u_ref[...]
    a = jnp.dot(u, w1b_ref[...], preferred_element_type=f32)
    b = jnp.dot(u, w3b_ref[...], preferred_element_type=f32)
    act = _silu(a) * b
    if gated:
        e = pl.program_id(0)
        gt = gt_ref[...]
        lane = _iota(gt.shape, 1)
        act = act * jnp.sum(jnp.where(lane == e, gt, 0.0), axis=-1, keepdims=True)
    o_ref[...] = act.astype(o_ref.dtype)


def _swiglu_up(u, w1, w3, lead, gates=None):
    m = u.shape[0]
    _, ne, _, ff = w1.shape
    tm = min(1024, m)
    tn = 512
    nj = ff // tn
    gated = gates is not None
    wspec = pl.BlockSpec((None, None, D, tn), lambda e, j, i: (lead, e, 0, j))
    in_specs = [pl.BlockSpec((tm, D), lambda e, j, i: (i, 0)), wspec, wspec]
    args = [u, w1, w3]
    if gated:
        in_specs.append(pl.BlockSpec((tm, 128), lambda e, j, i: (i, 0)))
        args.append(gates)
    return pl.pallas_call(
        functools.partial(_up_kernel, gated=gated),
        grid=(ne, nj, m // tm),
        in_specs=in_specs,
        out_specs=pl.BlockSpec((tm, tn), lambda e, j, i: (i, e * nj + j)),
        out_shape=jax.ShapeDtypeStruct((m, ne * ff), bf16),
        scratch_shapes=[pltpu.VMEM((D, tn), bf16), pltpu.VMEM((D, tn), bf16)],
        compiler_params=_cparams(3),
        name="swiglu_up",
    )(*args)


def _down_kernel(a_ref, w_ref, h_ref, g_ref, o_ref, acc_ref, *, nk):
    k = pl.program_id(3)

    @pl.when(k == 0)
    def _():
        acc_ref[...] = jnp.zeros_like(acc_ref)

    acc_ref[...] += jnp.dot(a_ref[...], w_ref[...].astype(bf16), preferred_element_type=f32)

    @pl.when(k == nk - 1)
    def _():
        o_ref[...] = h_ref[...] + g_ref[...] * acc_ref[...]


def _down_residual(act, w2, h, mod4, row_of_batch, k_gate):
    bsz, l, kk = act.shape
    tl = min(1024, l)
    tn = 1024
    tk = 2048
    nk = kk // tk
    return pl.pallas_call(
        functools.partial(_down_kernel, nk=nk),
        grid=(bsz, l // tl, D // tn, nk),
        in_specs=[pl.BlockSpec((None, tl, tk), lambda b, i, j, k: (b, i, k)),
                  pl.BlockSpec((tk, tn), lambda b, i, j, k: (k, j)),
                  pl.BlockSpec((None, tl, tn), lambda b, i, j, k: (b, i, j)),
                  pl.BlockSpec((None, None, 1, tn), lambda b, i, j, k: (row_of_batch(b), k_gate, 0, j))],
        out_specs=pl.BlockSpec((None, tl, tn), lambda b, i, j, k: (b, i, j)),
        out_shape=jax.ShapeDtypeStruct((bsz, l, D), f32),
        scratch_shapes=[pltpu.VMEM((tl, tn), f32)],
        compiler_params=_cparams(4),
        name="down_residual",
    )(act, w2, h, mod4)


MOE_TM = 512
MOE_TT = 512
NACT_LANE = 127


def _moe_count_kernel(gates_ref, pre_ref, cnt_ref, carry_ref):
    i = pl.program_id(0)

    @pl.when(i == 0)
    def _():
        carry_ref[...] = jnp.zeros_like(carry_ref)

    tt = gates_ref.shape[0]
    a = jnp.where(gates_ref[...] > 0.0, 1.0, 0.0)
    strict = jnp.where(_iota((tt, tt), 1) < _iota((tt, tt), 0), 1.0, 0.0).astype(bf16)
    carry = carry_ref[0:1, :]
    pre_ref[...] = jnp.dot(strict, a.astype(bf16), preferred_element_type=f32) + carry
    carry_ref[...] = jnp.broadcast_to(carry + jnp.sum(a, axis=0, keepdims=True), carry_ref.shape)
    cnt_ref[...] = carry_ref[...]


def _moe_place_kernel(gates_ref, pre_ref, cnt_ref, pos_ref, gv_ref, tile_ref, *, n_tiles):
    tt = gates_ref.shape[0]
    tm = float(MOE_TM)
    lane1 = _iota((1, 128), 1)
    cnt = cnt_ref[0:1, :]
    gsz = jnp.floor((cnt + (tm - 1.0)) * (1.0 / tm)) * tm
    upper = jnp.where(_iota((128, 128), 0) < _iota((128, 128), 1), 1.0, 0.0).astype(bf16)
    base = _mm_x3(jnp.broadcast_to(gsz, (8, 128)), upper)[0:1]
    gates = gates_ref[...]
    act = gates > 0.0
    pos = base + pre_ref[...]
    big = jnp.float32(1e9)
    p_lo = jnp.min(jnp.where(act, pos, big), axis=-1, keepdims=True)
    p_hi = jnp.max(jnp.where(act, pos, -1.0), axis=-1, keepdims=True)
    g_lo = jnp.sum(jnp.where(act & (pos == p_lo), gates, 0.0), axis=-1, keepdims=True)
    g_hi = jnp.where(p_hi != p_lo,
                     jnp.sum(jnp.where(act & (pos == p_hi), gates, 0.0), axis=-1, keepdims=True), 0.0)
    lane = _iota((tt, 128), 1)
    posf = jnp.where(lane == 0, p_lo, 0.0) + jnp.where(lane == 1, p_hi, 0.0)
    pos_ref[...] = posf.T[0:8].astype(jnp.int32)
    gv_ref[...] = jnp.where(lane == 0, g_lo, 0.0) + jnp.where(lane == 1, g_hi, 0.0)
    end = base + gsz
    start_j = lane1.astype(f32) * tm
    texp = jnp.zeros((1, 128), f32)
    for e in range(N_EXPERTS - 1):
        end_e = jnp.sum(jnp.where(lane1 == e, end, 0.0), axis=-1, keepdims=True)
        texp = texp + jnp.where(start_j >= end_e, 1.0, 0.0)
    total = jnp.sum(jnp.where(lane1 == N_EXPERTS - 1, end, 0.0), axis=-1, keepdims=True)
    table = jnp.where(lane1 == NACT_LANE, total * (1.0 / tm), texp)
    tile_ref[...] = jnp.broadcast_to(table, (8, 128)).astype(jnp.int32)


def _moe_scatter_kernel(pos_ref, v_ref, xs_in_ref, xs_ref, sem, *, dump_base):
    del xs_in_ref
    tt = v_ref.shape[0]

    def row_copy(r, dst_row):
        return pltpu.make_async_copy(v_ref.at[pl.ds(r, 1)], xs_ref.at[pl.ds(dst_row, 1)], sem)

    def body(r, carry):
        p0 = pos_ref[0, r]
        p1 = pos_ref[1, r]
        p1 = jnp.where(p1 == p0, dump_base + r, p1)
        row_copy(r, p0).start()
        row_copy(r, p1).start()
        return carry

    lax.fori_loop(0, tt, body, 0, unroll=8)
    for _ in range(2):
        pltpu.make_async_copy(v_ref, xs_ref.at[pl.ds(0, tt)], sem).wait()


def _new_expert_tile(tile_ref, t):
    return (t == 0) | (tile_ref[t] != tile_ref[jnp.maximum(t - 1, 0)])


def _moe_gup_kernel(tile_ref, x_ref, w1_ref, w3_ref, o_ref, w1b_ref, w3b_ref):
    t = pl.program_id(1)
    active = t < tile_ref[NACT_LANE]

    @pl.when(active & _new_expert_tile(tile_ref, t))
    def _():
        w1b_ref[...] = w1_ref[...].astype(bf16)
        w3b_ref[...] = w3_ref[...].astype(bf16)

    @pl.when(active)
    def _():
        x = x_ref[...].astype(bf16)
        a = jnp.dot(x, w1b_ref[...], preferred_element_type=f32)
        b = jnp.dot(x, w3b_ref[...], preferred_element_type=f32)
        o_ref[...] = (_silu(a) * b).astype(o_ref.dtype)

    @pl.when(jnp.logical_not(active))
    def _():
        o_ref[...] = jnp.zeros_like(o_ref)


def _moe_gdown_kernel(tile_ref, a_ref, w_ref, o_ref, wb_ref):
    t = pl.program_id(1)
    active = t < tile_ref[NACT_LANE]

    @pl.when(active & _new_expert_tile(tile_ref, t))
    def _():
        wb_ref[...] = w_ref[...].astype(bf16)

    @pl.when(active)
    def _():
        o_ref[...] = jnp.dot(a_ref[...], wb_ref[...], preferred_element_type=f32)

    @pl.when(jnp.logical_not(active))
    def _():
        o_ref[...] = jnp.zeros_like(o_ref)


def _moe_combine_kernel(pos_ref, gv_ref, h_ref, gf_ref, ys_ref, o_ref, buf0, buf1, sem):
    tt = h_ref.shape[0]

    def row_copy(src_row, buf, r):
        return pltpu.make_async_copy(ys_ref.at[pl.ds(src_row, 1)], buf.at[pl.ds(r, 1)], sem)

    def body(r, carry):
        row_copy(pos_ref[0, r], buf0, r).start()
        row_copy(pos_ref[1, r], buf1, r).start()
        return carry

    lax.fori_loop(0, tt, body, 0, unroll=8)
    for buf in (buf0, buf1):
        pltpu.make_async_copy(ys_ref.at[pl.ds(0, tt)], buf, sem).wait()
    gv = gv_ref[...]
    lane = _iota(gv.shape, 1)
    g0 = jnp.sum(jnp.where(lane == 0, gv, 0.0), axis=-1, keepdims=True)
    g1 = jnp.sum(jnp.where(lane == 1, gv, 0.0), axis=-1, keepdims=True)
    o_ref[...] = h_ref[...] + gf_ref[...] * (g0 * buf0[...] + g1 * buf1[...])


def _moe_sparse(v32, gates, h, w1, w3, w2, lead, mod4, row_of_batch, k_gate):
    bsz, l, _ = h.shape
    n_tok = bsz * l
    tt = min(MOE_TT, n_tok)
    n_tt = n_tok // tt
    n_tiles = (2 * n_tok) // MOE_TM + N_EXPERTS
    rows = n_tiles * MOE_TM
    ff = w1.shape[3]
    vf = v32.reshape(n_tok, D)
    gf = gates.reshape(n_tok, 128)

    pre, cnt = pl.pallas_call(
        _moe_count_kernel,
        grid=(n_tt,),
        in_specs=[pl.BlockSpec((tt, 128), lambda i: (i, 0))],
        out_specs=[pl.BlockSpec((tt, 128), lambda i: (i, 0)), pl.BlockSpec((8, 128), lambda i: (0, 0))],
        out_shape=[jax.ShapeDtypeStruct((n_tok, 128), f32), jax.ShapeDtypeStruct((8, 128), f32)],
        scratch_shapes=[pltpu.VMEM((8, 128), f32)],
        compiler_params=_cparams(1),
        name="moe_count",
    )(gf)

    pos, gv, table = pl.pallas_call(
        functools.partial(_moe_place_kernel, n_tiles=n_tiles),
        grid=(n_tt,),
        in_specs=[pl.BlockSpec((tt, 128), lambda i: (i, 0)), pl.BlockSpec((tt, 128), lambda i: (i, 0)),
                  pl.BlockSpec((8, 128), lambda i: (0, 0))],
        out_specs=[pl.BlockSpec((8, tt), lambda i: (0, i)), pl.BlockSpec((tt, 128), lambda i: (i, 0)),
                   pl.BlockSpec((8, 128), lambda i: (0, 0))],
        out_shape=[jax.ShapeDtypeStruct((8, n_tok), jnp.int32), jax.ShapeDtypeStruct((n_tok, 128), f32),
                   jax.ShapeDtypeStruct((8, 128), jnp.int32)],
        compiler_params=_cparams(1),
        name="moe_place",
    )(gf, pre, cnt)
    tile_tab = table[0]

    smem_pos = pl.BlockSpec((8, tt), lambda i: (0, i), memory_space=pltpu.SMEM)
    xs = pl.pallas_call(
        functools.partial(_moe_scatter_kernel, dump_base=rows),
        grid=(n_tt,),
        in_specs=[smem_pos, pl.BlockSpec((tt, D), lambda i: (i, 0)), pl.BlockSpec(memory_space=pl.ANY)],
        out_specs=pl.BlockSpec(memory_space=pl.ANY),
        out_shape=jax.ShapeDtypeStruct((rows + tt, D), f32),
        scratch_shapes=[pltpu.SemaphoreType.DMA],
        input_output_aliases={2: 0},
        compiler_params=_cparams(1),
        name="moe_scatter",
    )(pos, vf, jnp.zeros((rows + tt, D), f32))

    assert n_tiles < NACT_LANE
    tn = 512
    nj = ff // tn
    last_tile = lambda tab: jnp.maximum(tab[NACT_LANE] - 1, 0)
    act = pl.pallas_call(
        _moe_gup_kernel,
        grid_spec=pltpu.PrefetchScalarGridSpec(
            num_scalar_prefetch=1,
            grid=(nj, n_tiles),
            in_specs=[pl.BlockSpec((MOE_TM, D), lambda j, t, tab: (jnp.minimum(t, last_tile(tab)), 0)),
                      pl.BlockSpec((None, None, D, tn),
                                   lambda j, t, tab: (lead, tab[jnp.minimum(t, last_tile(tab))], 0, j)),
                      pl.BlockSpec((None, None, D, tn),
                                   lambda j, t, tab: (lead, tab[jnp.minimum(t, last_tile(tab))], 0, j))],
            out_specs=pl.BlockSpec((MOE_TM, tn), lambda j, t, tab: (t, j)),
            scratch_shapes=[pltpu.VMEM((D, tn), bf16), pltpu.VMEM((D, tn), bf16)]),
        out_shape=jax.ShapeDtypeStruct((rows, ff), bf16),
        compiler_params=_cparams(2),
        name="moe_up",
    )(tile_tab, xs, w1, w3)

    tnd = 1024
    ys = pl.pallas_call(
        _moe_gdown_kernel,
        grid_spec=pltpu.PrefetchScalarGridSpec(
            num_scalar_prefetch=1,
            grid=(D // tnd, n_tiles),
            in_specs=[pl.BlockSpec((MOE_TM, ff), lambda j, t, tab: (jnp.minimum(t, last_tile(tab)), 0)),
                      pl.BlockSpec((None, None, ff, tnd),
                                   lambda j, t, tab: (lead, tab[jnp.minimum(t, last_tile(tab))], 0, j))],
            out_specs=pl.BlockSpec((MOE_TM, tnd), lambda j, t, tab: (t, j)),
            scratch_shapes=[pltpu.VMEM((ff, tnd), bf16)]),
        out_shape=jax.ShapeDtypeStruct((rows, D), f32),
        compiler_params=_cparams(2),
        name="moe_down",
    )(tile_tab, act, w2)

    tpb = l // tt if l >= tt else 1
    out = pl.pallas_call(
        _moe_combine_kernel,
        grid=(n_tt,),
        in_specs=[smem_pos, pl.BlockSpec((tt, 128), lambda i: (i, 0)), pl.BlockSpec((tt, D), lambda i: (i, 0)),
                  pl.BlockSpec((None, None, 1, D), lambda i: (row_of_batch(i // tpb), k_gate, 0, 0)),
                  pl.BlockSpec(memory_space=pl.ANY)],
        out_specs=pl.BlockSpec((tt, D), lambda i: (i, 0)),
        out_shape=jax.ShapeDtypeStruct((n_tok, D), f32),
        scratch_shapes=[pltpu.VMEM((tt, D), f32), pltpu.VMEM((tt, D), f32), pltpu.SemaphoreType.DMA],
        compiler_params=_cparams(1),
        name="moe_combine",
    )(pos, gv, h.reshape(n_tok, D), mod4, ys)
    return out.reshape(bsz, l, D)


def _block_ones(n, bs):
    idx = jnp.arange(n) // bs
    return (idx[:, None] == idx[None, :]).astype(bf16)


def _pad_cols(w, n):
    return jnp.pad(w, ((0, 0), (0, n - w.shape[1])))


def _rwkv_params(li, mu, w0, w_up, a0, a_up, g_up, k_k, k_a, r_k, ln_w, ln_b):
    wup = jnp.zeros((2, 256, D), f32)
    aup = jnp.zeros((2, 256, D), f32)
    for d in range(2):
        wup = wup.at[d, d * W_LORA:(d + 1) * W_LORA].set(w_up[li, d])
        o = (RW_OFF_G - 2 * A_LORA) - RW_OFF_A + d * A_LORA
        aup = aup.at[d, o:o + A_LORA].set(a_up[li, d])
    return {
        "mu": jnp.pad(mu[li], (0, N_RWKV_PAD - N_RWKV)).reshape(1, N_RWKV_PAD),
        "w0": w0[li], "wup": wup.astype(bf16), "a0": a0[li], "aup": aup.astype(bf16),
        "gup": g_up[li].astype(bf16), "k_k": k_k[li].reshape(1, D), "k_a": k_a[li].reshape(1, D),
        "r_k": r_k[li].reshape(1, D), "ln_w": ln_w[li].reshape(1, D), "ln_b": ln_b[li].reshape(1, D),
        "ones_bd": _block_ones(RW_GW, HEAD),
    }


def _ssd_params(li, conv_w, conv_b, a_log, dt_bias, d_skip):
    a = -jnp.exp(a_log[li].astype(f32))
    hid = jnp.arange(D) // SSM_P
    sel = jnp.arange(128)[:, None]
    e = jnp.stack([(sel == hid[None, :] + d * SSM_HEADS) for d in range(2)]).astype(bf16)
    return {
        "cw": conv_w[li], "cb": conv_b[li].reshape(1, D_XBC),
        "dtb": jnp.pad(dt_bias[li].reshape(1, 2 * SSM_HEADS), ((0, 0), (0, 64))),
        "a": jnp.pad(a.reshape(1, 2 * SSM_HEADS), ((0, 0), (0, 64))),
        "dsk": jnp.stack([jnp.repeat(d_skip[li], SSM_P), jnp.zeros((D,), f32)]).reshape(2, 1, D),
        "e": e,
    }


def kernel(x, c, ctx, c_ctx, ada_w, ada_b, norm_mix, norm_ffn, norm_final, w_in, lru_conv_w, lru_conv_b, lru_gate_w, lru_gate_b, lru_lambda, rwkv_mu, rwkv_w0, rwkv_w_up, rwkv_a0, rwkv_a_up, rwkv_g_up, rwkv_k_k, rwkv_k_a, rwkv_r_k, rwkv_ln_w, rwkv_ln_b, ssm_conv_w, ssm_conv_b, ssm_a_log, ssm_dt_bias, ssm_d, ssm_norm_w, w_out_lru, w_out_rwkv, w_out_ssm, w_o, ffn_w1, ffn_w3, ffn_w2, moe_router, moe_w1, moe_w3, moe_w2):
    bsz, l, _ = x.shape
    depth = ada_w.shape[0]
    off_lru = 3 * D
    off_rwkv = off_lru + 2 * D
    off_ssm = off_rwkv + N_RWKV
    cvec = jnp.zeros((8, D), f32).at[:bsz].set(c).at[bsz].set(c_ctx)
    lat_row = lambda b: b
    ctx_row = lambda b: bsz

    h_lat, h_ctx = x, ctx
    for li in range(depth):
        last = li == depth - 1
        odd = li % 2 == 1
        mod4 = _ada(cvec, ada_w, ada_b, li).reshape(8, 6, 1, D)

        w_ss = _pad_cols(w_in[li, :, off_ssm:], N_SSM_PAD)[None]
        lru_gw = [jnp.concatenate([lru_gate_w[li, d, 0], lru_gate_w[li, d, 1]], axis=-1).astype(bf16)
                  for d in range(2)]
        rp = _rwkv_params(li, rwkv_mu, rwkv_w0, rwkv_w_up, rwkv_a0, rwkv_a_up, rwkv_g_up, rwkv_k_k,
                          rwkv_k_a, rwkv_r_k, rwkv_ln_w, rwkv_ln_b)
        sp = _ssd_params(li, ssm_conv_w, ssm_conv_b, ssm_a_log, ssm_dt_bias, ssm_d)
        w_outs = [w_out_lru[li].astype(bf16), w_out_rwkv[li].astype(bf16), w_out_ssm[li].astype(bf16)]
        w_o_b = w_o[li].astype(bf16)

        def token_mix(u, states, need_out):
            lx = u.shape[1]
            um = u.reshape(bsz * lx, D)
            p_lru = _proj(um, w_in, li, tn=1024, col0=off_lru, n=2 * D).reshape(bsz, lx, 2 * D)
            p_rw = _proj(um, w_in, li, tn=1024, col0=off_rwkv, n=N_RWKV_PAD).reshape(bsz, lx, N_RWKV_PAD)
            p_ss = _proj(um, w_ss, 0, tn=768).reshape(bsz, lx, N_SSM_PAD)
            lru_s, rw_s, ss_s = states
            cw, cb = lru_conv_w[li], lru_conv_b[li].reshape(1, D)
            hb, hl_b = _lru_pass(p_lru, cw, cb, lru_gw[1], lru_gate_b[li, 1], lru_lambda[li, 1].reshape(1, D),
                                 lru_s[1], None, reverse=True)
            ya, hl_f = _lru_pass(p_lru, cw, cb, lru_gw[0], lru_gate_b[li, 0], lru_lambda[li, 0].reshape(1, D),
                                 lru_s[0], hb, reverse=False)
            r, v, kk, g, bonus, lw, key, bvec = _rwkv_feat(p_rw, rp)
            y2, rw_fin = _rwkv_scan(r, v, kk, lw, key, bvec, rw_s)
            ys2, ss_fin = _ssd(p_ss, sp, ss_s)
            new_states = ((hl_f, hl_b), rw_fin, ss_fin)
            if not need_out:
                return None, new_states
            yb = _rwkv_out(y2, bonus, g, rp["ln_w"], rp["ln_b"], rp["ones_bd"])
            yc = _ssd_out(ys2, p_ss, ssm_norm_w[li].reshape(1, D))
            sig = _proj(um, w_in, li, tn=1024, col0=0, n=off_lru, act="sigmoid", out_dtype=bf16)
            m = _merge([ya.reshape(-1, D), yb.reshape(-1, D), yc.reshape(-1, D)], sig, w_outs)
            return m.reshape(bsz, lx, D), new_states

        zero_states = ((jnp.zeros((bsz, 1, D), f32), jnp.zeros((bsz, 1, D), f32)),
                       jnp.zeros((2, bsz, RW_NG, RW_GW, RW_GW), f32),
                       jnp.zeros((2, bsz, SSM_G, SSM_N, 8 * SSM_P), f32))

        u_ctx = _norm(h_ctx, norm_mix[li], mod4, ctx_row, 1, 0)
        u_lat = _norm(h_lat, norm_mix[li], mod4, lat_row, 1, 0, transposed=odd)
        m_ctx, ctx_states = token_mix(u_ctx, zero_states, not last)
        m_lat, _ = token_mix(u_lat, ctx_states, True)
        h_lat = _wo_residual(m_lat, w_o_b, h_lat, mod4, lat_row, 2, transposed=odd)
        if not last:
            h_ctx = _wo_residual(m_ctx, w_o_b, h_ctx, mod4, ctx_row, 2, transposed=False)

        j = li // 2
        streams = [(h_lat, lat_row)] + ([] if last else [(h_ctx, ctx_row)])
        outs = []
        for h, row_fn in streams:
            lx = h.shape[1]
            if not odd:
                v = _norm(h, norm_ffn[li], mod4, row_fn, 4, 3)
                act = _swiglu_up(v.reshape(bsz * lx, D), ffn_w1[:, None], ffn_w3[:, None], j)
                w2 = ffn_w2[j].astype(bf16)
            elif row_fn is lat_row:
                router_pad = _pad_cols(moe_router[j], 128)
                v32, gates = _norm_router(h, norm_ffn[li], mod4, row_fn, 4, 3, router_pad, out_dtype=f32)
                outs.append(_moe_sparse(v32, gates, h, moe_w1, moe_w3, moe_w2, j, mod4, row_fn, 5))
                continue
            else:
                router_pad = _pad_cols(moe_router[j], 128)
                v, gates = _norm_router(h, norm_ffn[li], mod4, row_fn, 4, 3, router_pad)
                act = _swiglu_up(v.reshape(bsz * lx, D), moe_w1, moe_w3, j, gates.reshape(bsz * lx, 128))
                w2 = moe_w2[j].astype(bf16).reshape(N_EXPERTS * D_FF_EXPERT, D)
            outs.append(_down_residual(act.reshape(bsz, lx, -1), w2, h, mod4, row_fn, 5))
        h_lat = outs[0]
        if not last:
            h_ctx = outs[1]
    return _final_norm(h_lat, norm_final)
```

```python
import functools
import math

import jax
import jax.numpy as jnp
from jax import lax
from jax.experimental import pallas as pl
from jax.experimental.pallas import tpu as pltpu

f32 = jnp.float32
bf16 = jnp.bfloat16

D = 2048
GRID_W = 64
EPS = 1e-6
CONV_W = 4
CONV_LEFT = 2
HALO = 8

LRU_BLOCKS = 16
LRU_BS = D // LRU_BLOCKS
LRU_C = 8.0

HEADS = 32
HEAD = 64
W_LORA = 96
A_LORA = 96
G_LORA = 256
GN_EPS = 64e-5
N_RWKV = 3 * D + 2 * W_LORA + 2 * A_LORA + G_LORA
N_RWKV_PAD = 7168
RW_OFF_W = 3 * D
RW_OFF_A = 3 * D + 128
RW_OFF_G = 3 * D + 2 * W_LORA + 2 * A_LORA
RW_CHUNK = 64
RW_GW = 256
RW_NG = D // RW_GW

SSM_HEADS = 32
SSM_P = 64
SSM_N = 128
SSM_G = 4
SSM_Q = 128
D_XBC = D + 2 * SSM_G * SSM_N
N_SSM = D + D_XBC + 2 * SSM_HEADS
N_SSM_PAD = 5376
SS_OFF_DT = D + D_XBC

D_FF = 3 * D
N_EXPERTS = 8
D_FF_EXPERT = D_FF // 2

VMEM_LIMIT = 56 * 1024 * 1024


def _cparams(n_axes, vmem=VMEM_LIMIT):
    return pltpu.CompilerParams(dimension_semantics=("arbitrary",) * n_axes, vmem_limit_bytes=vmem)


def _mm(a, b):
    return jnp.dot(a.astype(bf16), b.astype(bf16), preferred_element_type=f32)


def _mm_nt(a, b):
    return lax.dot_general(a.astype(bf16), b.astype(bf16), (((1,), (1,)), ((), ())),
                           preferred_element_type=f32)


def _split3(x):
    x0 = x.astype(bf16)
    r = x - x0.astype(f32)
    x1 = r.astype(bf16)
    r = r - x1.astype(f32)
    return x0, x1, r.astype(bf16)


def _mm_x3(x, e):
    x0, x1, x2 = _split3(x)
    return (jnp.dot(x0, e, preferred_element_type=f32) + jnp.dot(x1, e, preferred_element_type=f32)
            + jnp.dot(x2, e, preferred_element_type=f32))


def _mm_x2(x, e):
    x0 = x.astype(bf16)
    x1 = (x - x0.astype(f32)).astype(bf16)
    return jnp.dot(x0, e, preferred_element_type=f32) + jnp.dot(x1, e, preferred_element_type=f32)


def _mm_e3(e, x):
    x0, x1, x2 = _split3(x)
    return (jnp.dot(e, x0, preferred_element_type=f32) + jnp.dot(e, x1, preferred_element_type=f32)
            + jnp.dot(e, x2, preferred_element_type=f32))


def _softplus(x):
    return jnp.maximum(x, 0.0) + jnp.log1p(jnp.exp(-jnp.abs(x)))


def _sigmoid(x):
    return 0.5 * jnp.tanh(0.5 * x) + 0.5


def _silu(x):
    return x * _sigmoid(x)


def _iota(shape, dim):
    return lax.broadcasted_iota(jnp.int32, shape, dim)


def _ada_kernel(c_ref, w_ref, b_ref, o_ref):
    cv = c_ref[...]
    o_ref[...] = _mm(_silu(cv), w_ref[...]) + b_ref[...]


def _ada(cvec, w, b, li):
    depth, _, n = w.shape
    tn = 1536
    return pl.pallas_call(
        _ada_kernel,
        grid=(n // tn,),
        in_specs=[pl.BlockSpec((8, D), lambda j: (0, 0)),
                  pl.BlockSpec((None, D, tn), lambda j: (li, 0, j)),
                  pl.BlockSpec((None, 1, tn), lambda j: (li, 0, j))],
        out_specs=pl.BlockSpec((8, tn), lambda j: (0, j)),
        out_shape=jax.ShapeDtypeStruct((8, n), f32),
        compiler_params=_cparams(1),
        name="ada",
    )(cvec, w, b.reshape(depth, 1, n))


def _norm_kernel(x_ref, g_ref, sc_ref, sh_ref, o_ref):
    x = x_ref[...]
    ms = jnp.mean(x * x, axis=-1, keepdims=True)
    xn = x * lax.rsqrt(ms + EPS) * g_ref[...]
    o_ref[...] = (xn * (1.0 + sc_ref[...]) + sh_ref[...]).astype(o_ref.dtype)


def _norm(h, gain, mod4, row_of_batch, k_sc, k_sh, *, transposed=False, out_dtype=bf16):
    bsz, l, _ = h.shape
    if transposed:
        rows = l // GRID_W
        hin = h.reshape(bsz, rows, GRID_W * D)
        tl = rows
        nt = GRID_W
        in_spec = pl.BlockSpec((None, tl, D), lambda b, i: (b, 0, i))
    else:
        tl = min(512, l)
        nt = l // tl
        hin = h
        in_spec = pl.BlockSpec((None, tl, D), lambda b, i: (b, i, 0))
    return pl.pallas_call(
        _norm_kernel,
        grid=(bsz, nt),
        in_specs=[in_spec,
                  pl.BlockSpec((1, D), lambda b, i: (0, 0)),
                  pl.BlockSpec((None, None, 1, D), lambda b, i: (row_of_batch(b), k_sc, 0, 0)),
                  pl.BlockSpec((None, None, 1, D), lambda b, i: (row_of_batch(b), k_sh, 0, 0))],
        out_specs=pl.BlockSpec((None, tl, D), lambda b, i: (b, i, 0)),
        out_shape=jax.ShapeDtypeStruct((bsz, l, D), out_dtype),
        compiler_params=_cparams(2),
        name="norm",
    )(hin, gain.reshape(1, D), mod4, mod4)


def _plain_norm_kernel(x_ref, g_ref, o_ref):
    x = x_ref[...]
    ms = jnp.mean(x * x, axis=-1, keepdims=True)
    o_ref[...] = (x * lax.rsqrt(ms + EPS) * g_ref[...]).astype(o_ref.dtype)


def _final_norm(h, gain):
    bsz, l, _ = h.shape
    tl = min(512, l)
    return pl.pallas_call(
        _plain_norm_kernel,
        grid=(bsz, l // tl),
        in_specs=[pl.BlockSpec((None, tl, D), lambda b, i: (b, i, 0)),
                  pl.BlockSpec((1, D), lambda b, i: (0, 0))],
        out_specs=pl.BlockSpec((None, tl, D), lambda b, i: (b, i, 0)),
        out_shape=jax.ShapeDtypeStruct((bsz, l, D), f32),
        compiler_params=_cparams(2),
        name="final_norm",
    )(h, gain.reshape(1, D))


def _proj_kernel(u_ref, w_ref, o_ref, wb_ref, *, act, n_valid):
    @pl.when(pl.program_id(1) == 0)
    def _():
        wb_ref[...] = w_ref[...].astype(bf16)

    acc = jnp.dot(u_ref[...], wb_ref[...], preferred_element_type=f32)
    if act == "sigmoid":
        acc = _sigmoid(acc)
    if n_valid is not None:
        col = pl.program_id(0) * acc.shape[1] + _iota(acc.shape, 1)
        acc = jnp.where(col < n_valid, acc, 0.0)
    o_ref[...] = acc.astype(o_ref.dtype)


def _proj(u, w, li, *, tn, col0=0, n=None, act=None, out_dtype=f32):
    m, k = u.shape
    n = w.shape[2] if n is None else n
    assert col0 % tn == 0 and n % tn == 0 and col0 + n - tn < w.shape[2]
    n_valid = w.shape[2] - col0 if col0 + n > w.shape[2] else None
    off = col0 // tn
    tm = min(1024, m)
    return pl.pallas_call(
        functools.partial(_proj_kernel, act=act, n_valid=n_valid),
        grid=(n // tn, m // tm),
        in_specs=[pl.BlockSpec((tm, k), lambda j, i: (i, 0)),
                  pl.BlockSpec((None, k, tn), lambda j, i: (li, 0, j + off))],
        out_specs=pl.BlockSpec((tm, tn), lambda j, i: (i, j)),
        out_shape=jax.ShapeDtypeStruct((m, n), out_dtype),
        scratch_shapes=[pltpu.VMEM((k, tn), bf16)],
        compiler_params=_cparams(2),
        name="proj",
    )(u, w)


def _lru_kernel(*refs, reverse, final, tl, nt):
    if final:
        (x_ref, prev_ref, next_ref, gate_ref, hb_ref, cw_ref, cb_ref, gw_ref, gb_ref, lam_ref, h0_ref,
         out_ref, hlast_ref, xe_ref, a_ref, bx_ref, hs_ref, h_ref) = refs
    else:
        (x_ref, prev_ref, next_ref, cw_ref, cb_ref, gw_ref, gb_ref, lam_ref, h0_ref,
         out_ref, hlast_ref, xe_ref, a_ref, bx_ref, hs_ref, h_ref) = refs
    i = pl.program_id(1)
    t = (nt - 1 - i) if reverse else i

    @pl.when(i == 0)
    def _():
        h_ref[...] = h0_ref[...]

    zero = jnp.zeros((HALO, D), f32)
    xe_ref[0:HALO, :] = jnp.where(t > 0, prev_ref[...], zero)
    xe_ref[HALO:HALO + tl, :] = x_ref[...]
    xe_ref[HALO + tl:2 * HALO + tl, :] = jnp.where(t < nt - 1, next_ref[...], zero)

    for n in range(LRU_BLOCKS):
        cs = slice(n * LRU_BS, (n + 1) * LRU_BS)
        xc = cb_ref[:, cs]
        for tap in range(CONV_W):
            r0 = HALO - CONV_LEFT + tap
            xc = xc + xe_ref[r0:r0 + tl, cs] * cw_ref[tap:tap + 1, cs]
        g = _mm(xc, gw_ref[n])
        rec = _sigmoid(g[:, :LRU_BS] + gb_ref[0:1, cs])
        inp = _sigmoid(g[:, LRU_BS:] + gb_ref[1:2, cs])
        log_a = -LRU_C * rec * _softplus(-lam_ref[:, cs])
        a_ref[:, cs] = jnp.exp(log_a)
        th = jnp.tanh(log_a)
        bx_ref[:, cs] = jnp.sqrt(-2.0 * th / (1.0 - th)) * inp * xc

    def body(s, h):
        tt = (tl - 1 - s) if reverse else s
        h = a_ref[pl.ds(tt, 1), :] * h + bx_ref[pl.ds(tt, 1), :]
        hs_ref[pl.ds(tt, 1), :] = h
        return h

    h = lax.fori_loop(0, tl, body, h_ref[...], unroll=8)
    h_ref[...] = h

    @pl.when(i == nt - 1)
    def _():
        hlast_ref[...] = h

    if final:
        for n in range(LRU_BLOCKS):
            cs = slice(n * LRU_BS, (n + 1) * LRU_BS)
            y = (hs_ref[:, cs] + hb_ref[:, cs]) * jax.nn.gelu(gate_ref[:, cs])
            out_ref[:, cs] = y.astype(out_ref.dtype)
    else:
        out_ref[...] = hs_ref[...]


def _lru_pass(p, cw, cb, gw, gb, lam, h0, hb, *, reverse):
    bsz, l, _ = p.shape
    tl = min(256, l)
    nt = l // tl
    final = hb is not None
    nh = l // HALO
    tpb = tl // HALO

    def tmap(i):
        return (nt - 1 - i) if reverse else i

    main = pl.BlockSpec((None, tl, D), lambda b, i: (b, tmap(i), 0))
    in_specs = [main,
                pl.BlockSpec((None, HALO, D), lambda b, i: (b, jnp.maximum(tmap(i) * tpb - 1, 0), 0)),
                pl.BlockSpec((None, HALO, D), lambda b, i: (b, jnp.minimum((tmap(i) + 1) * tpb, nh - 1), 0))]
    args = [p, p, p]
    if final:
        in_specs += [pl.BlockSpec((None, tl, D), lambda b, i: (b, tmap(i), 1)), main]
        args += [p, hb]
    in_specs += [pl.BlockSpec((CONV_W, D), lambda b, i: (0, 0)),
                 pl.BlockSpec((1, D), lambda b, i: (0, 0)),
                 pl.BlockSpec((LRU_BLOCKS, LRU_BS, 2 * LRU_BS), lambda b, i: (0, 0, 0)),
                 pl.BlockSpec((2, D), lambda b, i: (0, 0)),
                 pl.BlockSpec((1, D), lambda b, i: (0, 0)),
                 pl.BlockSpec((None, 1, D), lambda b, i: (b, 0, 0))]
    args += [cw, cb, gw, gb, lam, h0]
    out, hlast = pl.pallas_call(
        functools.partial(_lru_kernel, reverse=reverse, final=final, tl=tl, nt=nt),
        grid=(bsz, nt),
        in_specs=in_specs,
        out_specs=[main, pl.BlockSpec((None, 1, D), lambda b, i: (b, 0, 0))],
        out_shape=[jax.ShapeDtypeStruct((bsz, l, D), bf16 if final else f32),
                   jax.ShapeDtypeStruct((bsz, 1, D), f32)],
        scratch_shapes=[pltpu.VMEM((tl + 2 * HALO, D), f32), pltpu.VMEM((tl, D), f32),
                        pltpu.VMEM((tl, D), f32), pltpu.VMEM((tl, D), f32), pltpu.VMEM((1, D), f32)],
        compiler_params=_cparams(2),
        name="lru_fwd" if final else "lru_bwd",
    )(*args)
    return out, hlast


def _rwkv_feat_kernel(p_ref, prev_ref, next_ref, mu_ref, w0_ref, wup_ref, a0_ref, aup_ref, gup_ref,
                      kk_ref, ka_ref, rk_ref, ones_ref,
                      r_out, v_out, kkv_out, g_out, bonus_out, lw_out, key_out, b_out,
                      pe_ref, *, tl, nt):
    i = pl.program_id(1)
    zero = jnp.zeros((HALO, N_RWKV_PAD), f32)
    pe_ref[0:HALO, :] = jnp.where(i > 0, prev_ref[...], zero)
    pe_ref[HALO:HALO + tl, :] = p_ref[...]
    pe_ref[HALO + tl:2 * HALO + tl, :] = jnp.where(i < nt - 1, next_ref[...], zero)

    def shifted(c0, width):
        cs = slice(c0, c0 + width)
        p = pe_ref[HALO:HALO + tl, cs]
        nb = 0.5 * (pe_ref[HALO - 1:HALO - 1 + tl, cs] + pe_ref[HALO + 1:HALO + 1 + tl, cs])
        return p + mu_ref[:, cs] * (nb - p)

    win_w = jnp.tanh(shifted(RW_OFF_W, 256)).astype(bf16)
    win_a = shifted(RW_OFF_A, 256).astype(bf16)
    win_g = _sigmoid(shifted(RW_OFF_G, 256)).astype(bf16)
    ones_bd = ones_ref[...]

    for g in range(RW_NG):
        cs = slice(g * RW_GW, (g + 1) * RW_GW)
        r = shifted(g * RW_GW, RW_GW)
        k = shifted(D + g * RW_GW, RW_GW)
        v = shifted(2 * D + g * RW_GW, RW_GW)
        kf = k * kk_ref[:, cs]
        kk = kf * lax.rsqrt(_mm_x2(kf * kf, ones_bd) + 1e-12)
        ksum = None
        for d in range(2):
            wv = -_softplus(-(w0_ref[d:d + 1, cs] + jnp.dot(win_w, wup_ref[d, :, cs],
                                                              preferred_element_type=f32))) - 0.5
            lw_out[d, :, cs] = -jnp.exp(wv)
            a = _sigmoid(a0_ref[d:d + 1, cs] + jnp.dot(win_a, aup_ref[d, :, cs],
                                                       preferred_element_type=f32))
            key = k * (1.0 + (a - 1.0) * ka_ref[:, cs])
            key_out[d, :, cs] = key
            b_out[d, :, cs] = kk * a
            ksum = key if ksum is None else ksum + key
        r_out[:, cs] = r
        v_out[:, cs] = v
        kkv_out[:, cs] = kk
        bonus_out[:, cs] = _mm_x2(r * ksum * rk_ref[:, cs], ones_bd) * v
        g_out[:, cs] = jnp.dot(win_g, gup_ref[:, cs], preferred_element_type=f32)


def _rwkv_feat(p, prm):
    bsz, l, _ = p.shape
    tl = min(128, l)
    nt = l // tl
    nh = l // HALO
    tpb = tl // HALO
    full2 = lambda shape: pl.BlockSpec(shape, lambda b, i: (0,) * len(shape))
    tok = pl.BlockSpec((None, tl, D), lambda b, i: (b, i, 0))
    tok2 = pl.BlockSpec((2, None, tl, D), lambda b, i: (0, b, i, 0))
    sd = jax.ShapeDtypeStruct((bsz, l, D), f32)
    sd2 = jax.ShapeDtypeStruct((2, bsz, l, D), f32)
    return pl.pallas_call(
        functools.partial(_rwkv_feat_kernel, tl=tl, nt=nt),
        grid=(bsz, nt),
        in_specs=[pl.BlockSpec((None, tl, N_RWKV_PAD), lambda b, i: (b, i, 0)),
                  pl.BlockSpec((None, HALO, N_RWKV_PAD), lambda b, i: (b, jnp.maximum(i * tpb - 1, 0), 0)),
                  pl.BlockSpec((None, HALO, N_RWKV_PAD),
                               lambda b, i: (b, jnp.minimum((i + 1) * tpb, nh - 1), 0)),
                  full2((1, N_RWKV_PAD)), full2((2, D)), full2((2, 256, D)), full2((2, D)),
                  full2((2, 256, D)), full2((256, D)), full2((1, D)), full2((1, D)), full2((1, D)),
                  full2((RW_GW, RW_GW))],
        out_specs=[tok, tok, tok, tok, tok, tok2, tok2, tok2],
        out_shape=[sd, sd, sd, sd, sd, sd2, sd2, sd2],
        scratch_shapes=[pltpu.VMEM((tl + 2 * HALO, N_RWKV_PAD), f32)],
        compiler_params=_cparams(2),
        name="rwkv_feat",
    )(p, p, p, prm["mu"], prm["w0"], prm["wup"], prm["a0"], prm["aup"], prm["gup"],
      prm["k_k"], prm["k_a"], prm["r_k"], prm["ones_bd"])


def _bd_expand(y, lane_head):
    yb = y.astype(f32)
    return jnp.concatenate([jnp.where(lane_head == h, yb, 0.0).astype(bf16) for h in range(4)], axis=0)


def _rwkv_scan_kernel(r_ref, v_ref, kk_ref, lw_ref, key_ref, b_ref, s0_ref, y_ref, sfin_ref,
                      s_ref, *, nc):
    c = RW_CHUNK
    d = pl.program_id(0)
    i = pl.program_id(2)
    fwd = d == 0

    @pl.when(i == 0)
    def _():
        s_ref[...] = s0_ref[...]

    row = _iota((c, c), 0)
    col = _iota((c, c), 1)
    sgn = jnp.where(fwd, 1, -1)
    tri = jnp.where((col - row) * sgn <= 0, 1.0, 0.0).astype(bf16)
    t4 = _iota((c, 4 * c), 0)
    j4 = _iota((c, 4 * c), 1) % c
    mask_s = (j4 - t4) * sgn < 0
    mask_i = (j4 - t4) * sgn <= 0
    eye4 = jnp.where(j4 == t4, 1.0, 0.0)
    lane_head = _iota((c, RW_GW), 1) // HEAD
    bd_mask = (_iota((RW_GW, RW_GW), 0) // HEAD) == (_iota((RW_GW, RW_GW), 1) // HEAD)

    def mmbd(x, y):
        return jnp.dot(x.astype(bf16), _bd_expand(y, lane_head), preferred_element_type=f32)

    groups = range(RW_NG)
    sl = [slice(g * RW_GW, (g + 1) * RW_GW) for g in groups]

    def state_free(rows):
        lw = [lw_ref[rows, sl[g]] for g in groups]
        cl = [_mm_e3(tri, lw[g]) for g in groups]
        tot = [jnp.sum(lw[g], axis=0, keepdims=True) for g in groups]
        v = [v_ref[rows, sl[g]] for g in groups]
        ar, a_b, a_k, bk = [], [], [], []
        for g in groups:
            g_inv = jnp.exp(-cl[g])
            g_end = jnp.exp(tot[g] - cl[g])
            bv = b_ref[rows, sl[g]]
            key = key_ref[rows, sl[g]]
            at = -kk_ref[rows, sl[g]] * jnp.exp(cl[g] - lw[g])
            rt = r_ref[rows, sl[g]] * jnp.exp(cl[g])
            ar.append(jnp.concatenate([at, rt], axis=0).astype(bf16))
            a_b.append(lax.dot_general(ar[g], _bd_expand(bv * g_inv, lane_head), (((1,), (1,)), ((), ())),
                                       preferred_element_type=f32))
            a_k.append(lax.dot_general(ar[g], _bd_expand(key * g_inv, lane_head), (((1,), (1,)), ((), ())),
                                       preferred_element_type=f32))
            bk.append(jnp.concatenate([bv * g_end, key * g_end], axis=0).astype(bf16))
        n_ab = [jnp.where(mask_s, a_b[g][:c], 0.0) for g in groups]
        a_rb = [jnp.where(mask_i, a_b[g][c:], 0.0) for g in groups]
        a_kk = [jnp.concatenate([jnp.where(mask_s, a_k[g][:c], 0.0), jnp.where(mask_i, a_k[g][c:], 0.0)],
                                axis=0) for g in groups]
        x = [eye4 + n_ab[g] for g in groups]
        m = [mmbd(n_ab[g], n_ab[g]) for g in groups]
        lvl = 2
        while lvl < c:
            if lvl * 2 < c:
                xm = [mmbd(jnp.concatenate([x[g], m[g]], axis=0), m[g]) for g in groups]
                x = [x[g] + xm[g][:c] for g in groups]
                m = [xm[g][c:] for g in groups]
            else:
                x = [x[g] + mmbd(x[g], m[g]) for g in groups]
            lvl *= 2
        av = [mmbd(a_kk[g], v[g]) for g in groups]
        return dict(ar=ar, x=x, av=av, a_rb=a_rb, v=v, bk=bk, tot=tot)

    def state_step(rows, p, s):
        sa = [_mm_nt(p["ar"][g], s[g]) for g in groups]
        u = [mmbd(p["x"][g], sa[g][:c] + p["av"][g][:c]) for g in groups]
        y = [sa[g][c:] + p["av"][g][c:] + mmbd(p["a_rb"][g], u[g]) for g in groups]
        upd = [_mm(jnp.concatenate([u[g], p["v"][g]], axis=0).T, p["bk"][g]) for g in groups]
        for g in groups:
            y_ref[rows, sl[g]] = y[g]
        return [s[g] * jnp.exp(p["tot"][g]) + jnp.where(bd_mask, upd[g], 0.0) for g in groups]

    first = pl.multiple_of(jnp.where(fwd, 0, c), c)
    chunk_rows = [pl.ds(first, c), pl.ds(pl.multiple_of(c - first, c), c)]
    parts = [state_free(rows) for rows in chunk_rows]
    s = [s_ref[g] for g in groups]
    for rows, p in zip(chunk_rows, parts):
        s = state_step(rows, p, s)
    for g in groups:
        s_ref[g] = s[g]

    @pl.when(i == nc - 1)
    def _():
        sfin_ref[...] = s_ref[...]


def _rwkv_scan(r, v, kk, lw, key, bvec, s0):
    bsz, l, _ = r.shape
    c = 2 * RW_CHUNK
    nc = l // c

    def cidx(d, i):
        return jnp.where(d == 0, i, nc - 1 - i)

    tok = pl.BlockSpec((None, c, D), lambda d, b, i: (b, cidx(d, i), 0))
    tok2 = pl.BlockSpec((None, None, c, D), lambda d, b, i: (d, b, cidx(d, i), 0))
    st = pl.BlockSpec((None, None, RW_NG, RW_GW, RW_GW), lambda d, b, i: (d, b, 0, 0, 0))
    return pl.pallas_call(
        functools.partial(_rwkv_scan_kernel, nc=nc),
        grid=(2, bsz, nc),
        in_specs=[tok, tok, tok, tok2, tok2, tok2, st],
        out_specs=[tok2, st],
        out_shape=[jax.ShapeDtypeStruct((2, bsz, l, D), f32),
                   jax.ShapeDtypeStruct((2, bsz, RW_NG, RW_GW, RW_GW), f32)],
        scratch_shapes=[pltpu.VMEM((RW_NG, RW_GW, RW_GW), f32)],
        compiler_params=_cparams(3),
        name="rwkv_scan",
    )(r, v, kk, lw, key, bvec, s0)


def _rwkv_out_kernel(y_ref, bonus_ref, g_ref, lnw_ref, lnb_ref, ones_ref, o_ref):
    ones_bd = ones_ref[...]
    for g in range(RW_NG):
        cs = slice(g * RW_GW, (g + 1) * RW_GW)
        y = y_ref[0, :, cs] + y_ref[1, :, cs]
        mean = _mm_x2(y, ones_bd) * (1.0 / HEAD)
        yc = y - mean
        var = _mm_x2(yc * yc, ones_bd) * (1.0 / HEAD)
        yn = yc * lax.rsqrt(var + GN_EPS) * lnw_ref[:, cs] + lnb_ref[:, cs]
        o_ref[:, cs] = ((yn + bonus_ref[:, cs]) * g_ref[:, cs]).astype(o_ref.dtype)


def _rwkv_out(y2, bonus, g, ln_w, ln_b, ones_bd):
    _, bsz, l, _ = y2.shape
    tl = min(256, l)
    tok = pl.BlockSpec((None, tl, D), lambda b, i: (b, i, 0))
    vec = pl.BlockSpec((1, D), lambda b, i: (0, 0))
    return pl.pallas_call(
        _rwkv_out_kernel,
        grid=(bsz, l // tl),
        in_specs=[pl.BlockSpec((2, None, tl, D), lambda b, i: (0, b, i, 0)), tok, tok, vec, vec,
                  pl.BlockSpec((RW_GW, RW_GW), lambda b, i: (0, 0))],
        out_specs=tok,
        out_shape=jax.ShapeDtypeStruct((bsz, l, D), bf16),
        compiler_params=_cparams(2),
        name="rwkv_out",
    )(y2, bonus, g, ln_w, ln_b, ones_bd)


def _ssd_prep_kernel(p_ref, prev_ref, next_ref, cw_ref, cb_ref, o_ref, pe_ref, *, tl, nt):
    i = pl.program_id(1)
    w = p_ref.shape[1]
    zero = jnp.zeros((HALO, w), f32)
    pe_ref[0:HALO, :] = jnp.where(i > 0, prev_ref[...], zero)
    pe_ref[HALO:HALO + tl, :] = p_ref[...]
    pe_ref[HALO + tl:2 * HALO + tl, :] = jnp.where(i < nt - 1, next_ref[...], zero)
    for n in range(w // 128):
        cs = slice(n * 128, (n + 1) * 128)
        xc = cb_ref[:, cs]
        for tap in range(CONV_W):
            r0 = HALO - CONV_LEFT + tap
            xc = xc + pe_ref[r0:r0 + tl, cs] * cw_ref[tap:tap + 1, cs]
        o_ref[:, cs] = _silu(xc)


def _ssd_prep(p, cw, cb):
    bsz, l, _ = p.shape
    tl = min(256, l)
    nt = l // tl
    nh = l // HALO
    tpb = tl // HALO
    wc = 1024
    c0 = D // wc
    return pl.pallas_call(
        functools.partial(_ssd_prep_kernel, tl=tl, nt=nt),
        grid=(bsz, nt, D_XBC // wc),
        in_specs=[pl.BlockSpec((None, tl, wc), lambda b, i, c: (b, i, c + c0)),
                  pl.BlockSpec((None, HALO, wc), lambda b, i, c: (b, jnp.maximum(i * tpb - 1, 0), c + c0)),
                  pl.BlockSpec((None, HALO, wc),
                               lambda b, i, c: (b, jnp.minimum((i + 1) * tpb, nh - 1), c + c0)),
                  pl.BlockSpec((CONV_W, wc), lambda b, i, c: (0, c)),
                  pl.BlockSpec((1, wc), lambda b, i, c: (0, c))],
        out_specs=pl.BlockSpec((None, tl, wc), lambda b, i, c: (b, i, c)),
        out_shape=jax.ShapeDtypeStruct((bsz, l, D_XBC), f32),
        scratch_shapes=[pltpu.VMEM((tl + 2 * HALO, wc), f32)],
        compiler_params=_cparams(3),
        name="ssd_prep",
    )(p, p, p, cw, cb)


def _ssd_kernel(xbc_ref, dtw_ref, dtb_ref, a_ref, dsk_ref, e_ref,
                h0_ref, y_ref, hfin_ref, cumt_ref, dtt_ref, h_ref, *, nc):
    q = SSM_Q
    d = pl.program_id(0)
    i = pl.program_id(2)
    fwd = d == 0

    @pl.when(i == 0)
    def _():
        h_ref[...] = h0_ref[...]

    row = _iota((q, q), 0)
    col = _iota((q, q), 1)
    sgn = jnp.where(fwd, 1, -1)
    low = (col - row) * sgn <= 0
    tri = jnp.where(low, 1.0, 0.0).astype(bf16)
    tri_t = jnp.where((row - col) * sgn <= 0, 1.0, 0.0).astype(bf16)

    dt = _softplus(dtw_ref[...] + dtb_ref[...])
    dta = dt * a_ref[...]
    cum = _mm_e3(tri, dta)
    tot = jnp.sum(dta, axis=0, keepdims=True)
    cumt_ref[...] = _mm_x3(dta.T, tri_t)
    dtt_ref[...] = dt.T
    e_d = e_ref[...]
    dec_e = _mm_x2(jnp.exp(cum), e_d)
    toend_e = _mm_x2(jnp.exp(tot - cum) * dt, e_d)
    tot_e = _mm_x2(jnp.broadcast_to(jnp.exp(tot), (8, 128)), e_d)[0:1]
    cum_d = pltpu.roll(cum, jnp.where(fwd, 0, 128 - SSM_HEADS), 1)
    lane = _iota((q, 128), 1)

    for g in range(SSM_G):
        bg = xbc_ref[:, D + g * SSM_N:D + (g + 1) * SSM_N]
        cg = xbc_ref[:, D + SSM_G * SSM_N + g * SSM_N:D + SSM_G * SSM_N + (g + 1) * SSM_N]
        cb = _mm_nt(cg, bg)
        gs = slice(g * 512, (g + 1) * 512)
        hg = h_ref[g]
        y_off = _mm(cg, hg) * dec_e[:, gs]
        for pr in range(4):
            ls = []
            for hh in range(2):
                h = g * 8 + pr * 2 + hh
                rowv = cumt_ref[pl.ds(d * 32 + h, 1), :]
                dtr = dtt_ref[pl.ds(d * 32 + h, 1), :]
                seg = cum_d[:, h:h + 1] - rowv
                ls.append(jnp.where(low, jnp.exp(jnp.where(low, seg, 0.0)), 0.0) * cb * dtr)
            lp = jnp.concatenate(ls, axis=1)
            ps = slice(g * 512 + pr * 128, g * 512 + (pr + 1) * 128)
            xp = xbc_ref[:, ps]
            bd2 = jnp.concatenate([jnp.where(lane < SSM_P, xp, 0.0), jnp.where(lane >= SSM_P, xp, 0.0)],
                                  axis=0)
            yd = _mm(lp, bd2)
            y_ref[:, ps] = yd + y_off[:, pr * 128:(pr + 1) * 128] + dsk_ref[:, ps] * xp
        xs = xbc_ref[:, gs] * toend_e[:, gs]
        states = _mm(bg.T, xs)
        h_ref[g] = hg * tot_e[:, gs] + states

    @pl.when(i == nc - 1)
    def _():
        hfin_ref[...] = h_ref[...]


def _ssd(p, prm, h0):
    bsz, l, _ = p.shape
    q = SSM_Q
    nc = l // q
    xbc = _ssd_prep(p, prm["cw"], prm["cb"])

    def cidx(d, i):
        return jnp.where(d == 0, i, nc - 1 - i)

    full = lambda shape: pl.BlockSpec(shape, lambda d, b, i: (0,) * len(shape))
    st = pl.BlockSpec((None, None, SSM_G, SSM_N, 8 * SSM_P), lambda d, b, i: (d, b, 0, 0, 0))
    return pl.pallas_call(
        functools.partial(_ssd_kernel, nc=nc),
        grid=(2, bsz, nc),
        in_specs=[pl.BlockSpec((None, q, D_XBC), lambda d, b, i: (b, cidx(d, i), 0)),
                  pl.BlockSpec((None, q, 128), lambda d, b, i: (b, cidx(d, i), SS_OFF_DT // 128)),
                  full((1, 128)), full((1, 128)),
                  pl.BlockSpec((None, 1, D), lambda d, b, i: (d, 0, 0)),
                  pl.BlockSpec((None, 128, D), lambda d, b, i: (d, 0, 0)),
                  st],
        out_specs=[pl.BlockSpec((None, None, q, D), lambda d, b, i: (d, b, cidx(d, i), 0)), st],
        out_shape=[jax.ShapeDtypeStruct((2, bsz, l, D), f32),
                   jax.ShapeDtypeStruct((2, bsz, SSM_G, SSM_N, 8 * SSM_P), f32)],
        scratch_shapes=[pltpu.VMEM((128, q), f32), pltpu.VMEM((128, q), f32),
                        pltpu.VMEM((SSM_G, SSM_N, 8 * SSM_P), f32)],
        compiler_params=_cparams(3),
        name="ssd",
    )(xbc, p, prm["dtb"], prm["a"], prm["dsk"], prm["e"], h0)


def _ssd_out_kernel(y_ref, z_ref, nw_ref, o_ref):
    gw = D // SSM_G
    for g in range(SSM_G):
        cs = slice(g * gw, (g + 1) * gw)
        yg = (y_ref[0, :, cs] + y_ref[1, :, cs]) * _silu(z_ref[:, cs])
        ms = jnp.mean(yg * yg, axis=-1, keepdims=True)
        o_ref[:, cs] = (yg * lax.rsqrt(ms + EPS) * nw_ref[:, cs]).astype(o_ref.dtype)


def _ssd_out(y2, p, norm_w):
    _, bsz, l, _ = y2.shape
    tl = min(256, l)
    tok = pl.BlockSpec((None, tl, D), lambda b, i: (b, i, 0))
    return pl.pallas_call(
        _ssd_out_kernel,
        grid=(bsz, l // tl),
        in_specs=[pl.BlockSpec((2, None, tl, D), lambda b, i: (0, b, i, 0)), tok,
                  pl.BlockSpec((1, D), lambda b, i: (0, 0))],
        out_specs=tok,
        out_shape=jax.ShapeDtypeStruct((bsz, l, D), bf16),
        compiler_params=_cparams(2),
        name="ssd_out",
    )(y2, p, norm_w)


def _merge_kernel(ya_ref, yb_ref, yc_ref, sa_ref, sb_ref, sc_ref, wa_ref, wb_ref, wc_ref, o_ref):
    acc = None
    for y_ref, s_ref, w_ref in ((ya_ref, sa_ref, wa_ref), (yb_ref, sb_ref, wb_ref), (yc_ref, sc_ref, wc_ref)):
        t = s_ref[...].astype(f32) * jnp.dot(y_ref[...], w_ref[...], preferred_element_type=f32)
        acc = t if acc is None else acc + t
    o_ref[...] = acc.astype(o_ref.dtype)


def _merge(ys, sig, ws):
    m = ys[0].shape[0]
    tm = min(1024, m)
    tn = 512
    nj = D // tn
    ysp = pl.BlockSpec((tm, D), lambda i, j: (i, 0))
    wsp = pl.BlockSpec((D, tn), lambda i, j: (0, j))
    ssp = [pl.BlockSpec((tm, tn), functools.partial(lambda i, j, k: (i, k * nj + j), k=k)) for k in range(3)]
    return pl.pallas_call(
        _merge_kernel,
        grid=(m // tm, nj),
        in_specs=[ysp, ysp, ysp] + ssp + [wsp, wsp, wsp],
        out_specs=pl.BlockSpec((tm, tn), lambda i, j: (i, j)),
        out_shape=jax.ShapeDtypeStruct((m, D), bf16),
        compiler_params=_cparams(2),
        name="merge",
    )(ys[0], ys[1], ys[2], sig, sig, sig, ws[0], ws[1], ws[2])


def _wo_kernel(m_ref, w_ref, h_ref, g_ref, o_ref):
    o_ref[...] = h_ref[...] + g_ref[...] * jnp.dot(m_ref[...], w_ref[...], preferred_element_type=f32)


def _wo_residual(m, w_o, h, mod4, row_of_batch, k_gate, *, transposed):
    bsz, l, _ = h.shape
    if transposed:
        rows = l // GRID_W
        hv = h.reshape(bsz, rows, GRID_W * D)
        tl, nt = rows, GRID_W
        hspec = pl.BlockSpec((None, tl, D), lambda b, i: (b, 0, i))
        oshape = jax.ShapeDtypeStruct((bsz, rows, GRID_W * D), f32)
    else:
        tl = min(512, l)
        nt = l // tl
        hv = h
        hspec = pl.BlockSpec((None, tl, D), lambda b, i: (b, i, 0))
        oshape = jax.ShapeDtypeStruct((bsz, l, D), f32)
    out = pl.pallas_call(
        _wo_kernel,
        grid=(bsz, nt),
        in_specs=[pl.BlockSpec((None, tl, D), lambda b, i: (b, i, 0)),
                  pl.BlockSpec((D, D), lambda b, i: (0, 0)),
                  hspec,
                  pl.BlockSpec((None, None, 1, D), lambda b, i: (row_of_batch(b), k_gate, 0, 0))],
        out_specs=hspec,
        out_shape=oshape,
        compiler_params=_cparams(2),
        name="wo_residual",
    )(m, w_o, hv, mod4)
    return out.reshape(bsz, l, D)


def _router_kernel(x_ref, g_ref, sc_ref, sh_ref, r_ref, u_ref, gates_ref):
    x = x_ref[...]
    ms = jnp.mean(x * x, axis=-1, keepdims=True)
    u = (x * lax.rsqrt(ms + EPS) * g_ref[...]) * (1.0 + sc_ref[...]) + sh_ref[...]
    u_ref[...] = u.astype(u_ref.dtype)
    u0, u1, u2 = _split3(u)
    r0, r1, r2 = _split3(r_ref[...])
    dot = lambda a, b: jnp.dot(a, b, preferred_element_type=f32)
    logits = (dot(u0, r0) + (dot(u0, r1) + dot(u1, r0))
              + (dot(u1, r1) + dot(u0, r2) + dot(u2, r0)))
    lane = _iota(logits.shape, 1)
    neg = jnp.float32(-jnp.inf)
    lg = jnp.where(lane < N_EXPERTS, logits, neg)
    m1 = jnp.max(lg, axis=-1, keepdims=True)
    i1 = jnp.min(jnp.where(lg == m1, lane, 128), axis=-1, keepdims=True)
    lg2 = jnp.where(lane == i1, neg, lg)
    m2 = jnp.max(lg2, axis=-1, keepdims=True)
    i2 = jnp.min(jnp.where(lg2 == m2, lane, 128), axis=-1, keepdims=True)
    e2 = jnp.exp(m2 - m1)
    den = 1.0 + e2
    gates_ref[...] = jnp.where(lane == i1, 1.0 / den, 0.0) + jnp.where(lane == i2, e2 / den, 0.0)


def _norm_router(h, gain, mod4, row_of_batch, k_sc, k_sh, router_pad, out_dtype=bf16):
    bsz, l, _ = h.shape
    tl = min(512, l)
    tok = pl.BlockSpec((None, tl, D), lambda b, i: (b, i, 0))
    return pl.pallas_call(
        _router_kernel,
        grid=(bsz, l // tl),
        in_specs=[tok,
                  pl.BlockSpec((1, D), lambda b, i: (0, 0)),
                  pl.BlockSpec((None, None, 1, D), lambda b, i: (row_of_batch(b), k_sc, 0, 0)),
                  pl.BlockSpec((None, None, 1, D), lambda b, i: (row_of_batch(b), k_sh, 0, 0)),
                  pl.BlockSpec((D, 128), lambda b, i: (0, 0))],
        out_specs=[tok, pl.BlockSpec((None, tl, 128), lambda b, i: (b, i, 0))],
        out_shape=[jax.ShapeDtypeStruct((bsz, l, D), out_dtype), jax.ShapeDtypeStruct((bsz, l, 128), f32)],
        compiler_params=_cparams(2),
        name="norm_router",
    )(h, gain.reshape(1, D), mod4, mod4, router_pad)


def _up_kernel(*refs, gated):
    if gated:
        u_ref, w1_ref, w3_ref, gt_ref, o_ref, w1b_ref, w3b_ref = refs
    else:
        u_ref, w1_ref, w3_ref, o_ref, w1b_ref, w3b_ref = refs

    @pl.when(pl.program_id(2) == 0)
    def _():
        w1b_ref[...] = w1_ref[...].astype(bf16)
        w3b_ref[...] = w3_ref[...].astype(bf16)

    u = u_ref[...]
    a = jnp.dot(u, w1b_ref[...], preferred_element_type=f32)
    b = jnp.dot(u, w3b_ref[...], preferred_element_type=f32)
    act = _silu(a) * b
    if gated:
        e = pl.program_id(0)
        gt = gt_ref[...]
        lane = _iota(gt.shape, 1)
        act = act * jnp.sum(jnp.where(lane == e, gt, 0.0), axis=-1, keepdims=True)
    o_ref[...] = act.astype(o_ref.dtype)


def _swiglu_up(u, w1, w3, lead, gates=None):
    m = u.shape[0]
    _, ne, _, ff = w1.shape
    tm = min(1024, m)
    tn = 512
    nj = ff // tn
    gated = gates is not None
    wspec = pl.BlockSpec((None, None, D, tn), lambda e, j, i: (lead, e, 0, j))
    in_specs = [pl.BlockSpec((tm, D), lambda e, j, i: (i, 0)), wspec, wspec]
    args = [u, w1, w3]
    if gated:
        in_specs.append(pl.BlockSpec((tm, 128), lambda e, j, i: (i, 0)))
        args.append(gates)
    return pl.pallas_call(
        functools.partial(_up_kernel, gated=gated),
        grid=(ne, nj, m // tm),
        in_specs=in_specs,
        out_specs=pl.BlockSpec((tm, tn), lambda e, j, i: (i, e * nj + j)),
        out_shape=jax.ShapeDtypeStruct((m, ne * ff), bf16),
        scratch_shapes=[pltpu.VMEM((D, tn), bf16), pltpu.VMEM((D, tn), bf16)],
        compiler_params=_cparams(3),
        name="swiglu_up",
    )(*args)


def _down_kernel(a_ref, w_ref, h_ref, g_ref, o_ref, acc_ref, *, nk):
    k = pl.program_id(3)

    @pl.when(k == 0)
    def _():
        acc_ref[...] = jnp.zeros_like(acc_ref)

    acc_ref[...] += jnp.dot(a_ref[...], w_ref[...].astype(bf16), preferred_element_type=f32)

    @pl.when(k == nk - 1)
    def _():
        o_ref[...] = h_ref[...] + g_ref[...] * acc_ref[...]


def _down_residual(act, w2, h, mod4, row_of_batch, k_gate):
    bsz, l, kk = act.shape
    tl = min(1024, l)
    tn = 1024
    tk = 2048
    nk = kk // tk
    return pl.pallas_call(
        functools.partial(_down_kernel, nk=nk),
        grid=(bsz, l // tl, D // tn, nk),
        in_specs=[pl.BlockSpec((None, tl, tk), lambda b, i, j, k: (b, i, k)),
                  pl.BlockSpec((tk, tn), lambda b, i, j, k: (k, j)),
                  pl.BlockSpec((None, tl, tn), lambda b, i, j, k: (b, i, j)),
                  pl.BlockSpec((None, None, 1, tn), lambda b, i, j, k: (row_of_batch(b), k_gate, 0, j))],
        out_specs=pl.BlockSpec((None, tl, tn), lambda b, i, j, k: (b, i, j)),
        out_shape=jax.ShapeDtypeStruct((bsz, l, D), f32),
        scratch_shapes=[pltpu.VMEM((tl, tn), f32)],
        compiler_params=_cparams(4),
        name="down_residual",
    )(act, w2, h, mod4)


MOE_TM = 512
MOE_TT = 512
NACT_LANE = 127


def _moe_count_kernel(gates_ref, pre_ref, cnt_ref, carry_ref):
    i = pl.program_id(0)

    @pl.when(i == 0)
    def _():
        carry_ref[...] = jnp.zeros_like(carry_ref)

    tt = gates_ref.shape[0]
    a = jnp.where(gates_ref[...] > 0.0, 1.0, 0.0)
    strict = jnp.where(_iota((tt, tt), 1) < _iota((tt, tt), 0), 1.0, 0.0).astype(bf16)
    carry = carry_ref[0:1, :]
    pre_ref[...] = jnp.dot(strict, a.astype(bf16), preferred_element_type=f32) + carry
    carry_ref[...] = jnp.broadcast_to(carry + jnp.sum(a, axis=0, keepdims=True), carry_ref.shape)
    cnt_ref[...] = carry_ref[...]


def _moe_place_kernel(gates_ref, pre_ref, cnt_ref, pos_ref, gv_ref, tile_ref, *, n_tiles):
    tt = gates_ref.shape[0]
    tm = float(MOE_TM)
    lane1 = _iota((1, 128), 1)
    cnt = cnt_ref[0:1, :]
    gsz = jnp.floor((cnt + (tm - 1.0)) * (1.0 / tm)) * tm
    upper = jnp.where(_iota((128, 128), 0) < _iota((128, 128), 1), 1.0, 0.0).astype(bf16)
    base = _mm_x3(jnp.broadcast_to(gsz, (8, 128)), upper)[0:1]
    gates = gates_ref[...]
    act = gates > 0.0
    pos = base + pre_ref[...]
    big = jnp.float32(1e9)
    p_lo = jnp.min(jnp.where(act, pos, big), axis=-1, keepdims=True)
    p_hi = jnp.max(jnp.where(act, pos, -1.0), axis=-1, keepdims=True)
    g_lo = jnp.sum(jnp.where(act & (pos == p_lo), gates, 0.0), axis=-1, keepdims=True)
    g_hi = jnp.where(p_hi != p_lo,
                     jnp.sum(jnp.where(act & (pos == p_hi), gates, 0.0), axis=-1, keepdims=True), 0.0)
    lane = _iota((tt, 128), 1)
    posf = jnp.where(lane == 0, p_lo, 0.0) + jnp.where(lane == 1, p_hi, 0.0)
    pos_ref[...] = posf.T[0:8].astype(jnp.int32)
    gv_ref[...] = jnp.where(lane == 0, g_lo, 0.0) + jnp.where(lane == 1, g_hi, 0.0)
    end = base + gsz
    start_j = lane1.astype(f32) * tm
    texp = jnp.zeros((1, 128), f32)
    for e in range(N_EXPERTS - 1):
        end_e = jnp.sum(jnp.where(lane1 == e, end, 0.0), axis=-1, keepdims=True)
        texp = texp + jnp.where(start_j >= end_e, 1.0, 0.0)
    total = jnp.sum(jnp.where(lane1 == N_EXPERTS - 1, end, 0.0), axis=-1, keepdims=True)
    table = jnp.where(lane1 == NACT_LANE, total * (1.0 / tm), texp)
    tile_ref[...] = jnp.broadcast_to(table, (8, 128)).astype(jnp.int32)


def _moe_scatter_kernel(pos_ref, v_ref, xs_in_ref, xs_ref, sem, *, dump_base):
    del xs_in_ref
    tt = v_ref.shape[0]

    def row_copy(r, dst_row):
        return pltpu.make_async_copy(v_ref.at[pl.ds(r, 1)], xs_ref.at[pl.ds(dst_row, 1)], sem)

    def body(r, carry):
        p0 = pos_ref[0, r]
        p1 = pos_ref[1, r]
        p1 = jnp.where(p1 == p0, dump_base + r, p1)
        row_copy(r, p0).start()
        row_copy(r, p1).start()
        return carry

    lax.fori_loop(0, tt, body, 0, unroll=8)
    for _ in range(2):
        pltpu.make_async_copy(v_ref, xs_ref.at[pl.ds(0, tt)], sem).wait()


def _new_expert_tile(tile_ref, t):
    return (t == 0) | (tile_ref[t] != tile_ref[jnp.maximum(t - 1, 0)])


def _moe_gup_kernel(tile_ref, x_ref, w1_ref, w3_ref, o_ref, w1b_ref, w3b_ref):
    t = pl.program_id(1)
    active = t < tile_ref[NACT_LANE]

    @pl.when(active & _new_expert_tile(tile_ref, t))
    def _():
        w1b_ref[...] = w1_ref[...].astype(bf16)
        w3b_ref[...] = w3_ref[...].astype(bf16)

    @pl.when(active)
    def _():
        x = x_ref[...].astype(bf16)
        a = jnp.dot(x, w1b_ref[...], preferred_element_type=f32)
        b = jnp.dot(x, w3b_ref[...], preferred_element_type=f32)
        o_ref[...] = (_silu(a) * b).astype(o_ref.dtype)

    @pl.when(jnp.logical_not(active))
    def _():
        o_ref[...] = jnp.zeros_like(o_ref)


def _moe_gdown_kernel(tile_ref, a_ref, w_ref, o_ref, wb_ref):
    t = pl.program_id(1)
    active = t < tile_ref[NACT_LANE]

    @pl.when(active & _new_expert_tile(tile_ref, t))
    def _():
        wb_ref[...] = w_ref[...].astype(bf16)

    @pl.when(active)
    def _():
        o_ref[...] = jnp.dot(a_ref[...], wb_ref[...], preferred_element_type=f32)

    @pl.when(jnp.logical_not(active))
    def _():
        o_ref[...] = jnp.zeros_like(o_ref)


def _moe_combine_kernel(pos_ref, gv_ref, h_ref, gf_ref, ys_ref, o_ref, buf0, buf1, sem):
    tt = h_ref.shape[0]

    def row_copy(src_row, buf, r):
        return pltpu.make_async_copy(ys_ref.at[pl.ds(src_row, 1)], buf.at[pl.ds(r, 1)], sem)

    def body(r, carry):
        row_copy(pos_ref[0, r], buf0, r).start()
        row_copy(pos_ref[1, r], buf1, r).start()
        return carry

    lax.fori_loop(0, tt, body, 0, unroll=8)
    for buf in (buf0, buf1):
        pltpu.make_async_copy(ys_ref.at[pl.ds(0, tt)], buf, sem).wait()
    gv = gv_ref[...]
    lane = _iota(gv.shape, 1)
    g0 = jnp.sum(jnp.where(lane == 0, gv, 0.0), axis=-1, keepdims=True)
    g1 = jnp.sum(jnp.where(lane == 1, gv, 0.0), axis=-1, keepdims=True)
    o_ref[...] = h_ref[...] + gf_ref[...] * (g0 * buf0[...] + g1 * buf1[...])


def _moe_sparse(v32, gates, h, w1, w3, w2, lead, mod4, row_of_batch, k_gate):
    bsz, l, _ = h.shape
    n_tok = bsz * l
    tt = min(MOE_TT, n_tok)
    n_tt = n_tok // tt
    n_tiles = (2 * n_tok) // MOE_TM + N_EXPERTS
    rows = n_tiles * MOE_TM
    ff = w1.shape[3]
    vf = v32.reshape(n_tok, D)
    gf = gates.reshape(n_tok, 128)

    pre, cnt = pl.pallas_call(
        _moe_count_kernel,
        grid=(n_tt,),
        in_specs=[pl.BlockSpec((tt, 128), lambda i: (i, 0))],
        out_specs=[pl.BlockSpec((tt, 128), lambda i: (i, 0)), pl.BlockSpec((8, 128), lambda i: (0, 0))],
        out_shape=[jax.ShapeDtypeStruct((n_tok, 128), f32), jax.ShapeDtypeStruct((8, 128), f32)],
        scratch_shapes=[pltpu.VMEM((8, 128), f32)],
        compiler_params=_cparams(1),
        name="moe_count",
    )(gf)

    pos, gv, table = pl.pallas_call(
        functools.partial(_moe_place_kernel, n_tiles=n_tiles),
        grid=(n_tt,),
        in_specs=[pl.BlockSpec((tt, 128), lambda i: (i, 0)), pl.BlockSpec((tt, 128), lambda i: (i, 0)),
                  pl.BlockSpec((8, 128), lambda i: (0, 0))],
        out_specs=[pl.BlockSpec((8, tt), lambda i: (0, i)), pl.BlockSpec((tt, 128), lambda i: (i, 0)),
                   pl.BlockSpec((8, 128), lambda i: (0, 0))],
        out_shape=[jax.ShapeDtypeStruct((8, n_tok), jnp.int32), jax.ShapeDtypeStruct((n_tok, 128), f32),
                   jax.ShapeDtypeStruct((8, 128), jnp.int32)],
        compiler_params=_cparams(1),
        name="moe_place",
    )(gf, pre, cnt)
    tile_tab = table[0]

    smem_pos = pl.BlockSpec((8, tt), lambda i: (0, i), memory_space=pltpu.SMEM)
    xs = pl.pallas_call(
        functools.partial(_moe_scatter_kernel, dump_base=rows),
        grid=(n_tt,),
        in_specs=[smem_pos, pl.BlockSpec((tt, D), lambda i: (i, 0)), pl.BlockSpec(memory_space=pl.ANY)],
        out_specs=pl.BlockSpec(memory_space=pl.ANY),
        out_shape=jax.ShapeDtypeStruct((rows + tt, D), f32),
        scratch_shapes=[pltpu.SemaphoreType.DMA],
        input_output_aliases={2: 0},
        compiler_params=_cparams(1),
        name="moe_scatter",
    )(pos, vf, jnp.zeros((rows + tt, D), f32))

    assert n_tiles < NACT_LANE
    tn = 512
    nj = ff // tn
    last_tile = lambda tab: jnp.maximum(tab[NACT_LANE] - 1, 0)
    act = pl.pallas_call(
        _moe_gup_kernel,
        grid_spec=pltpu.PrefetchScalarGridSpec(
            num_scalar_prefetch=1,
            grid=(nj, n_tiles),
            in_specs=[pl.BlockSpec((MOE_TM, D), lambda j, t, tab: (jnp.minimum(t, last_tile(tab)), 0)),
                      pl.BlockSpec((None, None, D, tn),
                                   lambda j, t, tab: (lead, tab[jnp.minimum(t, last_tile(tab))], 0, j)),
                      pl.BlockSpec((None, None, D, tn),
                                   lambda j, t, tab: (lead, tab[jnp.minimum(t, last_tile(tab))], 0, j))],
            out_specs=pl.BlockSpec((MOE_TM, tn), lambda j, t, tab: (t, j)),
            scratch_shapes=[pltpu.VMEM((D, tn), bf16), pltpu.VMEM((D, tn), bf16)]),
        out_shape=jax.ShapeDtypeStruct((rows, ff), bf16),
        compiler_params=_cparams(2),
        name="moe_up",
    )(tile_tab, xs, w1, w3)

    tnd = 1024
    ys = pl.pallas_call(
        _moe_gdown_kernel,
        grid_spec=pltpu.PrefetchScalarGridSpec(
            num_scalar_prefetch=1,
            grid=(D // tnd, n_tiles),
            in_specs=[pl.BlockSpec((MOE_TM, ff), lambda j, t, tab: (jnp.minimum(t, last_tile(tab)), 0)),
                      pl.BlockSpec((None, None, ff, tnd),
                                   lambda j, t, tab: (lead, tab[jnp.minimum(t, last_tile(tab))], 0, j))],
            out_specs=pl.BlockSpec((MOE_TM, tnd), lambda j, t, tab: (t, j)),
            scratch_shapes=[pltpu.VMEM((ff, tnd), bf16)]),
        out_shape=jax.ShapeDtypeStruct((rows, D), f32),
        compiler_params=_cparams(2),
        name="moe_down",
    )(tile_tab, act, w2)

    tpb = l // tt if l >= tt else 1
    out = pl.pallas_call(
        _moe_combine_kernel,
        grid=(n_tt,),
        in_specs=[smem_pos, pl.BlockSpec((tt, 128), lambda i: (i, 0)), pl.BlockSpec((tt, D), lambda i: (i, 0)),
                  pl.BlockSpec((None, None, 1, D), lambda i: (row_of_batch(i // tpb), k_gate, 0, 0)),
                  pl.BlockSpec(memory_space=pl.ANY)],
        out_specs=pl.BlockSpec((tt, D), lambda i: (i, 0)),
        out_shape=jax.ShapeDtypeStruct((n_tok, D), f32),
        scratch_shapes=[pltpu.VMEM((tt, D), f32), pltpu.VMEM((tt, D), f32), pltpu.SemaphoreType.DMA],
        compiler_params=_cparams(1),
        name="moe_combine",
    )(pos, gv, h.reshape(n_tok, D), mod4, ys)
    return out.reshape(bsz, l, D)


def _block_ones(n, bs):
    idx = jnp.arange(n) // bs
    return (idx[:, None] == idx[None, :]).astype(bf16)


def _pad_cols(w, n):
    return jnp.pad(w, ((0, 0), (0, n - w.shape[1])))


def _rwkv_params(li, mu, w0, w_up, a0, a_up, g_up, k_k, k_a, r_k, ln_w, ln_b):
    wup = jnp.zeros((2, 256, D), f32)
    aup = jnp.zeros((2, 256, D), f32)
    for d in range(2):
        wup = wup.at[d, d * W_LORA:(d + 1) * W_LORA].set(w_up[li, d])
        o = (RW_OFF_G - 2 * A_LORA) - RW_OFF_A + d * A_LORA
        aup = aup.at[d, o:o + A_LORA].set(a_up[li, d])
    return {
        "mu": jnp.pad(mu[li], (0, N_RWKV_PAD - N_RWKV)).reshape(1, N_RWKV_PAD),
        "w0": w0[li], "wup": wup.astype(bf16), "a0": a0[li], "aup": aup.astype(bf16),
        "gup": g_up[li].astype(bf16), "k_k": k_k[li].reshape(1, D), "k_a": k_a[li].reshape(1, D),
        "r_k": r_k[li].reshape(1, D), "ln_w": ln_w[li].reshape(1, D), "ln_b": ln_b[li].reshape(1, D),
        "ones_bd": _block_ones(RW_GW, HEAD),
    }


def _ssd_params(li, conv_w, conv_b, a_log, dt_bias, d_skip):
    a = -jnp.exp(a_log[li].astype(f32))
    hid = jnp.arange(D) // SSM_P
    sel = jnp.arange(128)[:, None]
    e = jnp.stack([(sel == hid[None, :] + d * SSM_HEADS) for d in range(2)]).astype(bf16)
    return {
        "cw": conv_w[li], "cb": conv_b[li].reshape(1, D_XBC),
        "dtb": jnp.pad(dt_bias[li].reshape(1, 2 * SSM_HEADS), ((0, 0), (0, 64))),
        "a": jnp.pad(a.reshape(1, 2 * SSM_HEADS), ((0, 0), (0, 64))),
        "dsk": jnp.stack([jnp.repeat(d_skip[li], SSM_P), jnp.zeros((D,), f32)]).reshape(2, 1, D),
        "e": e,
    }


def kernel(x, c, ctx, c_ctx, ada_w, ada_b, norm_mix, norm_ffn, norm_final, w_in, lru_conv_w, lru_conv_b, lru_gate_w, lru_gate_b, lru_lambda, rwkv_mu, rwkv_w0, rwkv_w_up, rwkv_a0, rwkv_a_up, rwkv_g_up, rwkv_k_k, rwkv_k_a, rwkv_r_k, rwkv_ln_w, rwkv_ln_b, ssm_conv_w, ssm_conv_b, ssm_a_log, ssm_dt_bias, ssm_d, ssm_norm_w, w_out_lru, w_out_rwkv, w_out_ssm, w_o, ffn_w1, ffn_w3, ffn_w2, moe_router, moe_w1, moe_w3, moe_w2):
    bsz, l, _ = x.shape
    depth = ada_w.shape[0]
    off_lru = 3 * D
    off_rwkv = off_lru + 2 * D
    off_ssm = off_rwkv + N_RWKV
    cvec = jnp.zeros((8, D), f32).at[:bsz].set(c).at[bsz].set(c_ctx)
    lat_row = lambda b: b
    ctx_row = lambda b: bsz

    h_lat, h_ctx = x, ctx
    for li in range(depth):
        last = li == depth - 1
        odd = li % 2 == 1
        mod4 = _ada(cvec, ada_w, ada_b, li).reshape(8, 6, 1, D)

        lru_gw = [jnp.concatenate([lru_gate_w[li, d, 0], lru_gate_w[li, d, 1]], axis=-1).astype(bf16)
                  for d in range(2)]
        rp = _rwkv_params(li, rwkv_mu, rwkv_w0, rwkv_w_up, rwkv_a0, rwkv_a_up, rwkv_g_up, rwkv_k_k,
                          rwkv_k_a, rwkv_r_k, rwkv_ln_w, rwkv_ln_b)
        sp = _ssd_params(li, ssm_conv_w, ssm_conv_b, ssm_a_log, ssm_dt_bias, ssm_d)
        w_outs = [w_out_lru[li].astype(bf16), w_out_rwkv[li].astype(bf16), w_out_ssm[li].astype(bf16)]
        w_o_b = w_o[li].astype(bf16)

        def token_mix(u, states, need_out):
            lx = u.shape[1]
            um = u.reshape(bsz * lx, D)
            p_lru = _proj(um, w_in, li, tn=1024, col0=off_lru, n=2 * D).reshape(bsz, lx, 2 * D)
            p_rw = _proj(um, w_in, li, tn=1024, col0=off_rwkv, n=N_RWKV_PAD).reshape(bsz, lx, N_RWKV_PAD)
            p_ss = _proj(um, w_in, li, tn=896, col0=off_ssm, n=N_SSM_PAD).reshape(bsz, lx, N_SSM_PAD)
            lru_s, rw_s, ss_s = states
            cw, cb = lru_conv_w[li], lru_conv_b[li].reshape(1, D)
            hb, hl_b = _lru_pass(p_lru, cw, cb, lru_gw[1], lru_gate_b[li, 1], lru_lambda[li, 1].reshape(1, D),
                                 lru_s[1], None, reverse=True)
            ya, hl_f = _lru_pass(p_lru, cw, cb, lru_gw[0], lru_gate_b[li, 0], lru_lambda[li, 0].reshape(1, D),
                                 lru_s[0], hb, reverse=False)
            r, v, kk, g, bonus, lw, key, bvec = _rwkv_feat(p_rw, rp)
            y2, rw_fin = _rwkv_scan(r, v, kk, lw, key, bvec, rw_s)
            ys2, ss_fin = _ssd(p_ss, sp, ss_s)
            new_states = ((hl_f, hl_b), rw_fin, ss_fin)
            if not need_out:
                return None, new_states
            yb = _rwkv_out(y2, bonus, g, rp["ln_w"], rp["ln_b"], rp["ones_bd"])
            yc = _ssd_out(ys2, p_ss, ssm_norm_w[li].reshape(1, D))
            sig = _proj(um, w_in, li, tn=1024, col0=0, n=off_lru, act="sigmoid", out_dtype=bf16)
            m = _merge([ya.reshape(-1, D), yb.reshape(-1, D), yc.reshape(-1, D)], sig, w_outs)
            return m.reshape(bsz, lx, D), new_states

        zero_states = ((jnp.zeros((bsz, 1, D), f32), jnp.zeros((bsz, 1, D), f32)),
                       jnp.zeros((2, bsz, RW_NG, RW_GW, RW_GW), f32),
                       jnp.zeros((2, bsz, SSM_G, SSM_N, 8 * SSM_P), f32))

        u_ctx = _norm(h_ctx, norm_mix[li], mod4, ctx_row, 1, 0)
        u_lat = _norm(h_lat, norm_mix[li], mod4, lat_row, 1, 0, transposed=odd)
        m_ctx, ctx_states = token_mix(u_ctx, zero_states, not last)
        m_lat, _ = token_mix(u_lat, ctx_states, True)
        h_lat = _wo_residual(m_lat, w_o_b, h_lat, mod4, lat_row, 2, transposed=odd)
        if not last:
            h_ctx = _wo_residual(m_ctx, w_o_b, h_ctx, mod4, ctx_row, 2, transposed=False)

        j = li // 2
        streams = [(h_lat, lat_row)] + ([] if last else [(h_ctx, ctx_row)])
        outs = []
        for h, row_fn in streams:
            lx = h.shape[1]
            if not odd:
                v = _norm(h, norm_ffn[li], mod4, row_fn, 4, 3)
                act = _swiglu_up(v.reshape(bsz * lx, D), ffn_w1[:, None], ffn_w3[:, None], j)
                w2 = ffn_w2[j].astype(bf16)
            elif row_fn is lat_row:
                router_pad = _pad_cols(moe_router[j], 128)
                v32, gates = _norm_router(h, norm_ffn[li], mod4, row_fn, 4, 3, router_pad, out_dtype=f32)
                outs.append(_moe_sparse(v32, gates, h, moe_w1, moe_w3, moe_w2, j, mod4, row_fn, 5))
                continue
            else:
                router_pad = _pad_cols(moe_router[j], 128)
                v, gates = _norm_router(h, norm_ffn[li], mod4, row_fn, 4, 3, router_pad)
                act = _swiglu_up(v.reshape(bsz * lx, D), moe_w1, moe_w3, j, gates.reshape(bsz * lx, 128))
                w2 = moe_w2[j].astype(bf16).reshape(N_EXPERTS * D_FF_EXPERT, D)
            outs.append(_down_residual(act.reshape(bsz, lx, -1), w2, h, mod4, row_fn, 5))
        h_lat = outs[0]
        if not last:
            h_ctx = outs[1]
    return _final_norm(h_lat, norm_final)
```

```python
import functools
import math

import jax
import jax.numpy as jnp
from jax import lax
from jax.experimental import pallas as pl
from jax.experimental.pallas import tpu as pltpu

f32 = jnp.float32
bf16 = jnp.bfloat16

D = 2048
GRID_W = 64
EPS = 1e-6
CONV_W = 4
CONV_LEFT = 2
HALO = 8

LRU_BLOCKS = 16
LRU_BS = D // LRU_BLOCKS
LRU_C = 8.0

HEADS = 32
HEAD = 64
W_LORA = 96
A_LORA = 96
G_LORA = 256
GN_EPS = 64e-5
N_RWKV = 3 * D + 2 * W_LORA + 2 * A_LORA + G_LORA
N_RWKV_PAD = 7168
RW_OFF_W = 3 * D
RW_OFF_A = 3 * D + 128
RW_OFF_G = 3 * D + 2 * W_LORA + 2 * A_LORA
RW_CHUNK = 64
RW_GW = 256
RW_NG = D // RW_GW

SSM_HEADS = 32
SSM_P = 64
SSM_N = 128
SSM_G = 4
SSM_Q = 128
D_XBC = D + 2 * SSM_G * SSM_N
N_SSM = D + D_XBC + 2 * SSM_HEADS
N_SSM_PAD = 5376
SS_OFF_DT = D + D_XBC

D_FF = 3 * D
N_EXPERTS = 8
D_FF_EXPERT = D_FF // 2

VMEM_LIMIT = 56 * 1024 * 1024


def _cparams(n_axes, vmem=VMEM_LIMIT):
    return pltpu.CompilerParams(dimension_semantics=("arbitrary",) * n_axes, vmem_limit_bytes=vmem)


def _mm(a, b):
    return jnp.dot(a.astype(bf16), b.astype(bf16), preferred_element_type=f32)


def _mm_nt(a, b):
    return lax.dot_general(a.astype(bf16), b.astype(bf16), (((1,), (1,)), ((), ())),
                           preferred_element_type=f32)


def _split3(x):
    x0 = x.astype(bf16)
    r = x - x0.astype(f32)
    x1 = r.astype(bf16)
    r = r - x1.astype(f32)
    return x0, x1, r.astype(bf16)


def _mm_x3(x, e):
    x0, x1, x2 = _split3(x)
    return (jnp.dot(x0, e, preferred_element_type=f32) + jnp.dot(x1, e, preferred_element_type=f32)
            + jnp.dot(x2, e, preferred_element_type=f32))


def _mm_x2(x, e):
    x0 = x.astype(bf16)
    x1 = (x - x0.astype(f32)).astype(bf16)
    return jnp.dot(x0, e, preferred_element_type=f32) + jnp.dot(x1, e, preferred_element_type=f32)


def _mm_e3(e, x):
    x0, x1, x2 = _split3(x)
    return (jnp.dot(e, x0, preferred_element_type=f32) + jnp.dot(e, x1, preferred_element_type=f32)
            + jnp.dot(e, x2, preferred_element_type=f32))


def _softplus(x):
    return jnp.maximum(x, 0.0) + jnp.log1p(jnp.exp(-jnp.abs(x)))


def _sigmoid(x):
    return 0.5 * jnp.tanh(0.5 * x) + 0.5


def _silu(x):
    return x * _sigmoid(x)


def _iota(shape, dim):
    return lax.broadcasted_iota(jnp.int32, shape, dim)


def _ada_kernel(c_ref, w_ref, b_ref, o_ref):
    cv = c_ref[...]
    o_ref[...] = _mm(_silu(cv), w_ref[...]) + b_ref[...]


def _ada(cvec, w, b, li):
    depth, _, n = w.shape
    tn = 1536
    return pl.pallas_call(
        _ada_kernel,
        grid=(n // tn,),
        in_specs=[pl.BlockSpec((8, D), lambda j: (0, 0)),
                  pl.BlockSpec((None, D, tn), lambda j: (li, 0, j)),
                  pl.BlockSpec((None, 1, tn), lambda j: (li, 0, j))],
        out_specs=pl.BlockSpec((8, tn), lambda j: (0, j)),
        out_shape=jax.ShapeDtypeStruct((8, n), f32),
        compiler_params=_cparams(1),
        name="ada",
    )(cvec, w, b.reshape(depth, 1, n))


def _norm_kernel(x_ref, g_ref, sc_ref, sh_ref, o_ref):
    x = x_ref[...]
    ms = jnp.mean(x * x, axis=-1, keepdims=True)
    xn = x * lax.rsqrt(ms + EPS) * g_ref[...]
    o_ref[...] = (xn * (1.0 + sc_ref[...]) + sh_ref[...]).astype(o_ref.dtype)


def _norm(h, gain, mod4, row_of_batch, k_sc, k_sh, *, transposed=False, out_dtype=bf16):
    bsz, l, _ = h.shape
    if transposed:
        rows = l // GRID_W
        hin = h.reshape(bsz, rows, GRID_W * D)
        tl = rows
        nt = GRID_W
        in_spec = pl.BlockSpec((None, tl, D), lambda b, i: (b, 0, i))
    else:
        tl = min(512, l)
        nt = l // tl
        hin = h
        in_spec = pl.BlockSpec((None, tl, D), lambda b, i: (b, i, 0))
    return pl.pallas_call(
        _norm_kernel,
        grid=(bsz, nt),
        in_specs=[in_spec,
                  pl.BlockSpec((1, D), lambda b, i: (0, 0)),
                  pl.BlockSpec((None, None, 1, D), lambda b, i: (row_of_batch(b), k_sc, 0, 0)),
                  pl.BlockSpec((None, None, 1, D), lambda b, i: (row_of_batch(b), k_sh, 0, 0))],
        out_specs=pl.BlockSpec((None, tl, D), lambda b, i: (b, i, 0)),
        out_shape=jax.ShapeDtypeStruct((bsz, l, D), out_dtype),
        compiler_params=_cparams(2),
        name="norm",
    )(hin, gain.reshape(1, D), mod4, mod4)


def _plain_norm_kernel(x_ref, g_ref, o_ref):
    x = x_ref[...]
    ms = jnp.mean(x * x, axis=-1, keepdims=True)
    o_ref[...] = (x * lax.rsqrt(ms + EPS) * g_ref[...]).astype(o_ref.dtype)


def _final_norm(h, gain):
    bsz, l, _ = h.shape
    tl = min(512, l)
    return pl.pallas_call(
        _plain_norm_kernel,
        grid=(bsz, l // tl),
        in_specs=[pl.BlockSpec((None, tl, D), lambda b, i: (b, i, 0)),
                  pl.BlockSpec((1, D), lambda b, i: (0, 0))],
        out_specs=pl.BlockSpec((None, tl, D), lambda b, i: (b, i, 0)),
        out_shape=jax.ShapeDtypeStruct((bsz, l, D), f32),
        compiler_params=_cparams(2),
        name="final_norm",
    )(h, gain.reshape(1, D))


def _proj_kernel(u_ref, w_ref, o_ref, wb_ref, *, act):
    @pl.when(pl.program_id(1) == 0)
    def _():
        wb_ref[...] = w_ref[...].astype(bf16)

    acc = jnp.dot(u_ref[...], wb_ref[...], preferred_element_type=f32)
    if act == "sigmoid":
        acc = _sigmoid(acc)
    o_ref[...] = acc.astype(o_ref.dtype)


def _proj(u, w, li, *, tn, col0=0, n=None, act=None, out_dtype=f32):
    m, k = u.shape
    n = w.shape[2] if n is None else n
    assert col0 % tn == 0 and n % tn == 0 and col0 + n <= w.shape[2]
    off = col0 // tn
    tm = min(1024, m)
    return pl.pallas_call(
        functools.partial(_proj_kernel, act=act),
        grid=(n // tn, m // tm),
        in_specs=[pl.BlockSpec((tm, k), lambda j, i: (i, 0)),
                  pl.BlockSpec((None, k, tn), lambda j, i: (li, 0, j + off))],
        out_specs=pl.BlockSpec((tm, tn), lambda j, i: (i, j)),
        out_shape=jax.ShapeDtypeStruct((m, n), out_dtype),
        scratch_shapes=[pltpu.VMEM((k, tn), bf16)],
        compiler_params=_cparams(2),
        name="proj",
    )(u, w)


def _lru_kernel(*refs, reverse, final, tl, nt):
    if final:
        (x_ref, prev_ref, next_ref, gate_ref, hb_ref, cw_ref, cb_ref, gw_ref, gb_ref, lam_ref, h0_ref,
         out_ref, hlast_ref, xe_ref, a_ref, bx_ref, hs_ref, h_ref) = refs
    else:
        (x_ref, prev_ref, next_ref, cw_ref, cb_ref, gw_ref, gb_ref, lam_ref, h0_ref,
         out_ref, hlast_ref, xe_ref, a_ref, bx_ref, hs_ref, h_ref) = refs
    i = pl.program_id(1)
    t = (nt - 1 - i) if reverse else i

    @pl.when(i == 0)
    def _():
        h_ref[...] = h0_ref[...]

    zero = jnp.zeros((HALO, D), f32)
    xe_ref[0:HALO, :] = jnp.where(t > 0, prev_ref[...], zero)
    xe_ref[HALO:HALO + tl, :] = x_ref[...]
    xe_ref[HALO + tl:2 * HALO + tl, :] = jnp.where(t < nt - 1, next_ref[...], zero)

    for n in range(LRU_BLOCKS):
        cs = slice(n * LRU_BS, (n + 1) * LRU_BS)
        xc = cb_ref[:, cs]
        for tap in range(CONV_W):
            r0 = HALO - CONV_LEFT + tap
            xc = xc + xe_ref[r0:r0 + tl, cs] * cw_ref[tap:tap + 1, cs]
        g = _mm(xc, gw_ref[n])
        rec = _sigmoid(g[:, :LRU_BS] + gb_ref[0:1, cs])
        inp = _sigmoid(g[:, LRU_BS:] + gb_ref[1:2, cs])
        log_a = -LRU_C * rec * _softplus(-lam_ref[:, cs])
        a_ref[:, cs] = jnp.exp(log_a)
        th = jnp.tanh(log_a)
        bx_ref[:, cs] = jnp.sqrt(-2.0 * th / (1.0 - th)) * inp * xc

    def body(s, h):
        tt = (tl - 1 - s) if reverse else s
        h = a_ref[pl.ds(tt, 1), :] * h + bx_ref[pl.ds(tt, 1), :]
        hs_ref[pl.ds(tt, 1), :] = h
        return h

    h = lax.fori_loop(0, tl, body, h_ref[...], unroll=8)
    h_ref[...] = h

    @pl.when(i == nt - 1)
    def _():
        hlast_ref[...] = h

    if final:
        for n in range(LRU_BLOCKS):
            cs = slice(n * LRU_BS, (n + 1) * LRU_BS)
            y = (hs_ref[:, cs] + hb_ref[:, cs]) * jax.nn.gelu(gate_ref[:, cs])
            out_ref[:, cs] = y.astype(out_ref.dtype)
    else:
        out_ref[...] = hs_ref[...]


def _lru_pass(p, cw, cb, gw, gb, lam, h0, hb, *, reverse):
    bsz, l, _ = p.shape
    tl = min(256, l)
    nt = l // tl
    final = hb is not None
    nh = l // HALO
    tpb = tl // HALO

    def tmap(i):
        return (nt - 1 - i) if reverse else i

    main = pl.BlockSpec((None, tl, D), lambda b, i: (b, tmap(i), 0))
    in_specs = [main,
                pl.BlockSpec((None, HALO, D), lambda b, i: (b, jnp.maximum(tmap(i) * tpb - 1, 0), 0)),
                pl.BlockSpec((None, HALO, D), lambda b, i: (b, jnp.minimum((tmap(i) + 1) * tpb, nh - 1), 0))]
    args = [p, p, p]
    if final:
        in_specs += [pl.BlockSpec((None, tl, D), lambda b, i: (b, tmap(i), 1)), main]
        args += [p, hb]
    in_specs += [pl.BlockSpec((CONV_W, D), lambda b, i: (0, 0)),
                 pl.BlockSpec((1, D), lambda b, i: (0, 0)),
                 pl.BlockSpec((LRU_BLOCKS, LRU_BS, 2 * LRU_BS), lambda b, i: (0, 0, 0)),
                 pl.BlockSpec((2, D), lambda b, i: (0, 0)),
                 pl.BlockSpec((1, D), lambda b, i: (0, 0)),
                 pl.BlockSpec((None, 1, D), lambda b, i: (b, 0, 0))]
    args += [cw, cb, gw, gb, lam, h0]
    out, hlast = pl.pallas_call(
        functools.partial(_lru_kernel, reverse=reverse, final=final, tl=tl, nt=nt),
        grid=(bsz, nt),
        in_specs=in_specs,
        out_specs=[main, pl.BlockSpec((None, 1, D), lambda b, i: (b, 0, 0))],
        out_shape=[jax.ShapeDtypeStruct((bsz, l, D), bf16 if final else f32),
                   jax.ShapeDtypeStruct((bsz, 1, D), f32)],
        scratch_shapes=[pltpu.VMEM((tl + 2 * HALO, D), f32), pltpu.VMEM((tl, D), f32),
                        pltpu.VMEM((tl, D), f32), pltpu.VMEM((tl, D), f32), pltpu.VMEM((1, D), f32)],
        compiler_params=_cparams(2),
        name="lru_fwd" if final else "lru_bwd",
    )(*args)
    return out, hlast


def _rwkv_feat_kernel(p_ref, prev_ref, next_ref, mu_ref, w0_ref, wup_ref, a0_ref, aup_ref, gup_ref,
                      kk_ref, ka_ref, rk_ref, ones_ref,
                      r_out, v_out, kkv_out, g_out, bonus_out, lw_out, key_out, b_out,
                      pe_ref, *, tl, nt):
    i = pl.program_id(1)
    zero = jnp.zeros((HALO, N_RWKV_PAD), f32)
    pe_ref[0:HALO, :] = jnp.where(i > 0, prev_ref[...], zero)
    pe_ref[HALO:HALO + tl, :] = p_ref[...]
    pe_ref[HALO + tl:2 * HALO + tl, :] = jnp.where(i < nt - 1, next_ref[...], zero)

    def shifted(c0, width):
        cs = slice(c0, c0 + width)
        p = pe_ref[HALO:HALO + tl, cs]
        nb = 0.5 * (pe_ref[HALO - 1:HALO - 1 + tl, cs] + pe_ref[HALO + 1:HALO + 1 + tl, cs])
        return p + mu_ref[:, cs] * (nb - p)

    win_w = jnp.tanh(shifted(RW_OFF_W, 256)).astype(bf16)
    win_a = shifted(RW_OFF_A, 256).astype(bf16)
    win_g = _sigmoid(shifted(RW_OFF_G, 256)).astype(bf16)
    ones_bd = ones_ref[...]

    for g in range(RW_NG):
        cs = slice(g * RW_GW, (g + 1) * RW_GW)
        r = shifted(g * RW_GW, RW_GW)
        k = shifted(D + g * RW_GW, RW_GW)
        v = shifted(2 * D + g * RW_GW, RW_GW)
        kf = k * kk_ref[:, cs]
        kk = kf * lax.rsqrt(_mm_x2(kf * kf, ones_bd) + 1e-12)
        ksum = None
        for d in range(2):
            wv = -_softplus(-(w0_ref[d:d + 1, cs] + jnp.dot(win_w, wup_ref[d, :, cs],
                                                              preferred_element_type=f32))) - 0.5
            lw_out[d, :, cs] = -jnp.exp(wv)
            a = _sigmoid(a0_ref[d:d + 1, cs] + jnp.dot(win_a, aup_ref[d, :, cs],
                                                       preferred_element_type=f32))
            key = k * (1.0 + (a - 1.0) * ka_ref[:, cs])
            key_out[d, :, cs] = key
            b_out[d, :, cs] = kk * a
            ksum = key if ksum is None else ksum + key
        r_out[:, cs] = r
        v_out[:, cs] = v
        kkv_out[:, cs] = kk
        bonus_out[:, cs] = _mm_x2(r * ksum * rk_ref[:, cs], ones_bd) * v
        g_out[:, cs] = jnp.dot(win_g, gup_ref[:, cs], preferred_element_type=f32)


def _rwkv_feat(p, prm):
    bsz, l, _ = p.shape
    tl = min(128, l)
    nt = l // tl
    nh = l // HALO
    tpb = tl // HALO
    full2 = lambda shape: pl.BlockSpec(shape, lambda b, i: (0,) * len(shape))
    tok = pl.BlockSpec((None, tl, D), lambda b, i: (b, i, 0))
    tok2 = pl.BlockSpec((2, None, tl, D), lambda b, i: (0, b, i, 0))
    sd = jax.ShapeDtypeStruct((bsz, l, D), f32)
    sd2 = jax.ShapeDtypeStruct((2, bsz, l, D), f32)
    return pl.pallas_call(
        functools.partial(_rwkv_feat_kernel, tl=tl, nt=nt),
        grid=(bsz, nt),
        in_specs=[pl.BlockSpec((None, tl, N_RWKV_PAD), lambda b, i: (b, i, 0)),
                  pl.BlockSpec((None, HALO, N_RWKV_PAD), lambda b, i: (b, jnp.maximum(i * tpb - 1, 0), 0)),
                  pl.BlockSpec((None, HALO, N_RWKV_PAD),
                               lambda b, i: (b, jnp.minimum((i + 1) * tpb, nh - 1), 0)),
                  full2((1, N_RWKV_PAD)), full2((2, D)), full2((2, 256, D)), full2((2, D)),
                  full2((2, 256, D)), full2((256, D)), full2((1, D)), full2((1, D)), full2((1, D)),
                  full2((RW_GW, RW_GW))],
        out_specs=[tok, tok, tok, tok, tok, tok2, tok2, tok2],
        out_shape=[sd, sd, sd, sd, sd, sd2, sd2, sd2],
        scratch_shapes=[pltpu.VMEM((tl + 2 * HALO, N_RWKV_PAD), f32)],
        compiler_params=_cparams(2),
        name="rwkv_feat",
    )(p, p, p, prm["mu"], prm["w0"], prm["wup"], prm["a0"], prm["aup"], prm["gup"],
      prm["k_k"], prm["k_a"], prm["r_k"], prm["ones_bd"])


def _bd_expand(y, lane_head):
    yb = y.astype(f32)
    return jnp.concatenate([jnp.where(lane_head == h, yb, 0.0).astype(bf16) for h in range(4)], axis=0)


def _rwkv_scan_kernel(r_ref, v_ref, kk_ref, lw_ref, key_ref, b_ref, s0_ref, y_ref, sfin_ref,
                      s_ref, *, nc):
    c = RW_CHUNK
    d = pl.program_id(0)
    i = pl.program_id(2)
    fwd = d == 0

    @pl.when(i == 0)
    def _():
        s_ref[...] = s0_ref[...]

    row = _iota((c, c), 0)
    col = _iota((c, c), 1)
    sgn = jnp.where(fwd, 1, -1)
    tri = jnp.where((col - row) * sgn <= 0, 1.0, 0.0).astype(bf16)
    t4 = _iota((c, 4 * c), 0)
    j4 = _iota((c, 4 * c), 1) % c
    mask_s = (j4 - t4) * sgn < 0
    mask_i = (j4 - t4) * sgn <= 0
    eye4 = jnp.where(j4 == t4, 1.0, 0.0)
    lane_head = _iota((c, RW_GW), 1) // HEAD
    bd_mask = (_iota((RW_GW, RW_GW), 0) // HEAD) == (_iota((RW_GW, RW_GW), 1) // HEAD)

    def mmbd(x, y):
        return jnp.dot(x.astype(bf16), _bd_expand(y, lane_head), preferred_element_type=f32)

    groups = range(RW_NG)
    sl = [slice(g * RW_GW, (g + 1) * RW_GW) for g in groups]

    def state_free(rows):
        lw = [lw_ref[rows, sl[g]] for g in groups]
        cl = [_mm_e3(tri, lw[g]) for g in groups]
        tot = [jnp.sum(lw[g], axis=0, keepdims=True) for g in groups]
        v = [v_ref[rows, sl[g]] for g in groups]
        ar, a_b, a_k, bk = [], [], [], []
        for g in groups:
            g_inv = jnp.exp(-cl[g])
            g_end = jnp.exp(tot[g] - cl[g])
            bv = b_ref[rows, sl[g]]
            key = key_ref[rows, sl[g]]
            at = -kk_ref[rows, sl[g]] * jnp.exp(cl[g] - lw[g])
            rt = r_ref[rows, sl[g]] * jnp.exp(cl[g])
            ar.append(jnp.concatenate([at, rt], axis=0).astype(bf16))
            a_b.append(lax.dot_general(ar[g], _bd_expand(bv * g_inv, lane_head), (((1,), (1,)), ((), ())),
                                       preferred_element_type=f32))
            a_k.append(lax.dot_general(ar[g], _bd_expand(key * g_inv, lane_head), (((1,), (1,)), ((), ())),
                                       preferred_element_type=f32))
            bk.append(jnp.concatenate([bv * g_end, key * g_end], axis=0).astype(bf16))
        n_ab = [jnp.where(mask_s, a_b[g][:c], 0.0) for g in groups]
        a_rb = [jnp.where(mask_i, a_b[g][c:], 0.0) for g in groups]
        a_kk = [jnp.concatenate([jnp.where(mask_s, a_k[g][:c], 0.0), jnp.where(mask_i, a_k[g][c:], 0.0)],
                                axis=0) for g in groups]
        x = [eye4 + n_ab[g] for g in groups]
        m = [mmbd(n_ab[g], n_ab[g]) for g in groups]
        lvl = 2
        while lvl < c:
            if lvl * 2 < c:
                xm = [mmbd(jnp.concatenate([x[g], m[g]], axis=0), m[g]) for g in groups]
                x = [x[g] + xm[g][:c] for g in groups]
                m = [xm[g][c:] for g in groups]
            else:
                x = [x[g] + mmbd(x[g], m[g]) for g in groups]
            lvl *= 2
        av = [mmbd(a_kk[g], v[g]) for g in groups]
        return dict(ar=ar, x=x, av=av, a_rb=a_rb, v=v, bk=bk, tot=tot)

    def state_step(rows, p, s):
        sa = [_mm_nt(p["ar"][g], s[g]) for g in groups]
        u = [mmbd(p["x"][g], sa[g][:c] + p["av"][g][:c]) for g in groups]
        y = [sa[g][c:] + p["av"][g][c:] + mmbd(p["a_rb"][g], u[g]) for g in groups]
        upd = [_mm(jnp.concatenate([u[g], p["v"][g]], axis=0).T, p["bk"][g]) for g in groups]
        for g in groups:
            y_ref[rows, sl[g]] = y[g]
        return [s[g] * jnp.exp(p["tot"][g]) + jnp.where(bd_mask, upd[g], 0.0) for g in groups]

    first = pl.multiple_of(jnp.where(fwd, 0, c), c)
    chunk_rows = [pl.ds(first, c), pl.ds(pl.multiple_of(c - first, c), c)]
    parts = [state_free(rows) for rows in chunk_rows]
    s = [s_ref[g] for g in groups]
    for rows, p in zip(chunk_rows, parts):
        s = state_step(rows, p, s)
    for g in groups:
        s_ref[g] = s[g]

    @pl.when(i == nc - 1)
    def _():
        sfin_ref[...] = s_ref[...]


def _rwkv_scan(r, v, kk, lw, key, bvec, s0):
    bsz, l, _ = r.shape
    c = 2 * RW_CHUNK
    nc = l // c

    def cidx(d, i):
        return jnp.where(d == 0, i, nc - 1 - i)

    tok = pl.BlockSpec((None, c, D), lambda d, b, i: (b, cidx(d, i), 0))
    tok2 = pl.BlockSpec((None, None, c, D), lambda d, b, i: (d, b, cidx(d, i), 0))
    st = pl.BlockSpec((None, None, RW_NG, RW_GW, RW_GW), lambda d, b, i: (d, b, 0, 0, 0))
    return pl.pallas_call(
        functools.partial(_rwkv_scan_kernel, nc=nc),
        grid=(2, bsz, nc),
        in_specs=[tok, tok, tok, tok2, tok2, tok2, st],
        out_specs=[tok2, st],
        out_shape=[jax.ShapeDtypeStruct((2, bsz, l, D), f32),
                   jax.ShapeDtypeStruct((2, bsz, RW_NG, RW_GW, RW_GW), f32)],
        scratch_shapes=[pltpu.VMEM((RW_NG, RW_GW, RW_GW), f32)],
        compiler_params=_cparams(3),
        name="rwkv_scan",
    )(r, v, kk, lw, key, bvec, s0)


def _rwkv_out_kernel(y_ref, bonus_ref, g_ref, lnw_ref, lnb_ref, ones_ref, o_ref):
    ones_bd = ones_ref[...]
    for g in range(RW_NG):
        cs = slice(g * RW_GW, (g + 1) * RW_GW)
        y = y_ref[0, :, cs] + y_ref[1, :, cs]
        mean = _mm_x2(y, ones_bd) * (1.0 / HEAD)
        yc = y - mean
        var = _mm_x2(yc * yc, ones_bd) * (1.0 / HEAD)
        yn = yc * lax.rsqrt(var + GN_EPS) * lnw_ref[:, cs] + lnb_ref[:, cs]
        o_ref[:, cs] = ((yn + bonus_ref[:, cs]) * g_ref[:, cs]).astype(o_ref.dtype)


def _rwkv_out(y2, bonus, g, ln_w, ln_b, ones_bd):
    _, bsz, l, _ = y2.shape
    tl = min(256, l)
    tok = pl.BlockSpec((None, tl, D), lambda b, i: (b, i, 0))
    vec = pl.BlockSpec((1, D), lambda b, i: (0, 0))
    return pl.pallas_call(
        _rwkv_out_kernel,
        grid=(bsz, l // tl),
        in_specs=[pl.BlockSpec((2, None, tl, D), lambda b, i: (0, b, i, 0)), tok, tok, vec, vec,
                  pl.BlockSpec((RW_GW, RW_GW), lambda b, i: (0, 0))],
        out_specs=tok,
        out_shape=jax.ShapeDtypeStruct((bsz, l, D), bf16),
        compiler_params=_cparams(2),
        name="rwkv_out",
    )(y2, bonus, g, ln_w, ln_b, ones_bd)


def _ssd_prep_kernel(p_ref, prev_ref, next_ref, cw_ref, cb_ref, o_ref, pe_ref, *, tl, nt):
    i = pl.program_id(1)
    w = p_ref.shape[1]
    zero = jnp.zeros((HALO, w), f32)
    pe_ref[0:HALO, :] = jnp.where(i > 0, prev_ref[...], zero)
    pe_ref[HALO:HALO + tl, :] = p_ref[...]
    pe_ref[HALO + tl:2 * HALO + tl, :] = jnp.where(i < nt - 1, next_ref[...], zero)
    for n in range(w // 128):
        cs = slice(n * 128, (n + 1) * 128)
        xc = cb_ref[:, cs]
        for tap in range(CONV_W):
            r0 = HALO - CONV_LEFT + tap
            xc = xc + pe_ref[r0:r0 + tl, cs] * cw_ref[tap:tap + 1, cs]
        o_ref[:, cs] = _silu(xc)


def _ssd_prep(p, cw, cb):
    bsz, l, _ = p.shape
    tl = min(256, l)
    nt = l // tl
    nh = l // HALO
    tpb = tl // HALO
    wc = 1024
    c0 = D // wc
    return pl.pallas_call(
        functools.partial(_ssd_prep_kernel, tl=tl, nt=nt),
        grid=(bsz, nt, D_XBC // wc),
        in_specs=[pl.BlockSpec((None, tl, wc), lambda b, i, c: (b, i, c + c0)),
                  pl.BlockSpec((None, HALO, wc), lambda b, i, c: (b, jnp.maximum(i * tpb - 1, 0), c + c0)),
                  pl.BlockSpec((None, HALO, wc),
                               lambda b, i, c: (b, jnp.minimum((i + 1) * tpb, nh - 1), c + c0)),
                  pl.BlockSpec((CONV_W, wc), lambda b, i, c: (0, c)),
                  pl.BlockSpec((1, wc), lambda b, i, c: (0, c))],
        out_specs=pl.BlockSpec((None, tl, wc), lambda b, i, c: (b, i, c)),
        out_shape=jax.ShapeDtypeStruct((bsz, l, D_XBC), f32),
        scratch_shapes=[pltpu.VMEM((tl + 2 * HALO, wc), f32)],
        compiler_params=_cparams(3),
        name="ssd_prep",
    )(p, p, p, cw, cb)


def _ssd_kernel(xbc_ref, dtw_ref, dtb_ref, a_ref, dsk_ref, e_ref,
                h0_ref, y_ref, hfin_ref, cumt_ref, dtt_ref, h_ref, *, nc):
    q = SSM_Q
    d = pl.program_id(0)
    i = pl.program_id(2)
    fwd = d == 0

    @pl.when(i == 0)
    def _():
        h_ref[...] = h0_ref[...]

    row = _iota((q, q), 0)
    col = _iota((q, q), 1)
    sgn = jnp.where(fwd, 1, -1)
    low = (col - row) * sgn <= 0
    tri = jnp.where(low, 1.0, 0.0).astype(bf16)
    tri_t = jnp.where((row - col) * sgn <= 0, 1.0, 0.0).astype(bf16)

    dt = _softplus(dtw_ref[...] + dtb_ref[...])
    dta = dt * a_ref[...]
    cum = _mm_e3(tri, dta)
    tot = jnp.sum(dta, axis=0, keepdims=True)
    cumt_ref[...] = _mm_x3(dta.T, tri_t)
    dtt_ref[...] = dt.T
    e_d = e_ref[...]
    dec_e = _mm_x2(jnp.exp(cum), e_d)
    toend_e = _mm_x2(jnp.exp(tot - cum) * dt, e_d)
    tot_e = _mm_x2(jnp.broadcast_to(jnp.exp(tot), (8, 128)), e_d)[0:1]
    cum_d = pltpu.roll(cum, jnp.where(fwd, 0, 128 - SSM_HEADS), 1)
    lane = _iota((q, 128), 1)

    for g in range(SSM_G):
        bg = xbc_ref[:, D + g * SSM_N:D + (g + 1) * SSM_N]
        cg = xbc_ref[:, D + SSM_G * SSM_N + g * SSM_N:D + SSM_G * SSM_N + (g + 1) * SSM_N]
        cb = _mm_nt(cg, bg)
        gs = slice(g * 512, (g + 1) * 512)
        hg = h_ref[g]
        y_off = _mm(cg, hg) * dec_e[:, gs]
        for pr in range(4):
            ls = []
            for hh in range(2):
                h = g * 8 + pr * 2 + hh
                rowv = cumt_ref[pl.ds(d * 32 + h, 1), :]
                dtr = dtt_ref[pl.ds(d * 32 + h, 1), :]
                seg = cum_d[:, h:h + 1] - rowv
                ls.append(jnp.where(low, jnp.exp(jnp.where(low, seg, 0.0)), 0.0) * cb * dtr)
            lp = jnp.concatenate(ls, axis=1)
            ps = slice(g * 512 + pr * 128, g * 512 + (pr + 1) * 128)
            xp = xbc_ref[:, ps]
            bd2 = jnp.concatenate([jnp.where(lane < SSM_P, xp, 0.0), jnp.where(lane >= SSM_P, xp, 0.0)],
                                  axis=0)
            yd = _mm(lp, bd2)
            y_ref[:, ps] = yd + y_off[:, pr * 128:(pr + 1) * 128] + dsk_ref[:, ps] * xp
        xs = xbc_ref[:, gs] * toend_e[:, gs]
        states = _mm(bg.T, xs)
        h_ref[g] = hg * tot_e[:, gs] + states

    @pl.when(i == nc - 1)
    def _():
        hfin_ref[...] = h_ref[...]


def _ssd(p, prm, h0):
    bsz, l, _ = p.shape
    q = SSM_Q
    nc = l // q
    xbc = _ssd_prep(p, prm["cw"], prm["cb"])

    def cidx(d, i):
        return jnp.where(d == 0, i, nc - 1 - i)

    full = lambda shape: pl.BlockSpec(shape, lambda d, b, i: (0,) * len(shape))
    st = pl.BlockSpec((None, None, SSM_G, SSM_N, 8 * SSM_P), lambda d, b, i: (d, b, 0, 0, 0))
    return pl.pallas_call(
        functools.partial(_ssd_kernel, nc=nc),
        grid=(2, bsz, nc),
        in_specs=[pl.BlockSpec((None, q, D_XBC), lambda d, b, i: (b, cidx(d, i), 0)),
                  pl.BlockSpec((None, q, 128), lambda d, b, i: (b, cidx(d, i), SS_OFF_DT // 128)),
                  full((1, 128)), full((1, 128)),
                  pl.BlockSpec((None, 1, D), lambda d, b, i: (d, 0, 0)),
                  pl.BlockSpec((None, 128, D), lambda d, b, i: (d, 0, 0)),
                  st],
        out_specs=[pl.BlockSpec((None, None, q, D), lambda d, b, i: (d, b, cidx(d, i), 0)), st],
        out_shape=[jax.ShapeDtypeStruct((2, bsz, l, D), f32),
                   jax.ShapeDtypeStruct((2, bsz, SSM_G, SSM_N, 8 * SSM_P), f32)],
        scratch_shapes=[pltpu.VMEM((128, q), f32), pltpu.VMEM((128, q), f32),
                        pltpu.VMEM((SSM_G, SSM_N, 8 * SSM_P), f32)],
        compiler_params=_cparams(3),
        name="ssd",
    )(xbc, p, prm["dtb"], prm["a"], prm["dsk"], prm["e"], h0)


def _ssd_out_kernel(y_ref, z_ref, nw_ref, o_ref):
    gw = D // SSM_G
    for g in range(SSM_G):
        cs = slice(g * gw, (g + 1) * gw)
        yg = (y_ref[0, :, cs] + y_ref[1, :, cs]) * _silu(z_ref[:, cs])
        ms = jnp.mean(yg * yg, axis=-1, keepdims=True)
        o_ref[:, cs] = (yg * lax.rsqrt(ms + EPS) * nw_ref[:, cs]).astype(o_ref.dtype)


def _ssd_out(y2, p, norm_w):
    _, bsz, l, _ = y2.shape
    tl = min(256, l)
    tok = pl.BlockSpec((None, tl, D), lambda b, i: (b, i, 0))
    return pl.pallas_call(
        _ssd_out_kernel,
        grid=(bsz, l // tl),
        in_specs=[pl.BlockSpec((2, None, tl, D), lambda b, i: (0, b, i, 0)), tok,
                  pl.BlockSpec((1, D), lambda b, i: (0, 0))],
        out_specs=tok,
        out_shape=jax.ShapeDtypeStruct((bsz, l, D), bf16),
        compiler_params=_cparams(2),
        name="ssd_out",
    )(y2, p, norm_w)


def _merge_kernel(ya_ref, yb_ref, yc_ref, sa_ref, sb_ref, sc_ref, wa_ref, wb_ref, wc_ref, o_ref):
    acc = None
    for y_ref, s_ref, w_ref in ((ya_ref, sa_ref, wa_ref), (yb_ref, sb_ref, wb_ref), (yc_ref, sc_ref, wc_ref)):
        t = s_ref[...].astype(f32) * jnp.dot(y_ref[...], w_ref[...], preferred_element_type=f32)
        acc = t if acc is None else acc + t
    o_ref[...] = acc.astype(o_ref.dtype)


def _merge(ys, sig, ws):
    m = ys[0].shape[0]
    tm = min(1024, m)
    tn = 512
    nj = D // tn
    ysp = pl.BlockSpec((tm, D), lambda i, j: (i, 0))
    wsp = pl.BlockSpec((D, tn), lambda i, j: (0, j))
    ssp = [pl.BlockSpec((tm, tn), functools.partial(lambda i, j, k: (i, k * nj + j), k=k)) for k in range(3)]
    return pl.pallas_call(
        _merge_kernel,
        grid=(m // tm, nj),
        in_specs=[ysp, ysp, ysp] + ssp + [wsp, wsp, wsp],
        out_specs=pl.BlockSpec((tm, tn), lambda i, j: (i, j)),
        out_shape=jax.ShapeDtypeStruct((m, D), bf16),
        compiler_params=_cparams(2),
        name="merge",
    )(ys[0], ys[1], ys[2], sig, sig, sig, ws[0], ws[1], ws[2])


def _wo_kernel(m_ref, w_ref, h_ref, g_ref, o_ref):
    o_ref[...] = h_ref[...] + g_ref[...] * jnp.dot(m_ref[...], w_ref[...], preferred_element_type=f32)


def _wo_residual(m, w_o, h, mod4, row_of_batch, k_gate, *, transposed):
    bsz, l, _ = h.shape
    if transposed:
        rows = l // GRID_W
        hv = h.reshape(bsz, rows, GRID_W * D)
        tl, nt = rows, GRID_W
        hspec = pl.BlockSpec((None, tl, D), lambda b, i: (b, 0, i))
        oshape = jax.ShapeDtypeStruct((bsz, rows, GRID_W * D), f32)
    else:
        tl = min(512, l)
        nt = l // tl
        hv = h
        hspec = pl.BlockSpec((None, tl, D), lambda b, i: (b, i, 0))
        oshape = jax.ShapeDtypeStruct((bsz, l, D), f32)
    out = pl.pallas_call(
        _wo_kernel,
        grid=(bsz, nt),
        in_specs=[pl.BlockSpec((None, tl, D), lambda b, i: (b, i, 0)),
                  pl.BlockSpec((D, D), lambda b, i: (0, 0)),
                  hspec,
                  pl.BlockSpec((None, None, 1, D), lambda b, i: (row_of_batch(b), k_gate, 0, 0))],
        out_specs=hspec,
        out_shape=oshape,
        compiler_params=_cparams(2),
        name="wo_residual",
    )(m, w_o, hv, mod4)
    return out.reshape(bsz, l, D)


def _router_kernel(x_ref, g_ref, sc_ref, sh_ref, r_ref, u_ref, gates_ref):
    x = x_ref[...]
    ms = jnp.mean(x * x, axis=-1, keepdims=True)
    u = (x * lax.rsqrt(ms + EPS) * g_ref[...]) * (1.0 + sc_ref[...]) + sh_ref[...]
    u_ref[...] = u.astype(u_ref.dtype)
    u0, u1, u2 = _split3(u)
    r0, r1, r2 = _split3(r_ref[...])
    dot = lambda a, b: jnp.dot(a, b, preferred_element_type=f32)
    logits = (dot(u0, r0) + (dot(u0, r1) + dot(u1, r0))
              + (dot(u1, r1) + dot(u0, r2) + dot(u2, r0)))
    lane = _iota(logits.shape, 1)
    neg = jnp.float32(-jnp.inf)
    lg = jnp.where(lane < N_EXPERTS, logits, neg)
    m1 = jnp.max(lg, axis=-1, keepdims=True)
    i1 = jnp.min(jnp.where(lg == m1, lane, 128), axis=-1, keepdims=True)
    lg2 = jnp.where(lane == i1, neg, lg)
    m2 = jnp.max(lg2, axis=-1, keepdims=True)
    i2 = jnp.min(jnp.where(lg2 == m2, lane, 128), axis=-1, keepdims=True)
    e2 = jnp.exp(m2 - m1)
    den = 1.0 + e2
    gates_ref[...] = jnp.where(lane == i1, 1.0 / den, 0.0) + jnp.where(lane == i2, e2 / den, 0.0)


def _norm_router(h, gain, mod4, row_of_batch, k_sc, k_sh, router_pad, out_dtype=bf16):
    bsz, l, _ = h.shape
    tl = min(512, l)
    tok = pl.BlockSpec((None, tl, D), lambda b, i: (b, i, 0))
    return pl.pallas_call(
        _router_kernel,
        grid=(bsz, l // tl),
        in_specs=[tok,
                  pl.BlockSpec((1, D), lambda b, i: (0, 0)),
                  pl.BlockSpec((None, None, 1, D), lambda b, i: (row_of_batch(b), k_sc, 0, 0)),
                  pl.BlockSpec((None, None, 1, D), lambda b, i: (row_of_batch(b), k_sh, 0, 0)),
                  pl.BlockSpec((D, 128), lambda b, i: (0, 0))],
        out_specs=[tok, pl.BlockSpec((None, tl, 128), lambda b, i: (b, i, 0))],
        out_shape=[jax.ShapeDtypeStruct((bsz, l, D), out_dtype), jax.ShapeDtypeStruct((bsz, l, 128), f32)],
        compiler_params=_cparams(2),
        name="norm_router",
    )(h, gain.reshape(1, D), mod4, mod4, router_pad)


def _up_kernel(*refs, gated):
    if gated:
        u_ref, w1_ref, w3_ref, gt_ref, o_ref, w1b_ref, w3b_ref = refs
    else:
        u_ref, w1_ref, w3_ref, o_ref, w1b_ref, w3b_ref = refs

    @pl.when(pl.program_id(2) == 0)
    def _():
        w1b_ref[...] = w1_ref[...].astype(bf16)
        w3b_ref[...] = w3_ref[...].astype(bf16)

    u = u_ref[...]
    a = jnp.dot(u, w1b_ref[...], preferred_element_type=f32)
    b = jnp.dot(u, w3b_ref[...], preferred_element_type=f32)
    act = _silu(a) * b
    if gated:
        e = pl.program_id(0)
        gt = gt_ref[...]
        lane = _iota(gt.shape, 1)
        act = act * jnp.sum(jnp.where(lane == e, gt, 0.0), axis=-1, keepdims=True)
    o_ref[...] = act.astype(o_ref.dtype)


def _swiglu_up(u, w1, w3, lead, gates=None):
    m = u.shape[0]
    _, ne, _, ff = w1.shape
    tm = min(1024, m)
    tn = 512
    nj = ff // tn
    gated = gates is not None
    wspec = pl.BlockSpec((None, None, D, tn), lambda e, j, i: (lead, e, 0, j))
    in_specs = [pl.BlockSpec((tm, D), lambda e, j, i: (i, 0)), wspec, wspec]
    args = [u, w1, w3]
    if gated:
        in_specs.append(pl.BlockSpec((tm, 128), lambda e, j, i: (i, 0)))
        args.append(gates)
    return pl.pallas_call(
        functools.partial(_up_kernel, gated=gated),
        grid=(ne, nj, m // tm),
        in_specs=in_specs,
        out_specs=pl.BlockSpec((tm, tn), lambda e, j, i: (i, e * nj + j)),
        out_shape=jax.ShapeDtypeStruct((m, ne * ff), bf16),
        scratch_shapes=[pltpu.VMEM((D, tn), bf16), pltpu.VMEM((D, tn), bf16)],
        compiler_params=_cparams(3),
        name="swiglu_up",
    )(*args)


def _down_kernel(a_ref, w_ref, h_ref, g_ref, o_ref, acc_ref, *, nk):
    k = pl.program_id(3)

    @pl.when(k == 0)
    def _():
        acc_ref[...] = jnp.zeros_like(acc_ref)

    acc_ref[...] += jnp.dot(a_ref[...], w_ref[...].astype(bf16), preferred_element_type=f32)

    @pl.when(k == nk - 1)
    def _():
        o_ref[...] = h_ref[...] + g_ref[...] * acc_ref[...]


def _down_residual(act, w2, h, mod4, row_of_batch, k_gate):
    bsz, l, kk = act.shape
    tl = min(1024, l)
    tn = 1024
    tk = 2048
    nk = kk // tk
    return pl.pallas_call(
        functools.partial(_down_kernel, nk=nk),
        grid=(bsz, l // tl, D // tn, nk),
        in_specs=[pl.BlockSpec((None, tl, tk), lambda b, i, j, k: (b, i, k)),
                  pl.BlockSpec((tk, tn), lambda b, i, j, k: (k, j)),
                  pl.BlockSpec((None, tl, tn), lambda b, i, j, k: (b, i, j)),
                  pl.BlockSpec((None, None, 1, tn), lambda b, i, j, k: (row_of_batch(b), k_gate, 0, j))],
        out_specs=pl.BlockSpec((None, tl, tn), lambda b, i, j, k: (b, i, j)),
        out_shape=jax.ShapeDtypeStruct((bsz, l, D), f32),
        scratch_shapes=[pltpu.VMEM((tl, tn), f32)],
        compiler_params=_cparams(4),
        name="down_residual",
    )(act, w2, h, mod4)


MOE_TM = 512
MOE_TT = 512
NACT_LANE = 127


def _moe_count_kernel(gates_ref, pre_ref, cnt_ref, carry_ref):
    i = pl.program_id(0)

    @pl.when(i == 0)
    def _():
        carry_ref[...] = jnp.zeros_like(carry_ref)

    tt = gates_ref.shape[0]
    a = jnp.where(gates_ref[...] > 0.0, 1.0, 0.0)
    strict = jnp.where(_iota((tt, tt), 1) < _iota((tt, tt), 0), 1.0, 0.0).astype(bf16)
    carry = carry_ref[0:1, :]
    pre_ref[...] = jnp.dot(strict, a.astype(bf16), preferred_element_type=f32) + carry
    carry_ref[...] = jnp.broadcast_to(carry + jnp.sum(a, axis=0, keepdims=True), carry_ref.shape)
    cnt_ref[...] = carry_ref[...]


def _moe_place_kernel(gates_ref, pre_ref, cnt_ref, pos_ref, gv_ref, tile_ref, *, n_tiles):
    tt = gates_ref.shape[0]
    tm = float(MOE_TM)
    lane1 = _iota((1, 128), 1)
    cnt = cnt_ref[0:1, :]
    gsz = jnp.floor((cnt + (tm - 1.0)) * (1.0 / tm)) * tm
    upper = jnp.where(_iota((128, 128), 0) < _iota((128, 128), 1), 1.0, 0.0).astype(bf16)
    base = _mm_x3(jnp.broadcast_to(gsz, (8, 128)), upper)[0:1]
    gates = gates_ref[...]
    act = gates > 0.0
    pos = base + pre_ref[...]
    big = jnp.float32(1e9)
    p_lo = jnp.min(jnp.where(act, pos, big), axis=-1, keepdims=True)
    p_hi = jnp.max(jnp.where(act, pos, -1.0), axis=-1, keepdims=True)
    g_lo = jnp.sum(jnp.where(act & (pos == p_lo), gates, 0.0), axis=-1, keepdims=True)
    g_hi = jnp.where(p_hi != p_lo,
                     jnp.sum(jnp.where(act & (pos == p_hi), gates, 0.0), axis=-1, keepdims=True), 0.0)
    lane = _iota((tt, 128), 1)
    posf = jnp.where(lane == 0, p_lo, 0.0) + jnp.where(lane == 1, p_hi, 0.0)
    pos_ref[...] = posf.T[0:8].astype(jnp.int32)
    gv_ref[...] = jnp.where(lane == 0, g_lo, 0.0) + jnp.where(lane == 1, g_hi, 0.0)
    end = base + gsz
    start_j = lane1.astype(f32) * tm
    texp = jnp.zeros((1, 128), f32)
    for e in range(N_EXPERTS - 1):
        end_e = jnp.sum(jnp.where(lane1 == e, end, 0.0), axis=-1, keepdims=True)
        texp = texp + jnp.where(start_j >= end_e, 1.0, 0.0)
    total = jnp.sum(jnp.where(lane1 == N_EXPERTS - 1, end, 0.0), axis=-1, keepdims=True)
    table = jnp.where(lane1 == NACT_LANE, total * (1.0 / tm), texp)
    tile_ref[...] = jnp.broadcast_to(table, (8, 128)).astype(jnp.int32)


def _moe_scatter_kernel(pos_ref, v_ref, xs_in_ref, xs_ref, sem, *, dump_base):
    del xs_in_ref
    tt = v_ref.shape[0]

    def row_copy(r, dst_row):
        return pltpu.make_async_copy(v_ref.at[pl.ds(r, 1)], xs_ref.at[pl.ds(dst_row, 1)], sem)

    def body(r, carry):
        p0 = pos_ref[0, r]
        p1 = pos_ref[1, r]
        p1 = jnp.where(p1 == p0, dump_base + r, p1)
        row_copy(r, p0).start()
        row_copy(r, p1).start()
        return carry

    lax.fori_loop(0, tt, body, 0, unroll=8)
    for _ in range(2):
        pltpu.make_async_copy(v_ref, xs_ref.at[pl.ds(0, tt)], sem).wait()


def _new_expert_tile(tile_ref, t):
    return (t == 0) | (tile_ref[t] != tile_ref[jnp.maximum(t - 1, 0)])


def _moe_gup_kernel(tile_ref, x_ref, w1_ref, w3_ref, o_ref, w1b_ref, w3b_ref):
    t = pl.program_id(1)
    active = t < tile_ref[NACT_LANE]

    @pl.when(active & _new_expert_tile(tile_ref, t))
    def _():
        w1b_ref[...] = w1_ref[...].astype(bf16)
        w3b_ref[...] = w3_ref[...].astype(bf16)

    @pl.when(active)
    def _():
        x = x_ref[...].astype(bf16)
        a = jnp.dot(x, w1b_ref[...], preferred_element_type=f32)
        b = jnp.dot(x, w3b_ref[...], preferred_element_type=f32)
        o_ref[...] = (_silu(a) * b).astype(o_ref.dtype)

    @pl.when(jnp.logical_not(active))
    def _():
        o_ref[...] = jnp.zeros_like(o_ref)


def _moe_gdown_kernel(tile_ref, a_ref, w_ref, o_ref, wb_ref):
    t = pl.program_id(1)
    active = t < tile_ref[NACT_LANE]

    @pl.when(active & _new_expert_tile(tile_ref, t))
    def _():
        wb_ref[...] = w_ref[...].astype(bf16)

    @pl.when(active)
    def _():
        o_ref[...] = jnp.dot(a_ref[...], wb_ref[...], preferred_element_type=f32)

    @pl.when(jnp.logical_not(active))
    def _():
        o_ref[...] = jnp.zeros_like(o_ref)


def _moe_combine_kernel(*refs, final):
    if final:
        pos_ref, gv_ref, h_ref, gf_ref, fg_ref, ys_ref, o_ref, buf0, buf1, sem = refs
    else:
        pos_ref, gv_ref, h_ref, gf_ref, ys_ref, o_ref, buf0, buf1, sem = refs
    tt = h_ref.shape[0]

    def row_copy(src_row, buf, r):
        return pltpu.make_async_copy(ys_ref.at[pl.ds(src_row, 1)], buf.at[pl.ds(r, 1)], sem)

    def body(r, carry):
        row_copy(pos_ref[0, r], buf0, r).start()
        row_copy(pos_ref[1, r], buf1, r).start()
        return carry

    lax.fori_loop(0, tt, body, 0, unroll=8)
    for buf in (buf0, buf1):
        pltpu.make_async_copy(ys_ref.at[pl.ds(0, tt)], buf, sem).wait()
    gv = gv_ref[...]
    lane = _iota(gv.shape, 1)
    g0 = jnp.sum(jnp.where(lane == 0, gv, 0.0), axis=-1, keepdims=True)
    g1 = jnp.sum(jnp.where(lane == 1, gv, 0.0), axis=-1, keepdims=True)
    out = h_ref[...] + gf_ref[...] * (g0 * buf0[...] + g1 * buf1[...])
    if final:
        ms = jnp.mean(out * out, axis=-1, keepdims=True)
        out = out * lax.rsqrt(ms + EPS) * fg_ref[...]
    o_ref[...] = out


def _moe_sparse(v32, gates, h, w1, w3, w2, lead, mod4, row_of_batch, k_gate, final_gain=None):
    bsz, l, _ = h.shape
    n_tok = bsz * l
    tt = min(MOE_TT, n_tok)
    n_tt = n_tok // tt
    n_tiles = (2 * n_tok) // MOE_TM + N_EXPERTS
    rows = n_tiles * MOE_TM
    ff = w1.shape[3]
    vf = v32.reshape(n_tok, D)
    gf = gates.reshape(n_tok, 128)

    pre, cnt = pl.pallas_call(
        _moe_count_kernel,
        grid=(n_tt,),
        in_specs=[pl.BlockSpec((tt, 128), lambda i: (i, 0))],
        out_specs=[pl.BlockSpec((tt, 128), lambda i: (i, 0)), pl.BlockSpec((8, 128), lambda i: (0, 0))],
        out_shape=[jax.ShapeDtypeStruct((n_tok, 128), f32), jax.ShapeDtypeStruct((8, 128), f32)],
        scratch_shapes=[pltpu.VMEM((8, 128), f32)],
        compiler_params=_cparams(1),
        name="moe_count",
    )(gf)

    pos, gv, table = pl.pallas_call(
        functools.partial(_moe_place_kernel, n_tiles=n_tiles),
        grid=(n_tt,),
        in_specs=[pl.BlockSpec((tt, 128), lambda i: (i, 0)), pl.BlockSpec((tt, 128), lambda i: (i, 0)),
                  pl.BlockSpec((8, 128), lambda i: (0, 0))],
        out_specs=[pl.BlockSpec((8, tt), lambda i: (0, i)), pl.BlockSpec((tt, 128), lambda i: (i, 0)),
                   pl.BlockSpec((8, 128), lambda i: (0, 0))],
        out_shape=[jax.ShapeDtypeStruct((8, n_tok), jnp.int32), jax.ShapeDtypeStruct((n_tok, 128), f32),
                   jax.ShapeDtypeStruct((8, 128), jnp.int32)],
        compiler_params=_cparams(1),
        name="moe_place",
    )(gf, pre, cnt)
    tile_tab = table[0]

    smem_pos = pl.BlockSpec((8, tt), lambda i: (0, i), memory_space=pltpu.SMEM)
    xs = pl.pallas_call(
        functools.partial(_moe_scatter_kernel, dump_base=rows),
        grid=(n_tt,),
        in_specs=[smem_pos, pl.BlockSpec((tt, D), lambda i: (i, 0)), pl.BlockSpec(memory_space=pl.ANY)],
        out_specs=pl.BlockSpec(memory_space=pl.ANY),
        out_shape=jax.ShapeDtypeStruct((rows + tt, D), f32),
        scratch_shapes=[pltpu.SemaphoreType.DMA],
        input_output_aliases={2: 0},
        compiler_params=_cparams(1),
        name="moe_scatter",
    )(pos, vf, jnp.zeros((rows + tt, D), f32))

    assert n_tiles < NACT_LANE
    tn = 512
    nj = ff // tn
    last_tile = lambda tab: jnp.maximum(tab[NACT_LANE] - 1, 0)
    act = pl.pallas_call(
        _moe_gup_kernel,
        grid_spec=pltpu.PrefetchScalarGridSpec(
            num_scalar_prefetch=1,
            grid=(nj, n_tiles),
            in_specs=[pl.BlockSpec((MOE_TM, D), lambda j, t, tab: (jnp.minimum(t, last_tile(tab)), 0)),
                      pl.BlockSpec((None, None, D, tn),
                                   lambda j, t, tab: (lead, tab[jnp.minimum(t, last_tile(tab))], 0, j)),
                      pl.BlockSpec((None, None, D, tn),
                                   lambda j, t, tab: (lead, tab[jnp.minimum(t, last_tile(tab))], 0, j))],
            out_specs=pl.BlockSpec((MOE_TM, tn), lambda j, t, tab: (t, j)),
            scratch_shapes=[pltpu.VMEM((D, tn), bf16), pltpu.VMEM((D, tn), bf16)]),
        out_shape=jax.ShapeDtypeStruct((rows, ff), bf16),
        compiler_params=_cparams(2),
        name="moe_up",
    )(tile_tab, xs, w1, w3)

    tnd = 1024
    ys = pl.pallas_call(
        _moe_gdown_kernel,
        grid_spec=pltpu.PrefetchScalarGridSpec(
            num_scalar_prefetch=1,
            grid=(D // tnd, n_tiles),
            in_specs=[pl.BlockSpec((MOE_TM, ff), lambda j, t, tab: (jnp.minimum(t, last_tile(tab)), 0)),
                      pl.BlockSpec((None, None, ff, tnd),
                                   lambda j, t, tab: (lead, tab[jnp.minimum(t, last_tile(tab))], 0, j))],
            out_specs=pl.BlockSpec((MOE_TM, tnd), lambda j, t, tab: (t, j)),
            scratch_shapes=[pltpu.VMEM((ff, tnd), bf16)]),
        out_shape=jax.ShapeDtypeStruct((rows, D), f32),
        compiler_params=_cparams(2),
        name="moe_down",
    )(tile_tab, act, w2)

    tpb = l // tt if l >= tt else 1
    final = final_gain is not None
    extra_specs = [pl.BlockSpec((1, D), lambda i: (0, 0))] if final else []
    extra_args = [final_gain.reshape(1, D)] if final else []
    out = pl.pallas_call(
        functools.partial(_moe_combine_kernel, final=final),
        grid=(n_tt,),
        in_specs=[smem_pos, pl.BlockSpec((tt, 128), lambda i: (i, 0)), pl.BlockSpec((tt, D), lambda i: (i, 0)),
                  pl.BlockSpec((None, None, 1, D), lambda i: (row_of_batch(i // tpb), k_gate, 0, 0))]
                 + extra_specs + [pl.BlockSpec(memory_space=pl.ANY)],
        out_specs=pl.BlockSpec((tt, D), lambda i: (i, 0)),
        out_shape=jax.ShapeDtypeStruct((n_tok, D), f32),
        scratch_shapes=[pltpu.VMEM((tt, D), f32), pltpu.VMEM((tt, D), f32), pltpu.SemaphoreType.DMA],
        compiler_params=_cparams(1),
        name="moe_combine",
    )(pos, gv, h.reshape(n_tok, D), mod4, *extra_args, ys)
    return out.reshape(bsz, l, D)


def _block_ones(n, bs):
    idx = jnp.arange(n) // bs
    return (idx[:, None] == idx[None, :]).astype(bf16)


def _pad_cols(w, n):
    return jnp.pad(w, ((0, 0), (0, n - w.shape[1])))


def _rwkv_params(li, mu, w0, w_up, a0, a_up, g_up, k_k, k_a, r_k, ln_w, ln_b):
    wup = jnp.zeros((2, 256, D), f32)
    aup = jnp.zeros((2, 256, D), f32)
    for d in range(2):
        wup = wup.at[d, d * W_LORA:(d + 1) * W_LORA].set(w_up[li, d])
        o = (RW_OFF_G - 2 * A_LORA) - RW_OFF_A + d * A_LORA
        aup = aup.at[d, o:o + A_LORA].set(a_up[li, d])
    return {
        "mu": jnp.pad(mu[li], (0, N_RWKV_PAD - N_RWKV)).reshape(1, N_RWKV_PAD),
        "w0": w0[li], "wup": wup.astype(bf16), "a0": a0[li], "aup": aup.astype(bf16),
        "gup": g_up[li].astype(bf16), "k_k": k_k[li].reshape(1, D), "k_a": k_a[li].reshape(1, D),
        "r_k": r_k[li].reshape(1, D), "ln_w": ln_w[li].reshape(1, D), "ln_b": ln_b[li].reshape(1, D),
        "ones_bd": _block_ones(RW_GW, HEAD),
    }


def _ssd_params(li, conv_w, conv_b, a_log, dt_bias, d_skip):
    a = -jnp.exp(a_log[li].astype(f32))
    hid = jnp.arange(D) // SSM_P
    sel = jnp.arange(128)[:, None]
    e = jnp.stack([(sel == hid[None, :] + d * SSM_HEADS) for d in range(2)]).astype(bf16)
    return {
        "cw": conv_w[li], "cb": conv_b[li].reshape(1, D_XBC),
        "dtb": jnp.pad(dt_bias[li].reshape(1, 2 * SSM_HEADS), ((0, 0), (0, 64))),
        "a": jnp.pad(a.reshape(1, 2 * SSM_HEADS), ((0, 0), (0, 64))),
        "dsk": jnp.stack([jnp.repeat(d_skip[li], SSM_P), jnp.zeros((D,), f32)]).reshape(2, 1, D),
        "e": e,
    }


def kernel(x, c, ctx, c_ctx, ada_w, ada_b, norm_mix, norm_ffn, norm_final, w_in, lru_conv_w, lru_conv_b, lru_gate_w, lru_gate_b, lru_lambda, rwkv_mu, rwkv_w0, rwkv_w_up, rwkv_a0, rwkv_a_up, rwkv_g_up, rwkv_k_k, rwkv_k_a, rwkv_r_k, rwkv_ln_w, rwkv_ln_b, ssm_conv_w, ssm_conv_b, ssm_a_log, ssm_dt_bias, ssm_d, ssm_norm_w, w_out_lru, w_out_rwkv, w_out_ssm, w_o, ffn_w1, ffn_w3, ffn_w2, moe_router, moe_w1, moe_w3, moe_w2):
    bsz, l, _ = x.shape
    depth = ada_w.shape[0]
    off_lru = 3 * D
    off_rwkv = off_lru + 2 * D
    off_ssm = off_rwkv + N_RWKV
    cvec = jnp.zeros((8, D), f32).at[:bsz].set(c).at[bsz].set(c_ctx)
    lat_row = lambda b: b
    ctx_row = lambda b: bsz

    h_lat, h_ctx = x, ctx
    for li in range(depth):
        last = li == depth - 1
        odd = li % 2 == 1
        mod4 = _ada(cvec, ada_w, ada_b, li).reshape(8, 6, 1, D)

        w_ss = _pad_cols(w_in[li, :, off_ssm:], N_SSM_PAD)[None]
        lru_gw = [jnp.concatenate([lru_gate_w[li, d, 0], lru_gate_w[li, d, 1]], axis=-1).astype(bf16)
                  for d in range(2)]
        rp = _rwkv_params(li, rwkv_mu, rwkv_w0, rwkv_w_up, rwkv_a0, rwkv_a_up, rwkv_g_up, rwkv_k_k,
                          rwkv_k_a, rwkv_r_k, rwkv_ln_w, rwkv_ln_b)
        sp = _ssd_params(li, ssm_conv_w, ssm_conv_b, ssm_a_log, ssm_dt_bias, ssm_d)
        w_outs = [w_out_lru[li].astype(bf16), w_out_rwkv[li].astype(bf16), w_out_ssm[li].astype(bf16)]
        w_o_b = w_o[li].astype(bf16)

        def token_mix(u, states, need_out):
            lx = u.shape[1]
            um = u.reshape(bsz * lx, D)
            p_lru = _proj(um, w_in, li, tn=1024, col0=off_lru, n=2 * D).reshape(bsz, lx, 2 * D)
            p_rw = _proj(um, w_in, li, tn=1024, col0=off_rwkv, n=N_RWKV_PAD).reshape(bsz, lx, N_RWKV_PAD)
            p_ss = _proj(um, w_ss, 0, tn=768).reshape(bsz, lx, N_SSM_PAD)
            lru_s, rw_s, ss_s = states
            cw, cb = lru_conv_w[li], lru_conv_b[li].reshape(1, D)
            hb, hl_b = _lru_pass(p_lru, cw, cb, lru_gw[1], lru_gate_b[li, 1], lru_lambda[li, 1].reshape(1, D),
                                 lru_s[1], None, reverse=True)
            ya, hl_f = _lru_pass(p_lru, cw, cb, lru_gw[0], lru_gate_b[li, 0], lru_lambda[li, 0].reshape(1, D),
                                 lru_s[0], hb, reverse=False)
            r, v, kk, g, bonus, lw, key, bvec = _rwkv_feat(p_rw, rp)
            y2, rw_fin = _rwkv_scan(r, v, kk, lw, key, bvec, rw_s)
            ys2, ss_fin = _ssd(p_ss, sp, ss_s)
            new_states = ((hl_f, hl_b), rw_fin, ss_fin)
            if not need_out:
                return None, new_states
            yb = _rwkv_out(y2, bonus, g, rp["ln_w"], rp["ln_b"], rp["ones_bd"])
            yc = _ssd_out(ys2, p_ss, ssm_norm_w[li].reshape(1, D))
            sig = _proj(um, w_in, li, tn=1024, col0=0, n=off_lru, act="sigmoid", out_dtype=bf16)
            m = _merge([ya.reshape(-1, D), yb.reshape(-1, D), yc.reshape(-1, D)], sig, w_outs)
            return m.reshape(bsz, lx, D), new_states

        zero_states = ((jnp.zeros((bsz, 1, D), f32), jnp.zeros((bsz, 1, D), f32)),
                       jnp.zeros((2, bsz, RW_NG, RW_GW, RW_GW), f32),
                       jnp.zeros((2, bsz, SSM_G, SSM_N, 8 * SSM_P), f32))

        u_ctx = _norm(h_ctx, norm_mix[li], mod4, ctx_row, 1, 0)
        u_lat = _norm(h_lat, norm_mix[li], mod4, lat_row, 1, 0, transposed=odd)
        m_ctx, ctx_states = token_mix(u_ctx, zero_states, not last)
        m_lat, _ = token_mix(u_lat, ctx_states, True)
        h_lat = _wo_residual(m_lat, w_o_b, h_lat, mod4, lat_row, 2, transposed=odd)
        if not last:
            h_ctx = _wo_residual(m_ctx, w_o_b, h_ctx, mod4, ctx_row, 2, transposed=False)

        j = li // 2
        streams = [(h_lat, lat_row)] + ([] if last else [(h_ctx, ctx_row)])
        outs = []
        for h, row_fn in streams:
            lx = h.shape[1]
            if not odd:
                v = _norm(h, norm_ffn[li], mod4, row_fn, 4, 3)
                act = _swiglu_up(v.reshape(bsz * lx, D), ffn_w1[:, None], ffn_w3[:, None], j)
                w2 = ffn_w2[j].astype(bf16)
            elif row_fn is lat_row:
                router_pad = _pad_cols(moe_router[j], 128)
                v32, gates = _norm_router(h, norm_ffn[li], mod4, row_fn, 4, 3, router_pad, out_dtype=f32)
                outs.append(_moe_sparse(v32, gates, h, moe_w1, moe_w3, moe_w2, j, mod4, row_fn, 5,
                                        final_gain=norm_final if last else None))
                continue
            else:
                router_pad = _pad_cols(moe_router[j], 128)
                v, gates = _norm_router(h, norm_ffn[li], mod4, row_fn, 4, 3, router_pad)
                act = _swiglu_up(v.reshape(bsz * lx, D), moe_w1, moe_w3, j, gates.reshape(bsz * lx, 128))
                w2 = moe_w2[j].astype(bf16).reshape(N_EXPERTS * D_FF_EXPERT, D)
            outs.append(_down_residual(act.reshape(bsz, lx, -1), w2, h, mod4, row_fn, 5))
        h_lat = outs[0]
        if not last:
            h_ctx = outs[1]
    if depth % 2 == 0:
        return h_lat
    return _final_norm(h_lat, norm_final)
```
